```python
import jax, jax.numpy as jnp
from jax import lax
import numpy as np

D_MODEL = 1024
BATCH = 32
SEQ = 256
DEPTH = 4
DEC_BATCH = 4
DEC_SEQ = 2048
PAST_LEN = 512

GRID_W = 64
N_MIXERS = 2
RET_HEADS = 4
RET_DK = 256
RET_DV = 512
RET_CHUNK = 128
MLA_HEADS = 8
MLA_Q_RANK = 512
MLA_KV_RANK = 256
MLA_NOPE = 128
MLA_ROPE = 64
MLA_V = 128
ROPE_BASE = 10000.0
ATTN_BLOCK = 128
N_EXPERTS = 16
N_GROUPS = 4
EXPERTS_PER_GROUP = N_EXPERTS // N_GROUPS
TOP_K = 2
GROUP_SCORE_K = 2
EXPERT_FF = 256
EPS = 1e-6

kernel_name = "hybrid_retention_mla_moe_diffusion_step"

F32 = jnp.float32


def rmsnorm(x, g):
    xf = x.astype(F32)
    y = xf * lax.rsqrt(jnp.mean(xf * xf, axis=-1, keepdims=True) + EPS)
    return (y * g.astype(F32)).astype(x.dtype)


def head_layernorm(y, g):
    B, H, T, dv = y.shape
    yf = y.astype(F32)
    mu = jnp.mean(yf, axis=-1, keepdims=True)
    var = jnp.mean(jnp.square(yf - mu), axis=-1, keepdims=True)
    yn = ((yf - mu) * lax.rsqrt(var + EPS)).transpose(0, 2, 1, 3).reshape(B, T, H * dv)
    return (yn * g.astype(F32)).astype(y.dtype)


def ada_params(cond, w_ada, b_ada):
    m = jax.nn.silu(cond) @ w_ada + b_ada
    return jnp.split(m[:, None, :], 6, axis=-1)


def retention_scan(q, k, v, log_gamma, s0):
    B, H, T, dk = q.shape
    dv = v.shape[-1]
    C = RET_CHUNK
    nc = T // C
    lg = log_gamma.astype(F32)[:, None]
    pos = jnp.arange(C, dtype=F32)
    diff = pos[:, None] - pos[None, :]
    intra = jnp.where(diff >= 0, jnp.exp(lg[:, :, None] * jnp.maximum(diff, 0.0)), 0.0).astype(q.dtype)
    q_dec = jnp.exp(lg * (pos + 1.0)).astype(q.dtype)[..., None]
    k_dec = jnp.exp(lg * (C - 1.0 - pos)).astype(q.dtype)[..., None]
    c_dec = jnp.exp(lg * C).astype(q.dtype)[:, :, None]

    def chunks(a):
        return jnp.moveaxis(a.reshape(B, H, nc, C, a.shape[-1]), 2, 0)

    def step(s, xs):
        qc, kc, vc = xs
        att = jnp.einsum('bhnd,bhmd->bhnm', qc, kc) * intra
        o = jnp.einsum('bhnm,bhmv->bhnv', att, vc) + jnp.einsum('bhnd,bhdv->bhnv', qc * q_dec, s)
        s = s * c_dec + jnp.einsum('bhmd,bhmv->bhdv', kc * k_dec, vc)
        return s, o

    s_fin, o = lax.scan(step, s0.astype(q.dtype), (chunks(q), chunks(k), chunks(v)))
    o = jnp.moveaxis(o, 0, 2).reshape(B, H, T, dv)
    return o, s_fin


def retention_mixer(h, w_in, decay_param, gn, w_out, s0_f, s0_b):
    B, T, _ = h.shape
    hk = RET_HEADS * RET_DK
    hv = RET_HEADS * RET_DV
    q, k, v, g_f, g_b = jnp.split(h @ w_in, [hk, 2 * hk, 2 * hk + hv, 2 * hk + 2 * hv], axis=-1)

    def heads(a, d):
        return a.reshape(B, T, RET_HEADS, d).transpose(0, 2, 1, 3)

    q = heads(q, RET_DK)
    k = heads(k, RET_DK) * (RET_DK ** -0.5)
    v = heads(v, RET_DV)
    log_gamma = -jnp.exp(decay_param.astype(F32))
    o_f, s_f = retention_scan(q, k, v, log_gamma[0], s0_f)
    flip = lambda a: jnp.flip(a, axis=2)
    o_b, s_b = retention_scan(flip(q), flip(k), flip(v), log_gamma[1], s0_b)
    o_b = flip(o_b)
    y = jax.nn.silu(g_f) * head_layernorm(o_f, gn[0]) + jax.nn.silu(g_b) * head_layernorm(o_b, gn[1])
    return y @ w_out, s_f, s_b


def axial_rope_tables(rows):
    row = jnp.repeat(jnp.arange(rows), GRID_W).astype(F32)
    col = jnp.tile(jnp.arange(GRID_W), rows).astype(F32)
    half = MLA_ROPE // 2
    inv = ROPE_BASE ** (-jnp.arange(0, half, 2, dtype=F32) / half)
    ang = jnp.stack([row[:, None] * inv, col[:, None] * inv], axis=1)[:, :, None, :]
    return jnp.cos(ang), jnp.sin(ang)


def apply_axial_rope(x, cos, sin):
    shp = x.shape
    xr = x.reshape(shp[:-1] + (2, 2, MLA_ROPE // 4))
    x1, x2 = xr[..., 0:1, :], xr[..., 1:2, :]
    cos = cos.astype(x.dtype)
    sin = sin.astype(x.dtype)
    out = jnp.concatenate([x1 * cos - x2 * sin, x2 * cos + x1 * sin], axis=-2)
    return out.reshape(shp)


def mla_project(h, w_in, q_norm, kv_norm, w_uq):
    B, T, _ = h.shape
    cq, ckv, kpe = jnp.split(h @ w_in, [MLA_Q_RANK, MLA_Q_RANK + MLA_KV_RANK], axis=-1)
    q = (rmsnorm(cq, q_norm) @ w_uq).reshape(B, T, MLA_HEADS, MLA_NOPE + MLA_ROPE)
    return q[..., :MLA_NOPE], q[..., MLA_NOPE:], rmsnorm(ckv, kv_norm), kpe


def mla_attend(q_nope, q_pe, ckv, kpe, w_uk, w_uv, w_o):
    k_nope = jnp.einsum('bkr,rhd->bkhd', ckv, w_uk)
    v = jnp.einsum('bkr,rhd->bkhd', ckv, w_uv)
    B, Tq = q_nope.shape[:2]
    nb = Tq // ATTN_BLOCK
    scale = (MLA_NOPE + MLA_ROPE) ** -0.5

    def blk(a):
        return jnp.moveaxis(a.reshape((B, nb, ATTN_BLOCK) + a.shape[2:]), 1, 0)

    def one(xs):
        qn, qp = xs
        s = jnp.einsum('bqhd,bkhd->bhqk', qn, k_nope) + jnp.einsum('bqhd,bkd->bhqk', qp, kpe)
        p = jax.nn.softmax(s.astype(F32) * scale, axis=-1).astype(v.dtype)
        return jnp.einsum('bhqk,bkhd->bqhd', p, v)

    o = lax.map(one, (blk(q_nope), blk(q_pe)))
    o = jnp.moveaxis(o, 0, 1).reshape(B, Tq, MLA_HEADS * MLA_V)
    return o @ w_o


def grouped_moe(h, w_router, router_bias, w_gate, w_up, w_down):
    B, T, D = h.shape
    x = h.reshape(B * T, D)
    s = jax.nn.sigmoid((x @ w_router).astype(F32))
    sb = s + router_bias.astype(F32)
    sbg = sb.reshape(-1, N_GROUPS, EXPERTS_PER_GROUP)
    gscore = jnp.sum(lax.top_k(sbg, GROUP_SCORE_K)[0], axis=-1)
    g = jnp.argmax(gscore, axis=-1)
    gmask = jnp.arange(N_GROUPS)[None, :] == g[:, None]
    masked = jnp.where(gmask[:, :, None], sbg, -jnp.inf).reshape(-1, N_EXPERTS)
    _, eid = lax.top_k(masked, TOP_K)
    w = jnp.take_along_axis(s, eid, axis=1)
    w = w / jnp.sum(w, axis=-1, keepdims=True)
    gate = jnp.sum(jax.nn.one_hot(eid, N_EXPERTS, dtype=F32) * w[..., None], axis=1)
    a = jnp.einsum('nd,edf->nef', x, w_gate)
    u = jnp.einsum('nd,edf->nef', x, w_up)
    act = jax.nn.silu(a) * u * gate[..., None].astype(x.dtype)
    return jnp.einsum('nef,efd->nd', act, w_down).reshape(B, T, D)


def setup_inputs(seed: int = 0) -> dict:
    key = jax.random.key(seed)
    ks = iter(jax.random.split(key, 40))
    nrm = lambda shape, s: jax.random.normal(next(ks), shape, F32) * s
    gain = lambda shape: 1.0 + nrm(shape, 0.02)
    n_ret = (DEPTH + N_MIXERS - 1) // N_MIXERS
    n_mla = DEPTH // N_MIXERS
    hk = RET_HEADS * RET_DK
    hv = RET_HEADS * RET_DV
    base = jnp.asarray(np.log(-np.log(1.0 - 2.0 ** (-5.0 - np.arange(RET_HEADS)))), F32)
    return {
        "x_prompt": nrm((BATCH, SEQ, D_MODEL), 1.0),
        "x_sample": nrm((DEC_BATCH, DEC_SEQ, D_MODEL), 1.0),
        "state_ret_fwd": nrm((DEC_BATCH, n_ret, RET_HEADS, RET_DK, RET_DV), 0.5),
        "state_ret_bwd": nrm((DEC_BATCH, n_ret, RET_HEADS, RET_DK, RET_DV), 0.5),
        "cache_mla_ckv": nrm((DEC_BATCH, n_mla, PAST_LEN, MLA_KV_RANK), 1.0),
        "cache_mla_kpe": nrm((DEC_BATCH, n_mla, PAST_LEN, MLA_ROPE), 1.0),
        "c": nrm((DEC_BATCH, D_MODEL), 1.0),
        "c_ctx": nrm((D_MODEL,), 1.0),
        "w_ada": nrm((DEPTH, D_MODEL, 6 * D_MODEL), 0.5 * D_MODEL ** -0.5),
        "b_ada": nrm((DEPTH, 6 * D_MODEL), 0.02),
        "norm1": gain((DEPTH, D_MODEL)),
        "norm2": gain((DEPTH, D_MODEL)),
        "norm_final": gain((D_MODEL,)),
        "ret_w_in": nrm((n_ret, D_MODEL, 2 * hk + 3 * hv), D_MODEL ** -0.5),
        "ret_decay": base + nrm((n_ret, 2, RET_HEADS), 0.1),
        "ret_gn": gain((n_ret, 2, hv)),
        "ret_w_out": nrm((n_ret, hv, D_MODEL), hv ** -0.5),
        "mla_w_in": nrm((n_mla, D_MODEL, MLA_Q_RANK + MLA_KV_RANK + MLA_ROPE), D_MODEL ** -0.5),
        "mla_q_norm": gain((n_mla, MLA_Q_RANK)),
        "mla_kv_norm": gain((n_mla, MLA_KV_RANK)),
        "mla_w_uq": nrm((n_mla, MLA_Q_RANK, MLA_HEADS * (MLA_NOPE + MLA_ROPE)), MLA_Q_RANK ** -0.5),
        "mla_w_uk": nrm((n_mla, MLA_KV_RANK, MLA_HEADS, MLA_NOPE), MLA_KV_RANK ** -0.5),
        "mla_w_uv": nrm((n_mla, MLA_KV_RANK, MLA_HEADS, MLA_V), MLA_KV_RANK ** -0.5),
        "mla_w_o": nrm((n_mla, MLA_HEADS * MLA_V, D_MODEL), (MLA_HEADS * MLA_V) ** -0.5),
        "w_router": nrm((D_MODEL, N_EXPERTS), D_MODEL ** -0.5),
        "router_bias": nrm((N_EXPERTS,), 0.01),
        "moe_w_gate": nrm((DEPTH, N_EXPERTS, D_MODEL, EXPERT_FF), D_MODEL ** -0.5),
        "moe_w_up": nrm((DEPTH, N_EXPERTS, D_MODEL, EXPERT_FF), D_MODEL ** -0.5),
        "moe_w_down": nrm((DEPTH, N_EXPERTS, EXPERT_FF, D_MODEL), EXPERT_FF ** -0.5),
    }


def reference(x_prompt, x_sample, state_ret_fwd, state_ret_bwd, cache_mla_ckv, cache_mla_kpe, c, c_ctx,
              w_ada, b_ada, norm1, norm2, norm_final,
              ret_w_in, ret_decay, ret_gn, ret_w_out,
              mla_w_in, mla_q_norm, mla_kv_norm, mla_w_uq, mla_w_uk, mla_w_uv, mla_w_o,
              w_router, router_bias, moe_w_gate, moe_w_up, moe_w_down):
    rows = x_sample.shape[1] // GRID_W
    cos, sin = axial_rope_tables(rows)
    xp, xs = x_prompt, x_sample
    ret_f, ret_b, mla_ckv, mla_kpe = [], [], [], []
    for i in range(DEPTH):
        j = i // N_MIXERS
        sh1p, sc1p, g1p, sh2p, sc2p, g2p = ada_params(c_ctx[None, :], w_ada[i], b_ada[i])
        sh1s, sc1s, g1s, sh2s, sc2s, g2s = ada_params(c, w_ada[i], b_ada[i])
        hp = rmsnorm(xp, norm1[i]) * (1.0 + sc1p) + sh1p
        hs = rmsnorm(xs, norm1[i]) * (1.0 + sc1s) + sh1s
        if i % N_MIXERS == 0:
            z = jnp.zeros((xp.shape[0], RET_HEADS, RET_DK, RET_DV), xp.dtype)
            op, sf, sb = retention_mixer(hp, ret_w_in[j], ret_decay[j], ret_gn[j], ret_w_out[j], z, z)
            ret_f.append(sf)
            ret_b.append(sb)
            os_, _, _ = retention_mixer(hs, ret_w_in[j], ret_decay[j], ret_gn[j], ret_w_out[j],
                                        state_ret_fwd[:, j], state_ret_bwd[:, j])
        else:
            qn, qpe, ckv, kpe = mla_project(hp, mla_w_in[j], mla_q_norm[j], mla_kv_norm[j], mla_w_uq[j])
            op = mla_attend(qn, qpe, ckv, kpe, mla_w_uk[j], mla_w_uv[j], mla_w_o[j])
            mla_ckv.append(ckv)
            mla_kpe.append(kpe)
            qn_s, qpe_s, ckv_s, kpe_s = mla_project(hs, mla_w_in[j], mla_q_norm[j], mla_kv_norm[j], mla_w_uq[j])
            qpe_s = apply_axial_rope(qpe_s, cos[:, None], sin[:, None])
            kpe_s = apply_axial_rope(kpe_s, cos, sin)
            ckv_all = jnp.concatenate([ckv_s, cache_mla_ckv[:, j].astype(ckv_s.dtype)], axis=1)
            kpe_all = jnp.concatenate([kpe_s, cache_mla_kpe[:, j].astype(kpe_s.dtype)], axis=1)
            os_ = mla_attend(qn_s, qpe_s, ckv_all, kpe_all, mla_w_uk[j], mla_w_uv[j], mla_w_o[j])
        xp = xp + g1p * op
        xs = xs + g1s * os_
        xp = xp + g2p * grouped_moe(rmsnorm(xp, norm2[i]) * (1.0 + sc2p) + sh2p,
                                    w_router, router_bias, moe_w_gate[i], moe_w_up[i], moe_w_down[i])
        xs = xs + g2s * grouped_moe(rmsnorm(xs, norm2[i]) * (1.0 + sc2s) + sh2s,
                                    w_router, router_bias, moe_w_gate[i], moe_w_up[i], moe_w_down[i])
    y_prompt = rmsnorm(xp, norm_final)
    y_sample = rmsnorm(xs, norm_final)
    return (y_prompt, y_sample, jnp.stack(ret_f, axis=1), jnp.stack(ret_b, axis=1),
            jnp.stack(mla_ckv, axis=1), jnp.stack(mla_kpe, axis=1))
```

```python
import functools

import numpy as np
import jax
import jax.numpy as jnp
from jax import lax
from jax.experimental import pallas as pl
from jax.experimental.pallas import tpu as pltpu

F32 = jnp.float32
BF16 = jnp.bfloat16

D_MODEL = 1024
DEPTH = 4
GRID_W = 64
RET_HEADS = 4
RET_DK = 256
RET_DV = 512
RET_CHUNK = 128
MLA_HEADS = 8
MLA_Q_RANK = 512
MLA_KV_RANK = 256
MLA_NOPE = 128
MLA_ROPE = 64
MLA_V = 128
ROPE_BASE = 10000.0
N_EXPERTS = 16
N_GROUPS = 4
EXPERTS_PER_GROUP = N_EXPERTS // N_GROUPS
EXPERT_FF = 256
EPS = 1e-6

LANES = 128
HEAD_W = 2 * LANES
VMEM_LIMIT = 48 * 1024 * 1024


def _params(sem):
    return pltpu.CompilerParams(dimension_semantics=sem, vmem_limit_bytes=VMEM_LIMIT)


def _sigmoid(x):
    return 1.0 / (1.0 + jnp.exp(-x))


def _normmod(x, g, sc, sh):
    y = x * lax.rsqrt(jnp.mean(x * x, axis=-1, keepdims=True) + EPS)
    return (y * g) * (1.0 + sc) + sh


def _rms(x, g):
    return x * lax.rsqrt(jnp.mean(x * x, axis=-1, keepdims=True) + EPS) * g


def _ada_kernel(c_ref, w_ref, b_ref, o_ref):
    c = c_ref[...]
    s = (c * _sigmoid(c)).astype(BF16)
    o_ref[0] = jnp.dot(s, w_ref[0].astype(BF16), preferred_element_type=F32) + b_ref[0]


def _ada_all(conds, w_ada, b_ada):
    tn = 1536
    n6 = 6 * D_MODEL
    return pl.pallas_call(
        _ada_kernel,
        out_shape=jax.ShapeDtypeStruct((DEPTH, 8, n6), F32),
        grid=(DEPTH, n6 // tn),
        in_specs=[
            pl.BlockSpec((8, D_MODEL), lambda l, j: (0, 0)),
            pl.BlockSpec((1, D_MODEL, tn), lambda l, j: (l, 0, j)),
            pl.BlockSpec((1, 1, tn), lambda l, j: (l, 0, j)),
        ],
        out_specs=pl.BlockSpec((1, 8, tn), lambda l, j: (l, 0, j)),
        compiler_params=_params(("arbitrary", "arbitrary")),
        name="ada",
    )(conds, w_ada, b_ada.reshape(DEPTH, 1, n6))


def _mod_spec(which, tm, rows_per_cond, ngrid):
    if ngrid == 1:
        return pl.BlockSpec((None, 1, D_MODEL), lambda i: (i * tm // rows_per_cond, 0, which))
    return pl.BlockSpec((None, 1, D_MODEL), lambda i, j: (i * tm // rows_per_cond, 0, which))


def _proj_kernel(x_ref, g_ref, sc_ref, sh_ref, w_ref, o_ref, h_ref):
    @pl.when(pl.program_id(1) == 0)
    def _():
        h_ref[...] = _normmod(x_ref[...], g_ref[...], sc_ref[...], sh_ref[...]).astype(BF16)

    o_ref[...] = jnp.dot(h_ref[...], w_ref[...], preferred_element_type=F32).astype(o_ref.dtype)


def _proj(x, mod, norm_g, w, rows_per_cond):
    n = x.shape[0]
    nout = w.shape[1]
    tm, tn = 1024, 1024
    return pl.pallas_call(
        _proj_kernel,
        out_shape=jax.ShapeDtypeStruct((n, nout), BF16),
        grid=(n // tm, nout // tn),
        in_specs=[
            pl.BlockSpec((tm, D_MODEL), lambda i, j: (i, 0)),
            pl.BlockSpec((1, D_MODEL), lambda i, j: (0, 0)),
            _mod_spec(1, tm, rows_per_cond, 2),
            _mod_spec(0, tm, rows_per_cond, 2),
            pl.BlockSpec((D_MODEL, tn), lambda i, j: (0, j)),
        ],
        out_specs=pl.BlockSpec((tm, tn), lambda i, j: (i, j)),
        scratch_shapes=[pltpu.VMEM((tm, D_MODEL), BF16)],
        compiler_params=_params(("arbitrary", "arbitrary")),
        name="ret_proj",
    )(x, norm_g, mod, mod, w)


def _scan_kernel(*refs, seq, has_s0, want_state):
    q_ref, k_ref, v_ref, gf_ref, gb_ref, dec_ref, gn_ref = refs[:7]
    pos = 7
    if has_s0:
        s0f_ref, s0b_ref = refs[pos:pos + 2]
        pos += 2
    y_ref = refs[pos]
    pos += 1
    if want_state:
        sf_ref, sb_ref = refs[pos:pos + 2]
        pos += 2
    s_ref, yacc_ref = refs[pos:pos + 2]

    C = RET_CHUNK
    nc = seq // C
    ri = lax.broadcasted_iota(jnp.int32, (C, C), 0).astype(F32)
    ci = lax.broadcasted_iota(jnp.int32, (C, C), 1).astype(F32)
    rcol = ri[:, 0:1]

    def direction(d, reverse):
        lgw = -jnp.exp(dec_ref[d])
        lg = lgw[:, 0:C]
        lgc = lgw[:, 0:1]
        diff = (ci - ri) if reverse else (ri - ci)
        intra = jnp.where(diff >= 0, jnp.exp(lg * jnp.maximum(diff, 0.0)), 0.0)
        if reverse:
            q_dec = jnp.exp(lgc * (C - rcol))
            k_dec = jnp.exp(lgc * rcol)
        else:
            q_dec = jnp.exp(lgc * (rcol + 1.0))
            k_dec = jnp.exp(lgc * (C - 1.0 - rcol))
        c_dec = jnp.exp(lgw[0:1, :] * float(C))
        gn = gn_ref[d:d + 1, :]
        g_ref = gb_ref if reverse else gf_ref

        if has_s0:
            s_ref[...] = (s0b_ref if reverse else s0f_ref)[0, 0]
        else:
            s_ref[...] = jnp.zeros_like(s_ref)

        def body(i, carry):
            c = (nc - 1 - i) if reverse else i
            rows = pl.ds(pl.multiple_of(c * C, C), C)
            q = q_ref[rows, :]
            kf = k_ref[rows, :].astype(F32) * (RET_DK ** -0.5)
            v = v_ref[rows, :]
            att = lax.dot_general(q, kf.astype(BF16), (((1,), (1,)), ((), ())),
                                  preferred_element_type=F32)
            att = (att * intra).astype(BF16)
            qd = (q.astype(F32) * q_dec).astype(BF16)
            s_old = s_ref[...]
            o = (jnp.dot(att, v, preferred_element_type=F32)
                 + jnp.dot(qd, s_old.astype(BF16), preferred_element_type=F32))
            kd = (kf * k_dec).astype(BF16)
            s_ref[...] = s_old * c_dec + lax.dot_general(
                kd, v, (((0,), (0,)), ((), ())), preferred_element_type=F32)
            mu = jnp.mean(o, axis=-1, keepdims=True)
            dlt = o - mu
            var = jnp.mean(dlt * dlt, axis=-1, keepdims=True)
            yn = (dlt * lax.rsqrt(var + EPS)) * gn
            g = g_ref[rows, :].astype(F32)
            contrib = (g * _sigmoid(g)) * yn
            if reverse:
                yacc_ref[rows, :] = yacc_ref[rows, :] + contrib
            else:
                yacc_ref[rows, :] = contrib
            return carry

        lax.fori_loop(0, nc, body, 0)
        if want_state:
            (sb_ref if reverse else sf_ref)[0, 0] = s_ref[...]

    direction(0, False)
    direction(1, True)
    y_ref[...] = yacc_ref[...].astype(y_ref.dtype)


def _retention_scan(proj, dec, gn, batch, seq, s0f=None, s0b=None, want_state=False):
    has_s0 = s0f is not None
    hv = RET_HEADS * RET_DV
    nkb = RET_HEADS
    nvb = 2 * RET_HEADS * RET_DK // RET_DV
    in_specs = [
        pl.BlockSpec((seq, RET_DK), lambda b, h: (b, h)),
        pl.BlockSpec((seq, RET_DK), lambda b, h: (b, nkb + h)),
        pl.BlockSpec((seq, RET_DV), lambda b, h: (b, nvb + h)),
        pl.BlockSpec((seq, RET_DV), lambda b, h: (b, nvb + RET_HEADS + h)),
        pl.BlockSpec((seq, RET_DV), lambda b, h: (b, nvb + 2 * RET_HEADS + h)),
        pl.BlockSpec((2, None, RET_CHUNK, RET_DV), lambda b, h: (0, h, 0, 0)),
        pl.BlockSpec((2, RET_DV), lambda b, h: (0, h)),
    ]
    args = [proj, proj, proj, proj, proj, dec, gn]
    if has_s0:
        st_spec = pl.BlockSpec((1, 1, RET_DK, RET_DV), lambda b, h: (b, h, 0, 0))
        in_specs += [st_spec, st_spec]
        args += [s0f, s0b]
    out_shape = [jax.ShapeDtypeStruct((batch * seq, hv), BF16)]
    out_specs = [pl.BlockSpec((seq, RET_DV), lambda b, h: (b, h))]
    if want_state:
        st = jax.ShapeDtypeStruct((batch, RET_HEADS, RET_DK, RET_DV), F32)
        out_shape += [st, st]
        st_spec = pl.BlockSpec((1, 1, RET_DK, RET_DV), lambda b, h: (b, h, 0, 0))
        out_specs += [st_spec, st_spec]
    return pl.pallas_call(
        functools.partial(_scan_kernel, seq=seq, has_s0=has_s0, want_state=want_state),
        out_shape=out_shape,
        grid=(batch, RET_HEADS),
        in_specs=in_specs,
        out_specs=out_specs,
        scratch_shapes=[pltpu.VMEM((RET_DK, RET_DV), F32), pltpu.VMEM((seq, RET_DV), F32)],
        compiler_params=_params(("arbitrary", "arbitrary")),
        name="ret_scan",
    )(*args)


def _route(logits, bias):
    s = _sigmoid(logits)
    sb = s + bias
    lane = lax.broadcasted_iota(jnp.int32, sb.shape, 1)
    lane_f = lane.astype(F32)
    neg = -jnp.inf
    big = float(LANES)

    def top2(vals):
        m1 = jnp.max(vals, axis=-1, keepdims=True)
        i1 = jnp.min(jnp.where(vals == m1, lane_f, big), axis=-1, keepdims=True)
        vals2 = jnp.where(lane_f == i1, neg, vals)
        m2 = jnp.max(vals2, axis=-1, keepdims=True)
        i2 = jnp.min(jnp.where(vals2 == m2, lane_f, big), axis=-1, keepdims=True)
        return m1 + m2, i1, i2

    best = None
    for g in range(N_GROUPS):
        in_group = (lane // EXPERTS_PER_GROUP) == g
        gs, i1, i2 = top2(jnp.where(in_group, sb, neg))
        if best is None:
            best = (gs, i1, i2)
        else:
            take = gs > best[0]
            best = (jnp.where(take, gs, best[0]), jnp.where(take, i1, best[1]),
                    jnp.where(take, i2, best[2]))
    _, e1, e2 = best
    oh1 = lane_f == e1
    oh2 = lane_f == e2
    w1 = jnp.sum(jnp.where(oh1, s, 0.0), axis=-1, keepdims=True)
    w2 = jnp.sum(jnp.where(oh2, s, 0.0), axis=-1, keepdims=True)
    den = w1 + w2
    return jnp.where(oh1, w1 / den, 0.0) + jnp.where(oh2, w2 / den, 0.0)


def _split_bf16(a):
    hi = a.astype(BF16)
    lo = (a - hi.astype(F32)).astype(BF16)
    return hi, lo


def _mix_out_kernel(y_ref, w_ref, x_ref, g1_ref, n2_ref, sc_ref, sh_ref, wr_ref, rb_ref,
                    xo_ref, h_ref, gate_ref):
    x = x_ref[...] + g1_ref[...] * jnp.dot(y_ref[...], w_ref[...], preferred_element_type=F32)
    xo_ref[...] = x
    h = _normmod(x, n2_ref[...], sc_ref[...], sh_ref[...])
    h_ref[...] = h.astype(BF16)
    h_hi, h_lo = _split_bf16(h)
    w_hi, w_lo = _split_bf16(wr_ref[...])
    logits = (jnp.dot(h_hi, w_hi, preferred_element_type=F32)
              + jnp.dot(h_hi, w_lo, preferred_element_type=F32)
              + jnp.dot(h_lo, w_hi, preferred_element_type=F32))
    gate_ref[...] = _route(logits, rb_ref[...])


def _mix_out(y, w, x, mod, norm2_g, wr_pad, rb_pad, rows_per_cond):
    n, kdim = y.shape
    tm = 512
    full = lambda i: (0, 0)
    return pl.pallas_call(
        _mix_out_kernel,
        out_shape=[jax.ShapeDtypeStruct((n, D_MODEL), F32),
                   jax.ShapeDtypeStruct((n, D_MODEL), BF16),
                   jax.ShapeDtypeStruct((n, LANES), F32)],
        grid=(n // tm,),
        in_specs=[
            pl.BlockSpec((tm, kdim), lambda i: (i, 0)),
            pl.BlockSpec((kdim, D_MODEL), full),
            pl.BlockSpec((tm, D_MODEL), lambda i: (i, 0)),
            _mod_spec(2, tm, rows_per_cond, 1),
            pl.BlockSpec((1, D_MODEL), full),
            _mod_spec(4, tm, rows_per_cond, 1),
            _mod_spec(3, tm, rows_per_cond, 1),
            pl.BlockSpec((D_MODEL, LANES), full),
            pl.BlockSpec((1, LANES), full),
        ],
        out_specs=[pl.BlockSpec((tm, D_MODEL), lambda i: (i, 0)),
                   pl.BlockSpec((tm, D_MODEL), lambda i: (i, 0)),
                   pl.BlockSpec((tm, LANES), lambda i: (i, 0))],
        compiler_params=_params(("arbitrary",)),
        name="mix_out",
    )(y, w, x, mod, norm2_g, mod, mod, wr_pad, rb_pad)


def _moe_kernel(x_ref, h_ref, gate_ref, g2_ref, wg_ref, wu_ref, wd_ref, nf_ref, o_ref, acc_ref,
                *, final_norm):
    e = pl.program_id(1)

    @pl.when(e == 0)
    def _():
        acc_ref[...] = jnp.zeros_like(acc_ref)

    h = h_ref[...]
    a = jnp.dot(h, wg_ref[0], preferred_element_type=F32)
    u = jnp.dot(h, wu_ref[0], preferred_element_type=F32)
    gate = gate_ref[...]
    lane = lax.broadcasted_iota(jnp.int32, gate.shape, 1)
    ge = jnp.sum(jnp.where(lane == e, gate, 0.0), axis=-1, keepdims=True)
    act = ((a * _sigmoid(a)) * u * ge).astype(BF16)
    acc_ref[...] += jnp.dot(act, wd_ref[0], preferred_element_type=F32)

    @pl.when(e == N_EXPERTS - 1)
    def _():
        x = x_ref[...] + g2_ref[...] * acc_ref[...]
        if final_norm:
            x = _rms(x, nf_ref[...])
        o_ref[...] = x


def _moe(x, h, gate, mod, wg, wu, wd, norm_f, rows_per_cond, final_norm):
    n = x.shape[0]
    tm = 1024
    return pl.pallas_call(
        functools.partial(_moe_kernel, final_norm=final_norm),
        out_shape=jax.ShapeDtypeStruct((n, D_MODEL), F32),
        grid=(n // tm, N_EXPERTS),
        in_specs=[
            pl.BlockSpec((tm, D_MODEL), lambda i, e: (i, 0)),
            pl.BlockSpec((tm, D_MODEL), lambda i, e: (i, 0)),
            pl.BlockSpec((tm, LANES), lambda i, e: (i, 0)),
            _mod_spec(5, tm, rows_per_cond, 2),
            pl.BlockSpec((1, D_MODEL, EXPERT_FF), lambda i, e: (e, 0, 0)),
            pl.BlockSpec((1, D_MODEL, EXPERT_FF), lambda i, e: (e, 0, 0)),
            pl.BlockSpec((1, EXPERT_FF, D_MODEL), lambda i, e: (e, 0, 0)),
            pl.BlockSpec((1, D_MODEL), lambda i, e: (0, 0)),
        ],
        out_specs=pl.BlockSpec((tm, D_MODEL), lambda i, e: (i, 0)),
        scratch_shapes=[pltpu.VMEM((tm, D_MODEL), F32)],
        compiler_params=_params(("arbitrary", "arbitrary")),
        name="moe",
    )(x, h, gate, mod, wg, wu, wd, norm_f)


def _swap_pairs(x):
    lane = lax.broadcasted_iota(jnp.int32, x.shape, 1)
    first = (lane % MLA_ROPE) < (MLA_ROPE // 2)
    return jnp.where(first, pltpu.roll(x, LANES - MLA_ROPE // 2, axis=1),
                     pltpu.roll(x, MLA_ROPE // 2, axis=1))


def _mla_proj_kernel(*refs, rope):
    x_ref, g_ref, sc_ref, sh_ref, win_ref, qn_ref, kvn_ref, wuq_ref = refs[:8]
    pos = 8
    if rope:
        cos_ref, sin_ref = refs[pos:pos + 2]
        pos += 2
    q_ref, ckv_ref, kpe_ref = refs[pos:pos + 3]

    h = _normmod(x_ref[...], g_ref[...], sc_ref[...], sh_ref[...]).astype(BF16)
    p = jnp.dot(h, win_ref[...], preferred_element_type=F32)
    cq = _rms(p[:, :MLA_Q_RANK], qn_ref[...]).astype(BF16)
    ckv_ref[...] = _rms(p[:, MLA_Q_RANK:MLA_Q_RANK + MLA_KV_RANK], kvn_ref[...])
    kpe = p[:, MLA_Q_RANK + MLA_KV_RANK:]
    q = jnp.dot(cq, wuq_ref[...], preferred_element_type=F32)
    if rope:
        cos = cos_ref[...]
        sin = sin_ref[...]
        kpe = kpe * cos + _swap_pairs(kpe) * sin
    kpe_ref[...] = kpe
    for hd in range(MLA_HEADS):
        lo = hd * HEAD_W
        q_ref[:, lo:lo + LANES] = q[:, lo:lo + LANES].astype(BF16)
        qp = q[:, lo + LANES:lo + HEAD_W]
        if rope:
            qp = qp * cos + _swap_pairs(qp) * sin
        q_ref[:, lo + LANES:lo + HEAD_W] = qp.astype(BF16)


def _mla_proj(x, mod, norm_g, w_in, q_norm, kv_norm, w_uq, rows_per_cond, cos=None, sin=None):
    n = x.shape[0]
    tm = 512
    rope = cos is not None
    full = lambda i: (0, 0)
    nin = w_in.shape[1]
    in_specs = [
        pl.BlockSpec((tm, D_MODEL), lambda i: (i, 0)),
        pl.BlockSpec((1, D_MODEL), full),
        _mod_spec(1, tm, rows_per_cond, 1),
        _mod_spec(0, tm, rows_per_cond, 1),
        pl.BlockSpec((D_MODEL, nin), full),
        pl.BlockSpec((1, MLA_Q_RANK), full),
        pl.BlockSpec((1, MLA_KV_RANK), full),
        pl.BlockSpec((MLA_Q_RANK, MLA_HEADS * HEAD_W), full),
    ]
    args = [x, norm_g, mod, mod, w_in, q_norm, kv_norm, w_uq]
    if rope:
        nt = cos.shape[0] // tm
        tab = pl.BlockSpec((tm, LANES), lambda i: (i % nt, 0))
        in_specs += [tab, tab]
        args += [cos, sin]
    return pl.pallas_call(
        functools.partial(_mla_proj_kernel, rope=rope),
        out_shape=[jax.ShapeDtypeStruct((n, MLA_HEADS * HEAD_W), BF16),
                   jax.ShapeDtypeStruct((n, MLA_KV_RANK), F32),
                   jax.ShapeDtypeStruct((n, LANES), F32)],
        grid=(n // tm,),
        in_specs=in_specs,
        out_specs=[pl.BlockSpec((tm, MLA_HEADS * HEAD_W), lambda i: (i, 0)),
                   pl.BlockSpec((tm, MLA_KV_RANK), lambda i: (i, 0)),
                   pl.BlockSpec((tm, LANES), lambda i: (i, 0))],
        compiler_params=_params(("arbitrary",)),
        name="mla_proj",
    )(*args)


def _kv_up_kernel(ckv_ref, kpe_ref, wuk_ref, wuv_ref, k_ref, v_ref):
    c = ckv_ref[...].astype(BF16)
    kn = jnp.dot(c, wuk_ref[...], preferred_element_type=F32).astype(BF16)
    v_ref[...] = jnp.dot(c, wuv_ref[...], preferred_element_type=F32).astype(BF16)
    kpe = kpe_ref[...].astype(BF16)
    for hd in range(MLA_HEADS):
        k_ref[:, hd * HEAD_W:hd * HEAD_W + LANES] = kn[:, hd * MLA_NOPE:(hd + 1) * MLA_NOPE]
        k_ref[:, hd * HEAD_W + LANES:(hd + 1) * HEAD_W] = kpe


def _kv_up(ckv, kpe, w_uk, w_uv):
    n = ckv.shape[0]
    tk = 512
    full = lambda i: (0, 0)
    return pl.pallas_call(
        _kv_up_kernel,
        out_shape=[jax.ShapeDtypeStruct((n, MLA_HEADS * HEAD_W), BF16),
                   jax.ShapeDtypeStruct((n, MLA_HEADS * MLA_V), BF16)],
        grid=(n // tk,),
        in_specs=[
            pl.BlockSpec((tk, MLA_KV_RANK), lambda i: (i, 0)),
            pl.BlockSpec((tk, LANES), lambda i: (i, 0)),
            pl.BlockSpec((MLA_KV_RANK, MLA_HEADS * MLA_NOPE), full),
            pl.BlockSpec((MLA_KV_RANK, MLA_HEADS * MLA_V), full),
        ],
        out_specs=[pl.BlockSpec((tk, MLA_HEADS * HEAD_W), lambda i: (i, 0)),
                   pl.BlockSpec((tk, MLA_HEADS * MLA_V), lambda i: (i, 0))],
        compiler_params=_params(("arbitrary",)),
        name="kv_up",
    )(ckv, kpe, w_uk, w_uv)


def _attn_kernel(q_ref, k_ref, v_ref, o_ref, *, heads):
    scale = (MLA_NOPE + MLA_ROPE) ** -0.5
    for hd in range(heads):
        q = q_ref[:, hd * HEAD_W:(hd + 1) * HEAD_W]
        k = k_ref[:, hd * HEAD_W:(hd + 1) * HEAD_W]
        s = lax.dot_general(q, k, (((1,), (1,)), ((), ())), preferred_element_type=F32)
        m = jnp.max(s, axis=-1, keepdims=True)
        e = jnp.exp((s - m) * scale)
        den = jnp.sum(e, axis=-1, keepdims=True)
        o = jnp.dot(e.astype(BF16), v_ref[:, hd * MLA_V:(hd + 1) * MLA_V],
                    preferred_element_type=F32)
        o_ref[:, hd * MLA_V:(hd + 1) * MLA_V] = (o / den).astype(o_ref.dtype)


def _attention(q, k, v, batch, tq_total, tk_total, heads_per_step, tq):
    nq = tq_total // tq
    nh = MLA_HEADS // heads_per_step
    hp = heads_per_step
    return pl.pallas_call(
        functools.partial(_attn_kernel, heads=hp),
        out_shape=jax.ShapeDtypeStruct((batch * tq_total, MLA_HEADS * MLA_V), BF16),
        grid=(batch, nh, nq),
        in_specs=[
            pl.BlockSpec((tq, hp * HEAD_W), lambda b, h, i: (b * nq + i, h)),
            pl.BlockSpec((tk_total, hp * HEAD_W), lambda b, h, i: (b, h)),
            pl.BlockSpec((tk_total, hp * MLA_V), lambda b, h, i: (b, h)),
        ],
        out_specs=pl.BlockSpec((tq, hp * MLA_V), lambda b, h, i: (b * nq + i, h)),
        compiler_params=_params(("arbitrary", "arbitrary", "arbitrary")),
        name="attn",
    )(q, k, v)


_ROPE_PERM = np.concatenate([np.arange(0, 16), np.arange(32, 48), np.arange(16, 32), np.arange(48, 64)])


def _rope_tables(seq):
    pos = jnp.arange(seq)
    row = (pos // GRID_W).astype(F32)
    col = (pos % GRID_W).astype(F32)
    half = MLA_ROPE // 2
    inv = ROPE_BASE ** (-jnp.arange(0, half, 2, dtype=F32) / half)
    ang = jnp.concatenate([row[:, None] * inv, col[:, None] * inv], axis=1)
    cos = jnp.cos(ang)
    sin = jnp.sin(ang)
    pad = jnp.zeros((seq, LANES - MLA_ROPE), F32)
    return (jnp.concatenate([cos, cos, pad], axis=1),
            jnp.concatenate([-sin, sin, pad], axis=1))


def _prep_mla_weights(w_in, w_uq, w_uk, w_uv):
    nq = MLA_Q_RANK + MLA_KV_RANK
    w_in_p = jnp.concatenate(
        [w_in[:, :nq], w_in[:, nq:][:, _ROPE_PERM], jnp.zeros((D_MODEL, LANES - MLA_ROPE), F32)], axis=1)
    wq = w_uq.reshape(MLA_Q_RANK, MLA_HEADS, MLA_NOPE + MLA_ROPE)
    wq_p = jnp.concatenate(
        [wq[:, :, :MLA_NOPE], wq[:, :, MLA_NOPE:][:, :, _ROPE_PERM],
         jnp.zeros((MLA_Q_RANK, MLA_HEADS, HEAD_W - MLA_NOPE - MLA_ROPE), F32)], axis=2)
    return (w_in_p.astype(BF16), wq_p.reshape(MLA_Q_RANK, MLA_HEADS * HEAD_W).astype(BF16),
            w_uk.reshape(MLA_KV_RANK, MLA_HEADS * MLA_NOPE).astype(BF16),
            w_uv.reshape(MLA_KV_RANK, MLA_HEADS * MLA_V).astype(BF16))


def kernel(x_prompt, x_sample, state_ret_fwd, state_ret_bwd, cache_mla_ckv, cache_mla_kpe, c, c_ctx,
           w_ada, b_ada, norm1, norm2, norm_final, ret_w_in, ret_decay, ret_gn, ret_w_out,
           mla_w_in, mla_q_norm, mla_kv_norm, mla_w_uq, mla_w_uk, mla_w_uv, mla_w_o,
           w_router, router_bias, moe_w_gate, moe_w_up, moe_w_down):
    bp, tp, _ = x_prompt.shape
    bs, ts, _ = x_sample.shape
    past = cache_mla_ckv.shape[2]
    xp = x_prompt.reshape(bp * tp, D_MODEL)
    xs = x_sample.reshape(bs * ts, D_MODEL)

    conds = jnp.concatenate([c_ctx[None, :], c, jnp.zeros((8 - 1 - bs, D_MODEL), F32)], axis=0)
    mods = _ada_all(conds, w_ada, b_ada)

    wr_pad = jnp.pad(w_router, ((0, 0), (0, LANES - N_EXPERTS)))
    rb_pad = jnp.pad(router_bias, (0, LANES - N_EXPERTS)).reshape(1, LANES)
    nf = norm_final.reshape(1, D_MODEL)
    cos_t, sin_t = _rope_tables(ts)

    groups = {
        "p": dict(x=xp, rows_per_cond=bp * tp, lo=0, hi=1),
        "s": dict(x=xs, rows_per_cond=ts, lo=1, hi=1 + bs),
    }
    ret_f, ret_b, out_ckv, out_kpe = [], [], [], []

    for i in range(DEPTH):
        j = i // 2
        n1 = norm1[i].reshape(1, D_MODEL)
        n2 = norm2[i].reshape(1, D_MODEL)
        wg = moe_w_gate[i].astype(BF16)
        wu = moe_w_up[i].astype(BF16)
        wd = moe_w_down[i].astype(BF16)
        mixed = {}
        if i % 2 == 0:
            w_in = ret_w_in[j].astype(BF16)
            w_out = ret_w_out[j].astype(BF16)
            dec = jnp.broadcast_to(ret_decay[j][:, :, None, None], (2, RET_HEADS, RET_CHUNK, RET_DV))
            gn = ret_gn[j]
            for name, gr in groups.items():
                mod = mods[i, gr["lo"]:gr["hi"]].reshape(-1, 1, 6 * D_MODEL)
                proj = _proj(gr["x"], mod, n1, w_in, gr["rows_per_cond"])
                if name == "p":
                    y, sf, sb = _retention_scan(proj, dec, gn, bp, tp, want_state=True)
                    ret_f.append(sf)
                    ret_b.append(sb)
                else:
                    (y,) = _retention_scan(proj, dec, gn, bs, ts, state_ret_fwd[:, j], state_ret_bwd[:, j])
                mixed[name] = (y, w_out, mod)
        else:
            w_in_p, w_uq_p, w_uk, w_uv = _prep_mla_weights(mla_w_in[j], mla_w_uq[j], mla_w_uk[j], mla_w_uv[j])
            w_o = mla_w_o[j].astype(BF16)
            qn = mla_q_norm[j].reshape(1, MLA_Q_RANK)
            kvn = mla_kv_norm[j].reshape(1, MLA_KV_RANK)
            for name, gr in groups.items():
                mod = mods[i, gr["lo"]:gr["hi"]].reshape(-1, 1, 6 * D_MODEL)
                if name == "p":
                    q, ckv, kpe = _mla_proj(gr["x"], mod, n1, w_in_p, qn, kvn, w_uq_p, gr["rows_per_cond"])
                    out_ckv.append(ckv.reshape(bp, tp, MLA_KV_RANK))
                    out_kpe.append(kpe[:, :MLA_ROPE][:, _ROPE_PERM].reshape(bp, tp, MLA_ROPE))
                    k, v = _kv_up(ckv, kpe, w_uk, w_uv)
                    o = _attention(q, k, v, bp, tp, tp, MLA_HEADS, tp)
                else:
                    q, ckv, kpe = _mla_proj(gr["x"], mod, n1, w_in_p, qn, kvn, w_uq_p, gr["rows_per_cond"],
                                            cos_t, sin_t)
                    cache_kpe = jnp.pad(cache_mla_kpe[:, j][:, :, _ROPE_PERM],
                                        ((0, 0), (0, 0), (0, LANES - MLA_ROPE)))
                    ckv_all = jnp.concatenate([ckv.reshape(bs, ts, MLA_KV_RANK), cache_mla_ckv[:, j]], axis=1)
                    kpe_all = jnp.concatenate([kpe.reshape(bs, ts, LANES), cache_kpe], axis=1)
                    tk = ts + past
                    k, v = _kv_up(ckv_all.reshape(bs * tk, MLA_KV_RANK), kpe_all.reshape(bs * tk, LANES),
                                  w_uk, w_uv)
                    o = _attention(q, k, v, bs, ts, tk, 1, 512)
                mixed[name] = (o, w_o, mod)

        for name, gr in groups.items():
            y, w_o, mod = mixed[name]
            x_new, h2, gate = _mix_out(y, w_o, gr["x"], mod, n2, wr_pad, rb_pad, gr["rows_per_cond"])
            gr["x"] = _moe(x_new, h2, gate, mod, wg, wu, wd, nf, gr["rows_per_cond"], i == DEPTH - 1)

    return (groups["p"]["x"].reshape(bp, tp, D_MODEL), groups["s"]["x"].reshape(bs, ts, D_MODEL),
            jnp.stack(ret_f, axis=1), jnp.stack(ret_b, axis=1),
            jnp.stack(out_ckv, axis=1), jnp.stack(out_kpe, axis=1))
```

```python
import functools

import numpy as np
import jax
import jax.numpy as jnp
from jax import lax
from jax.experimental import pallas as pl
from jax.experimental.pallas import tpu as pltpu

F32 = jnp.float32
BF16 = jnp.bfloat16

D_MODEL = 1024
DEPTH = 4
GRID_W = 64
RET_HEADS = 4
RET_DK = 256
RET_DV = 512
SCAN_CHUNK = 256
MLA_HEADS = 8
MLA_Q_RANK = 512
MLA_KV_RANK = 256
MLA_NOPE = 128
MLA_ROPE = 64
MLA_V = 128
ROPE_BASE = 10000.0
N_EXPERTS = 16
N_GROUPS = 4
EXPERTS_PER_GROUP = N_EXPERTS // N_GROUPS
EXPERT_FF = 256
EPS = 1e-6

LANES = 128
HEAD_W = 2 * LANES
VMEM_LIMIT = 48 * 1024 * 1024


def _params(sem):
    return pltpu.CompilerParams(dimension_semantics=sem, vmem_limit_bytes=VMEM_LIMIT)


def _sigmoid(x):
    return 1.0 / (1.0 + jnp.exp(-x))


def _normmod(x, g, sc, sh):
    y = x * lax.rsqrt(jnp.mean(x * x, axis=-1, keepdims=True) + EPS)
    return (y * g) * (1.0 + sc) + sh


def _rms(x, g):
    return x * lax.rsqrt(jnp.mean(x * x, axis=-1, keepdims=True) + EPS) * g


def _ada_kernel(c_ref, w_ref, b_ref, o_ref):
    c = c_ref[...]
    s = (c * _sigmoid(c)).astype(BF16)
    o_ref[0] = jnp.dot(s, w_ref[0].astype(BF16), preferred_element_type=F32) + b_ref[0]


def _ada_all(conds, w_ada, b_ada):
    tn = 1536
    n6 = 6 * D_MODEL
    return pl.pallas_call(
        _ada_kernel,
        out_shape=jax.ShapeDtypeStruct((DEPTH, 8, n6), F32),
        grid=(DEPTH, n6 // tn),
        in_specs=[
            pl.BlockSpec((8, D_MODEL), lambda l, j: (0, 0)),
            pl.BlockSpec((1, D_MODEL, tn), lambda l, j: (l, 0, j)),
            pl.BlockSpec((1, 1, tn), lambda l, j: (l, 0, j)),
        ],
        out_specs=pl.BlockSpec((1, 8, tn), lambda l, j: (l, 0, j)),
        compiler_params=_params(("arbitrary", "arbitrary")),
        name="ada",
    )(conds, w_ada, b_ada.reshape(DEPTH, 1, n6))


def _mod_spec(which, tm, rows_per_cond, ngrid):
    if ngrid == 1:
        return pl.BlockSpec((None, 1, D_MODEL), lambda i: (i * tm // rows_per_cond, 0, which))
    return pl.BlockSpec((None, 1, D_MODEL), lambda i, j: (i * tm // rows_per_cond, 0, which))


def _silu(g):
    hg = 0.5 * g
    return hg + hg * jnp.tanh(hg)


def _proj_kernel(x_ref, g_ref, sc_ref, sh_ref, w_ref, o_ref, h_ref, *, silu_from):
    j = pl.program_id(1)

    @pl.when(j == 0)
    def _():
        h_ref[...] = _normmod(x_ref[...], g_ref[...], sc_ref[...], sh_ref[...]).astype(BF16)

    acc = jnp.dot(h_ref[...], w_ref[...], preferred_element_type=F32)

    @pl.when(j < silu_from)
    def _():
        o_ref[...] = acc.astype(o_ref.dtype)

    @pl.when(j >= silu_from)
    def _():
        o_ref[...] = _silu(acc).astype(o_ref.dtype)


def _proj(x, mod, norm_g, w, rows_per_cond, silu_from_col):
    n = x.shape[0]
    nout = w.shape[1]
    tm, tn = 1024, 1024
    return pl.pallas_call(
        functools.partial(_proj_kernel, silu_from=silu_from_col // tn),
        out_shape=jax.ShapeDtypeStruct((n, nout), BF16),
        grid=(n // tm, nout // tn),
        in_specs=[
            pl.BlockSpec((tm, D_MODEL), lambda i, j: (i, 0)),
            pl.BlockSpec((1, D_MODEL), lambda i, j: (0, 0)),
            _mod_spec(1, tm, rows_per_cond, 2),
            _mod_spec(0, tm, rows_per_cond, 2),
            pl.BlockSpec((D_MODEL, tn), lambda i, j: (0, j)),
        ],
        out_specs=pl.BlockSpec((tm, tn), lambda i, j: (i, j)),
        scratch_shapes=[pltpu.VMEM((tm, D_MODEL), BF16)],
        compiler_params=_params(("arbitrary", "arbitrary")),
        name="ret_proj",
    )(x, norm_g, mod, mod, w)


def _scan_kernel(*refs, seq, has_s0, aliased):
    q_ref, k_ref, v_ref, gf_ref, gb_ref, dec_ref, gn_ref = refs[:7]
    pos = 7
    if has_s0:
        s0_refs = refs[pos:pos + 2]
        pos += 2
    if aliased:
        pos += 2
    y_ref = refs[pos]
    s_refs = refs[pos + 1:pos + 3]
    intra_ref, qdec_ref, kdec_ref, cdec_ref, yacc_ref = refs[pos + 3:pos + 8]

    C = SCAN_CHUNK
    nc = seq // C

    @pl.when(pl.program_id(1) == 0)
    def _():
        ri = lax.broadcasted_iota(jnp.int32, (C, C), 0).astype(F32)
        ci = lax.broadcasted_iota(jnp.int32, (C, C), 1).astype(F32)
        k_scale = RET_DK ** -0.5
        for d in range(2):
            lg = -jnp.exp(dec_ref[d])
            if d == 1:
                diff, q_pow, k_pow = ci - ri, C - ri, ri
            else:
                diff, q_pow, k_pow = ri - ci, ri + 1.0, C - 1.0 - ri
            intra_ref[d] = jnp.where(diff >= 0, jnp.exp(lg * jnp.maximum(diff, 0.0)), 0.0) * k_scale
            qdec_ref[d] = jnp.exp(lg * q_pow)
            kdec_ref[d] = jnp.exp(lg * k_pow) * k_scale
            cdec_ref[d] = jnp.exp(lg[0:8, :] * float(C))

    state_is_zero = not has_s0 and nc == 1
    for d in range(2):
        if has_s0:
            s_refs[d][...] = s0_refs[d][...]
        elif not state_is_zero:
            s_refs[d][...] = jnp.zeros_like(s_refs[d])

    def lanes(a, width):
        return jnp.concatenate([a] * (width // C), axis=1)

    def contribution(rows, d, att):
        g_ref = (gf_ref, gb_ref)[d]
        s_ref = s_refs[d]
        q = q_ref[rows, :]
        k = k_ref[rows, :]
        v = v_ref[rows, :]
        att = (att * intra_ref[d]).astype(BF16)
        o = jnp.dot(att, v, preferred_element_type=F32)
        kd = (k.astype(F32) * lanes(kdec_ref[d], RET_DK)).astype(BF16)
        s_new = lax.dot_general(kd, v, (((0,), (0,)), ((), ())), preferred_element_type=F32)
        if state_is_zero:
            s_ref[...] = s_new
        else:
            qd = (q.astype(F32) * lanes(qdec_ref[d], RET_DK)).astype(BF16)
            s_old = s_ref[...]
            o = o + jnp.dot(qd, s_old.astype(BF16), preferred_element_type=F32)
            s_ref[...] = s_old * lanes(cdec_ref[d][0:1, :], RET_DV) + s_new
        mu = jnp.mean(o, axis=-1, keepdims=True)
        dlt = o - mu
        var = jnp.mean(dlt * dlt, axis=-1, keepdims=True)
        yn = (dlt * lax.rsqrt(var + EPS)) * gn_ref[d:d + 1, :]
        return g_ref[rows, :].astype(F32) * yn

    def scores(rows):
        return lax.dot_general(q_ref[rows, :], k_ref[rows, :], (((1,), (1,)), ((), ())),
                               preferred_element_type=F32)

    if nc == 1:
        rows = pl.ds(0, C)
        att = scores(rows)
        y_ref[...] = (contribution(rows, 0, att) + contribution(rows, 1, att)).astype(y_ref.dtype)
    else:
        def pair(i, second_visit):
            for d, c in ((0, i), (1, nc - 1 - i)):
                rows = pl.ds(pl.multiple_of(c * C, C), C)
                contrib = contribution(rows, d, scores(rows))
                if second_visit:
                    y_ref[rows, :] = (yacc_ref[rows, :] + contrib).astype(y_ref.dtype)
                else:
                    yacc_ref[rows, :] = contrib

        def first(i, carry):
            pair(i, False)
            return carry

        def second(i, carry):
            pair(i, True)
            return carry

        lax.fori_loop(0, nc // 2, first, 0)
        lax.fori_loop(nc // 2, nc, second, 0)


def _retention_scan(proj, dec, gn, batch, seq, s0f=None, s0b=None, s0_slot=0, state_slot=None,
                    state_bufs=None, n_slots=1):
    has_s0 = s0f is not None
    want_state = state_slot is not None
    aliased = state_bufs is not None
    hv = RET_HEADS * RET_DV
    nkb = RET_HEADS
    nvb = 2 * RET_HEADS * RET_DK // RET_DV
    C = SCAN_CHUNK
    assert seq % C == 0 and (seq == C or (seq // C) % 2 == 0)
    in_specs = [
        pl.BlockSpec((seq, RET_DK), lambda h, b: (b, h)),
        pl.BlockSpec((seq, RET_DK), lambda h, b: (b, nkb + h)),
        pl.BlockSpec((seq, RET_DV), lambda h, b: (b, nvb + h)),
        pl.BlockSpec((seq, RET_DV), lambda h, b: (b, nvb + RET_HEADS + h)),
        pl.BlockSpec((seq, RET_DV), lambda h, b: (b, nvb + 2 * RET_HEADS + h)),
        pl.BlockSpec((2, None, C, C), lambda h, b: (0, h, 0, 0)),
        pl.BlockSpec((2, RET_DV), lambda h, b: (0, h)),
    ]
    args = [proj, proj, proj, proj, proj, dec, gn]
    if has_s0:
        st_spec = pl.BlockSpec((None, None, None, RET_DK, RET_DV), lambda h, b: (b, s0_slot, h, 0, 0))
        in_specs += [st_spec, st_spec]
        args += [s0f, s0b]
    aliases = {}
    if aliased:
        in_specs += [pl.BlockSpec(memory_space=pl.ANY)] * 2
        aliases = {len(args): 1, len(args) + 1: 2}
        args += list(state_bufs)
    out_shape = [jax.ShapeDtypeStruct((batch * seq, hv), BF16)]
    out_specs = [pl.BlockSpec((seq, RET_DV), lambda h, b: (b, h))]
    scratch = [pltpu.VMEM((2, C, C), F32), pltpu.VMEM((2, C, C), F32), pltpu.VMEM((2, C, C), F32),
               pltpu.VMEM((2, 8, C), F32), pltpu.VMEM((seq, RET_DV), F32)]
    if want_state:
        st = jax.ShapeDtypeStruct((batch, n_slots, RET_HEADS, RET_DK, RET_DV), F32)
        out_shape += [st, st]
        st_spec = pl.BlockSpec((None, None, None, RET_DK, RET_DV), lambda h, b: (b, state_slot, h, 0, 0))
        out_specs += [st_spec, st_spec]
    else:
        scratch = [pltpu.VMEM((RET_DK, RET_DV), F32)] * 2 + scratch
    return pl.pallas_call(
        functools.partial(_scan_kernel, seq=seq, has_s0=has_s0, aliased=aliased),
        out_shape=out_shape,
        grid=(RET_HEADS, batch),
        in_specs=in_specs,
        out_specs=out_specs,
        scratch_shapes=scratch,
        input_output_aliases=aliases,
        compiler_params=_params(("arbitrary", "arbitrary")),
        name="ret_scan",
    )(*args)


def _route(logits, bias):
    s = _sigmoid(logits)
    sb = s + bias
    lane = lax.broadcasted_iota(jnp.int32, sb.shape, 1)
    lane_f = lane.astype(F32)
    neg = -jnp.inf
    big = float(LANES)

    def top2(vals):
        m1 = jnp.max(vals, axis=-1, keepdims=True)
        i1 = jnp.min(jnp.where(vals == m1, lane_f, big), axis=-1, keepdims=True)
        vals2 = jnp.where(lane_f == i1, neg, vals)
        m2 = jnp.max(vals2, axis=-1, keepdims=True)
        i2 = jnp.min(jnp.where(vals2 == m2, lane_f, big), axis=-1, keepdims=True)
        return m1 + m2, i1, i2

    best = None
    for g in range(N_GROUPS):
        in_group = (lane // EXPERTS_PER_GROUP) == g
        gs, i1, i2 = top2(jnp.where(in_group, sb, neg))
        if best is None:
            best = (gs, i1, i2)
        else:
            take = gs > best[0]
            best = (jnp.where(take, gs, best[0]), jnp.where(take, i1, best[1]),
                    jnp.where(take, i2, best[2]))
    _, e1, e2 = best
    oh1 = lane_f == e1
    oh2 = lane_f == e2
    w1 = jnp.sum(jnp.where(oh1, s, 0.0), axis=-1, keepdims=True)
    w2 = jnp.sum(jnp.where(oh2, s, 0.0), axis=-1, keepdims=True)
    den = w1 + w2
    return jnp.where(oh1, w1 / den, 0.0) + jnp.where(oh2, w2 / den, 0.0)


def _split_bf16(a):
    hi = a.astype(BF16)
    lo = (a - hi.astype(F32)).astype(BF16)
    return hi, lo


def _mix_out_kernel(y_ref, w_ref, x_ref, g1_ref, n2_ref, sc_ref, sh_ref, wr_ref, rb_ref,
                    xo_ref, h_ref, gate_ref):
    x = x_ref[...] + g1_ref[...] * jnp.dot(y_ref[...], w_ref[...], preferred_element_type=F32)
    xo_ref[...] = x
    h = _normmod(x, n2_ref[...], sc_ref[...], sh_ref[...])
    h_ref[...] = h.astype(BF16)
    h_hi, h_lo = _split_bf16(h)
    w_hi, w_lo = _split_bf16(wr_ref[...])
    logits = (jnp.dot(h_hi, w_hi, preferred_element_type=F32)
              + jnp.dot(h_hi, w_lo, preferred_element_type=F32)
              + jnp.dot(h_lo, w_hi, preferred_element_type=F32))
    gate_ref[...] = _route(logits, rb_ref[...])


def _mix_out(y, w, x, mod, norm2_g, wr_pad, rb_pad, rows_per_cond):
    n, kdim = y.shape
    tm = 512
    full = lambda i: (0, 0)
    return pl.pallas_call(
        _mix_out_kernel,
        out_shape=[jax.ShapeDtypeStruct((n, D_MODEL), F32),
                   jax.ShapeDtypeStruct((n, D_MODEL), BF16),
                   jax.ShapeDtypeStruct((n, LANES), F32)],
        grid=(n // tm,),
        in_specs=[
            pl.BlockSpec((tm, kdim), lambda i: (i, 0)),
            pl.BlockSpec((kdim, D_MODEL), full),
            pl.BlockSpec((tm, D_MODEL), lambda i: (i, 0)),
            _mod_spec(2, tm, rows_per_cond, 1),
            pl.BlockSpec((1, D_MODEL), full),
            _mod_spec(4, tm, rows_per_cond, 1),
            _mod_spec(3, tm, rows_per_cond, 1),
            pl.BlockSpec((D_MODEL, LANES), full),
            pl.BlockSpec((1, LANES), full),
        ],
        out_specs=[pl.BlockSpec((tm, D_MODEL), lambda i: (i, 0)),
                   pl.BlockSpec((tm, D_MODEL), lambda i: (i, 0)),
                   pl.BlockSpec((tm, LANES), lambda i: (i, 0))],
        compiler_params=_params(("arbitrary",)),
        name="mix_out",
    )(y, w, x, mod, norm2_g, mod, mod, wr_pad, rb_pad)


def _moe_kernel(x_ref, h_ref, gate_ref, g2_ref, wg_ref, wu_ref, wd_ref, nf_ref, o_ref, acc_ref,
                *, final_norm):
    e = pl.program_id(1)

    @pl.when(e == 0)
    def _():
        acc_ref[...] = jnp.zeros_like(acc_ref)

    h = h_ref[...]
    a = jnp.dot(h, wg_ref[0], preferred_element_type=F32)
    u = jnp.dot(h, wu_ref[0], preferred_element_type=F32)
    gate = gate_ref[...]
    lane = lax.broadcasted_iota(jnp.int32, gate.shape, 1)
    ge = jnp.sum(jnp.where(lane == e, gate, 0.0), axis=-1, keepdims=True)
    act = ((a * _sigmoid(a)) * u * ge).astype(BF16)
    acc_ref[...] += jnp.dot(act, wd_ref[0], preferred_element_type=F32)

    @pl.when(e == N_EXPERTS - 1)
    def _():
        x = x_ref[...] + g2_ref[...] * acc_ref[...]
        if final_norm:
            x = _rms(x, nf_ref[...])
        o_ref[...] = x


def _moe(x, h, gate, mod, wg, wu, wd, norm_f, rows_per_cond, final_norm):
    n = x.shape[0]
    tm = 1024
    return pl.pallas_call(
        functools.partial(_moe_kernel, final_norm=final_norm),
        out_shape=jax.ShapeDtypeStruct((n, D_MODEL), F32),
        grid=(n // tm, N_EXPERTS),
        in_specs=[
            pl.BlockSpec((tm, D_MODEL), lambda i, e: (i, 0)),
            pl.BlockSpec((tm, D_MODEL), lambda i, e: (i, 0)),
            pl.BlockSpec((tm, LANES), lambda i, e: (i, 0)),
            _mod_spec(5, tm, rows_per_cond, 2),
            pl.BlockSpec((1, D_MODEL, EXPERT_FF), lambda i, e: (e, 0, 0)),
            pl.BlockSpec((1, D_MODEL, EXPERT_FF), lambda i, e: (e, 0, 0)),
            pl.BlockSpec((1, EXPERT_FF, D_MODEL), lambda i, e: (e, 0, 0)),
            pl.BlockSpec((1, D_MODEL), lambda i, e: (0, 0)),
        ],
        out_specs=pl.BlockSpec((tm, D_MODEL), lambda i, e: (i, 0)),
        scratch_shapes=[pltpu.VMEM((tm, D_MODEL), F32)],
        compiler_params=_params(("arbitrary", "arbitrary")),
        name="moe",
    )(x, h, gate, mod, wg, wu, wd, norm_f)


def _swap_pairs(x):
    lane = lax.broadcasted_iota(jnp.int32, x.shape, 1)
    first = (lane % MLA_ROPE) < (MLA_ROPE // 2)
    return jnp.where(first, pltpu.roll(x, LANES - MLA_ROPE // 2, axis=1),
                     pltpu.roll(x, MLA_ROPE // 2, axis=1))


def _mla_proj_kernel(*refs, rope):
    x_ref, g_ref, sc_ref, sh_ref, win_ref, qn_ref, kvn_ref, wuq_ref = refs[:8]
    pos = 8
    if rope:
        cos_ref, sin_ref = refs[pos:pos + 2]
        pos += 2
    q_ref, ckv_ref, kpe_ref = refs[pos:pos + 3]

    h = _normmod(x_ref[...], g_ref[...], sc_ref[...], sh_ref[...]).astype(BF16)
    p = jnp.dot(h, win_ref[...], preferred_element_type=F32)
    cq = _rms(p[:, :MLA_Q_RANK], qn_ref[...]).astype(BF16)
    ckv_ref[...] = _rms(p[:, MLA_Q_RANK:MLA_Q_RANK + MLA_KV_RANK], kvn_ref[...])
    kpe = p[:, MLA_Q_RANK + MLA_KV_RANK:]
    q = jnp.dot(cq, wuq_ref[...], preferred_element_type=F32)
    if rope:
        cos = cos_ref[...]
        sin = sin_ref[...]
        kpe = kpe * cos + _swap_pairs(kpe) * sin
    kpe_ref[...] = kpe
    for hd in range(MLA_HEADS):
        lo = hd * HEAD_W
        q_ref[:, lo:lo + LANES] = q[:, lo:lo + LANES].astype(BF16)
        qp = q[:, lo + LANES:lo + HEAD_W]
        if rope:
            qp = qp * cos + _swap_pairs(qp) * sin
        q_ref[:, lo + LANES:lo + HEAD_W] = qp.astype(BF16)


def _mla_proj(x, mod, norm_g, w_in, q_norm, kv_norm, w_uq, rows_per_cond, cos=None, sin=None):
    n = x.shape[0]
    tm = 512
    rope = cos is not None
    full = lambda i: (0, 0)
    nin = w_in.shape[1]
    in_specs = [
        pl.BlockSpec((tm, D_MODEL), lambda i: (i, 0)),
        pl.BlockSpec((1, D_MODEL), full),
        _mod_spec(1, tm, rows_per_cond, 1),
        _mod_spec(0, tm, rows_per_cond, 1),
        pl.BlockSpec((D_MODEL, nin), full),
        pl.BlockSpec((1, MLA_Q_RANK), full),
        pl.BlockSpec((1, MLA_KV_RANK), full),
        pl.BlockSpec((MLA_Q_RANK, MLA_HEADS * HEAD_W), full),
    ]
    args = [x, norm_g, mod, mod, w_in, q_norm, kv_norm, w_uq]
    if rope:
        nt = cos.shape[0] // tm
        tab = pl.BlockSpec((tm, LANES), lambda i: (i % nt, 0))
        in_specs += [tab, tab]
        args += [cos, sin]
    return pl.pallas_call(
        functools.partial(_mla_proj_kernel, rope=rope),
        out_shape=[jax.ShapeDtypeStruct((n, MLA_HEADS * HEAD_W), BF16),
                   jax.ShapeDtypeStruct((n, MLA_KV_RANK), F32),
                   jax.ShapeDtypeStruct((n, LANES), F32)],
        grid=(n // tm,),
        in_specs=in_specs,
        out_specs=[pl.BlockSpec((tm, MLA_HEADS * HEAD_W), lambda i: (i, 0)),
                   pl.BlockSpec((tm, MLA_KV_RANK), lambda i: (i, 0)),
                   pl.BlockSpec((tm, LANES), lambda i: (i, 0))],
        compiler_params=_params(("arbitrary",)),
        name="mla_proj",
    )(*args)


def _kv_up_kernel(ckv_ref, kpe_ref, wuk_ref, wuv_ref, k_ref, v_ref):
    c = ckv_ref[...].astype(BF16)
    kn = jnp.dot(c, wuk_ref[...], preferred_element_type=F32).astype(BF16)
    v_ref[...] = jnp.dot(c, wuv_ref[...], preferred_element_type=F32).astype(BF16)
    kpe = kpe_ref[...].astype(BF16)
    for hd in range(MLA_HEADS):
        k_ref[:, hd * HEAD_W:hd * HEAD_W + LANES] = kn[:, hd * MLA_NOPE:(hd + 1) * MLA_NOPE]
        k_ref[:, hd * HEAD_W + LANES:(hd + 1) * HEAD_W] = kpe


def _kv_up(ckv, kpe, w_uk, w_uv):
    n = ckv.shape[0]
    tk = 512
    full = lambda i: (0, 0)
    return pl.pallas_call(
        _kv_up_kernel,
        out_shape=[jax.ShapeDtypeStruct((n, MLA_HEADS * HEAD_W), BF16),
                   jax.ShapeDtypeStruct((n, MLA_HEADS * MLA_V), BF16)],
        grid=(n // tk,),
        in_specs=[
            pl.BlockSpec((tk, MLA_KV_RANK), lambda i: (i, 0)),
            pl.BlockSpec((tk, LANES), lambda i: (i, 0)),
            pl.BlockSpec((MLA_KV_RANK, MLA_HEADS * MLA_NOPE), full),
            pl.BlockSpec((MLA_KV_RANK, MLA_HEADS * MLA_V), full),
        ],
        out_specs=[pl.BlockSpec((tk, MLA_HEADS * HEAD_W), lambda i: (i, 0)),
                   pl.BlockSpec((tk, MLA_HEADS * MLA_V), lambda i: (i, 0))],
        compiler_params=_params(("arbitrary",)),
        name="kv_up",
    )(ckv, kpe, w_uk, w_uv)


def _attn_kernel(q_ref, k_ref, v_ref, o_ref, *, heads):
    scale = (MLA_NOPE + MLA_ROPE) ** -0.5
    for hd in range(heads):
        q = q_ref[:, hd * HEAD_W:(hd + 1) * HEAD_W]
        k = k_ref[:, hd * HEAD_W:(hd + 1) * HEAD_W]
        s = lax.dot_general(q, k, (((1,), (1,)), ((), ())), preferred_element_type=F32)
        m = jnp.max(s, axis=-1, keepdims=True)
        e = jnp.exp((s - m) * scale)
        den = jnp.sum(e, axis=-1, keepdims=True)
        o = jnp.dot(e.astype(BF16), v_ref[:, hd * MLA_V:(hd + 1) * MLA_V],
                    preferred_element_type=F32)
        o_ref[:, hd * MLA_V:(hd + 1) * MLA_V] = (o / den).astype(o_ref.dtype)


def _attention(q, k, v, batch, tq_total, tk_total, heads_per_step, tq):
    nq = tq_total // tq
    nh = MLA_HEADS // heads_per_step
    hp = heads_per_step
    return pl.pallas_call(
        functools.partial(_attn_kernel, heads=hp),
        out_shape=jax.ShapeDtypeStruct((batch * tq_total, MLA_HEADS * MLA_V), BF16),
        grid=(batch, nh, nq),
        in_specs=[
            pl.BlockSpec((tq, hp * HEAD_W), lambda b, h, i: (b * nq + i, h)),
            pl.BlockSpec((tk_total, hp * HEAD_W), lambda b, h, i: (b, h)),
            pl.BlockSpec((tk_total, hp * MLA_V), lambda b, h, i: (b, h)),
        ],
        out_specs=pl.BlockSpec((tq, hp * MLA_V), lambda b, h, i: (b * nq + i, h)),
        compiler_params=_params(("arbitrary", "arbitrary", "arbitrary")),
        name="attn",
    )(q, k, v)


_ROPE_PERM = np.concatenate([np.arange(0, 16), np.arange(32, 48), np.arange(16, 32), np.arange(48, 64)])


def _rope_tables(seq):
    pos = jnp.arange(seq)
    row = (pos // GRID_W).astype(F32)
    col = (pos % GRID_W).astype(F32)
    half = MLA_ROPE // 2
    inv = ROPE_BASE ** (-jnp.arange(0, half, 2, dtype=F32) / half)
    ang = jnp.concatenate([row[:, None] * inv, col[:, None] * inv], axis=1)
    cos = jnp.cos(ang)
    sin = jnp.sin(ang)
    pad = jnp.zeros((seq, LANES - MLA_ROPE), F32)
    return (jnp.concatenate([cos, cos, pad], axis=1),
            jnp.concatenate([-sin, sin, pad], axis=1))


def _prep_mla_weights(w_in, w_uq, w_uk, w_uv):
    nq = MLA_Q_RANK + MLA_KV_RANK
    w_in_p = jnp.concatenate(
        [w_in[:, :nq], w_in[:, nq:][:, _ROPE_PERM], jnp.zeros((D_MODEL, LANES - MLA_ROPE), F32)], axis=1)
    wq = w_uq.reshape(MLA_Q_RANK, MLA_HEADS, MLA_NOPE + MLA_ROPE)
    wq_p = jnp.concatenate(
        [wq[:, :, :MLA_NOPE], wq[:, :, MLA_NOPE:][:, :, _ROPE_PERM],
         jnp.zeros((MLA_Q_RANK, MLA_HEADS, HEAD_W - MLA_NOPE - MLA_ROPE), F32)], axis=2)
    return (w_in_p.astype(BF16), wq_p.reshape(MLA_Q_RANK, MLA_HEADS * HEAD_W).astype(BF16),
            w_uk.reshape(MLA_KV_RANK, MLA_HEADS * MLA_NOPE).astype(BF16),
            w_uv.reshape(MLA_KV_RANK, MLA_HEADS * MLA_V).astype(BF16))


def kernel(x_prompt, x_sample, state_ret_fwd, state_ret_bwd, cache_mla_ckv, cache_mla_kpe, c, c_ctx,
           w_ada, b_ada, norm1, norm2, norm_final, ret_w_in, ret_decay, ret_gn, ret_w_out,
           mla_w_in, mla_q_norm, mla_kv_norm, mla_w_uq, mla_w_uk, mla_w_uv, mla_w_o,
           w_router, router_bias, moe_w_gate, moe_w_up, moe_w_down):
    bp, tp, _ = x_prompt.shape
    bs, ts, _ = x_sample.shape
    past = cache_mla_ckv.shape[2]
    xp = x_prompt.reshape(bp * tp, D_MODEL)
    xs = x_sample.reshape(bs * ts, D_MODEL)

    conds = jnp.concatenate([c_ctx[None, :], c, jnp.zeros((8 - 1 - bs, D_MODEL), F32)], axis=0)
    mods = _ada_all(conds, w_ada, b_ada)

    wr_pad = jnp.pad(w_router, ((0, 0), (0, LANES - N_EXPERTS)))
    rb_pad = jnp.pad(router_bias, (0, LANES - N_EXPERTS)).reshape(1, LANES)
    nf = norm_final.reshape(1, D_MODEL)
    cos_t, sin_t = _rope_tables(ts)

    groups = {
        "p": dict(x=xp, rows_per_cond=bp * tp, lo=0, hi=1),
        "s": dict(x=xs, rows_per_cond=ts, lo=1, hi=1 + bs),
    }
    out_ckv, out_kpe = [], []
    ret_states = None
    n_ret = state_ret_fwd.shape[1]

    for i in range(DEPTH):
        j = i // 2
        n1 = norm1[i].reshape(1, D_MODEL)
        n2 = norm2[i].reshape(1, D_MODEL)
        wg = moe_w_gate[i].astype(BF16)
        wu = moe_w_up[i].astype(BF16)
        wd = moe_w_down[i].astype(BF16)
        mixed = {}
        if i % 2 == 0:
            w_in = ret_w_in[j].astype(BF16)
            w_out = ret_w_out[j].astype(BF16)
            dec = jnp.broadcast_to(ret_decay[j][:, :, None, None], (2, RET_HEADS, SCAN_CHUNK, SCAN_CHUNK))
            gn = ret_gn[j]
            for name, gr in groups.items():
                mod = mods[i, gr["lo"]:gr["hi"]].reshape(-1, 1, 6 * D_MODEL)
                proj = _proj(gr["x"], mod, n1, w_in, gr["rows_per_cond"],
                             2 * RET_HEADS * RET_DK + RET_HEADS * RET_DV)
                if name == "p":
                    y, *ret_states = _retention_scan(proj, dec, gn, bp, tp, state_slot=j,
                                                     state_bufs=ret_states, n_slots=n_ret)
                else:
                    (y,) = _retention_scan(proj, dec, gn, bs, ts, state_ret_fwd, state_ret_bwd, s0_slot=j)
                mixed[name] = (y, w_out, mod)
        else:
            w_in_p, w_uq_p, w_uk, w_uv = _prep_mla_weights(mla_w_in[j], mla_w_uq[j], mla_w_uk[j], mla_w_uv[j])
            w_o = mla_w_o[j].astype(BF16)
            qn = mla_q_norm[j].reshape(1, MLA_Q_RANK)
            kvn = mla_kv_norm[j].reshape(1, MLA_KV_RANK)
            for name, gr in groups.items():
                mod = mods[i, gr["lo"]:gr["hi"]].reshape(-1, 1, 6 * D_MODEL)
                if name == "p":
                    q, ckv, kpe = _mla_proj(gr["x"], mod, n1, w_in_p, qn, kvn, w_uq_p, gr["rows_per_cond"])
                    out_ckv.append(ckv.reshape(bp, tp, MLA_KV_RANK))
                    out_kpe.append(kpe[:, :MLA_ROPE][:, _ROPE_PERM].reshape(bp, tp, MLA_ROPE))
                    k, v = _kv_up(ckv, kpe, w_uk, w_uv)
                    o = _attention(q, k, v, bp, tp, tp, MLA_HEADS, tp)
                else:
                    q, ckv, kpe = _mla_proj(gr["x"], mod, n1, w_in_p, qn, kvn, w_uq_p, gr["rows_per_cond"],
                                            cos_t, sin_t)
                    cache_kpe = jnp.pad(cache_mla_kpe[:, j][:, :, _ROPE_PERM],
                                        ((0, 0), (0, 0), (0, LANES - MLA_ROPE)))
                    ckv_all = jnp.concatenate([ckv.reshape(bs, ts, MLA_KV_RANK), cache_mla_ckv[:, j]], axis=1)
                    kpe_all = jnp.concatenate([kpe.reshape(bs, ts, LANES), cache_kpe], axis=1)
                    tk = ts + past
                    k, v = _kv_up(ckv_all.reshape(bs * tk, MLA_KV_RANK), kpe_all.reshape(bs * tk, LANES),
                                  w_uk, w_uv)
                    o = _attention(q, k, v, bs, ts, tk, 1, 512)
                mixed[name] = (o, w_o, mod)

        for name, gr in groups.items():
            y, w_o, mod = mixed[name]
            x_new, h2, gate = _mix_out(y, w_o, gr["x"], mod, n2, wr_pad, rb_pad, gr["rows_per_cond"])
            gr["x"] = _moe(x_new, h2, gate, mod, wg, wu, wd, nf, gr["rows_per_cond"], i == DEPTH - 1)

    return (groups["p"]["x"].reshape(bp, tp, D_MODEL), groups["s"]["x"].reshape(bs, ts, D_MODEL),
            ret_states[0], ret_states[1],
            jnp.stack(out_ckv, axis=1), jnp.stack(out_kpe, axis=1))
```

```python
import functools

import numpy as np
import jax
import jax.numpy as jnp
from jax import lax
from jax.experimental import pallas as pl
from jax.experimental.pallas import tpu as pltpu

F32 = jnp.float32
BF16 = jnp.bfloat16

D_MODEL = 1024
DEPTH = 4
GRID_W = 64
RET_HEADS = 4
RET_DK = 256
RET_DV = 512
SCAN_CHUNK = 256
MLA_HEADS = 8
MLA_Q_RANK = 512
MLA_KV_RANK = 256
MLA_NOPE = 128
MLA_ROPE = 64
MLA_V = 128
ROPE_BASE = 10000.0
N_EXPERTS = 16
N_GROUPS = 4
EXPERTS_PER_GROUP = N_EXPERTS // N_GROUPS
EXPERT_FF = 256
EPS = 1e-6

LANES = 128
HEAD_W = 2 * LANES
VMEM_LIMIT = 48 * 1024 * 1024
MIX_SUB = 256
ATTN_SUB = 256


def _params(sem):
    return pltpu.CompilerParams(dimension_semantics=sem, vmem_limit_bytes=VMEM_LIMIT)


def _sigmoid(x):
    return 1.0 / (1.0 + jnp.exp(-x))


def _normmod(x, g, sc, sh):
    y = x * lax.rsqrt(jnp.mean(x * x, axis=-1, keepdims=True) + EPS)
    return (y * g) * (1.0 + sc) + sh


def _rms(x, g):
    return x * lax.rsqrt(jnp.mean(x * x, axis=-1, keepdims=True) + EPS) * g


def _ada_kernel(c_ref, w_ref, b_ref, o_ref):
    c = c_ref[...]
    s = (c * _sigmoid(c)).astype(BF16)
    o_ref[0] = jnp.dot(s, w_ref[0].astype(BF16), preferred_element_type=F32) + b_ref[0]


def _ada_all(conds, w_ada, b_ada):
    tn = 1536
    n6 = 6 * D_MODEL
    return pl.pallas_call(
        _ada_kernel,
        out_shape=jax.ShapeDtypeStruct((DEPTH, 8, n6), F32),
        grid=(DEPTH, n6 // tn),
        in_specs=[
            pl.BlockSpec((8, D_MODEL), lambda l, j: (0, 0)),
            pl.BlockSpec((1, D_MODEL, tn), lambda l, j: (l, 0, j)),
            pl.BlockSpec((1, 1, tn), lambda l, j: (l, 0, j)),
        ],
        out_specs=pl.BlockSpec((1, 8, tn), lambda l, j: (l, 0, j)),
        compiler_params=_params(("arbitrary", "arbitrary")),
        name="ada",
    )(conds, w_ada, b_ada.reshape(DEPTH, 1, n6))


def _mod_spec(which, tm, rows_per_cond, ngrid):
    if ngrid == 1:
        return pl.BlockSpec((None, 1, D_MODEL), lambda i: (i * tm // rows_per_cond, 0, which))
    return pl.BlockSpec((None, 1, D_MODEL), lambda i, j: (i * tm // rows_per_cond, 0, which))


def _silu(g):
    hg = 0.5 * g
    return hg + hg * jnp.tanh(hg)


def _proj_kernel(x_ref, g_ref, sc_ref, sh_ref, w_ref, o_ref, h_ref, *, silu_from):
    j = pl.program_id(1)

    @pl.when(j == 0)
    def _():
        h_ref[...] = _normmod(x_ref[...], g_ref[...], sc_ref[...], sh_ref[...]).astype(BF16)

    acc = jnp.dot(h_ref[...], w_ref[...], preferred_element_type=F32)
    o_ref[...] = jnp.where(j >= silu_from, _silu(acc), acc).astype(o_ref.dtype)


def _proj(x, mod, norm_g, w, rows_per_cond, silu_from_col):
    n = x.shape[0]
    nout = w.shape[1]
    tm, tn = 1024, 1024
    return pl.pallas_call(
        functools.partial(_proj_kernel, silu_from=silu_from_col // tn),
        out_shape=jax.ShapeDtypeStruct((n, nout), BF16),
        grid=(n // tm, nout // tn),
        in_specs=[
            pl.BlockSpec((tm, D_MODEL), lambda i, j: (i, 0)),
            pl.BlockSpec((1, D_MODEL), lambda i, j: (0, 0)),
            _mod_spec(1, tm, rows_per_cond, 2),
            _mod_spec(0, tm, rows_per_cond, 2),
            pl.BlockSpec((D_MODEL, tn), lambda i, j: (0, j)),
        ],
        out_specs=pl.BlockSpec((tm, tn), lambda i, j: (i, j)),
        scratch_shapes=[pltpu.VMEM((tm, D_MODEL), BF16)],
        compiler_params=_params(("arbitrary", "arbitrary")),
        name="ret_proj",
    )(x, norm_g, mod, mod, w)


def _scan_kernel(*refs, seq, has_s0, aliased):
    q_ref, k_ref, v_ref, gf_ref, gb_ref, dec_ref, gn_ref = refs[:7]
    pos = 7
    if has_s0:
        s0_refs = refs[pos:pos + 2]
        pos += 2
    if aliased:
        pos += 2
    y_ref = refs[pos]
    s_refs = refs[pos + 1:pos + 3]
    intra_ref, qdec_ref, kdec_ref, cdec_ref, yacc_ref = refs[pos + 3:pos + 8]

    C = SCAN_CHUNK
    nc = seq // C

    @pl.when(pl.program_id(1) == 0)
    def _():
        ri = lax.broadcasted_iota(jnp.int32, (C, C), 0).astype(F32)
        ci = lax.broadcasted_iota(jnp.int32, (C, C), 1).astype(F32)
        k_scale = RET_DK ** -0.5
        for d in range(2):
            lg = -jnp.exp(dec_ref[d])
            if d == 1:
                diff, q_pow, k_pow = ci - ri, C - ri, ri
            else:
                diff, q_pow, k_pow = ri - ci, ri + 1.0, C - 1.0 - ri
            intra_ref[d] = jnp.where(diff >= 0, jnp.exp(lg * jnp.maximum(diff, 0.0)), 0.0) * k_scale
            qdec_ref[d] = jnp.exp(lg * q_pow)
            kdec_ref[d] = jnp.exp(lg * k_pow) * k_scale
            cdec_ref[d] = jnp.exp(lg[0:8, :] * float(C))

    state_is_zero = not has_s0 and nc == 1
    for d in range(2):
        if has_s0:
            s_refs[d][...] = s0_refs[d][...]
        elif not state_is_zero:
            s_refs[d][...] = jnp.zeros_like(s_refs[d])

    def lanes(a, width):
        return jnp.concatenate([a] * (width // C), axis=1)

    def contribution(rows, d, att):
        g_ref = (gf_ref, gb_ref)[d]
        s_ref = s_refs[d]
        q = q_ref[rows, :]
        k = k_ref[rows, :]
        v = v_ref[rows, :]
        att = (att * intra_ref[d]).astype(BF16)
        o = jnp.dot(att, v, preferred_element_type=F32)
        kd = (k.astype(F32) * lanes(kdec_ref[d], RET_DK)).astype(BF16)
        s_new = lax.dot_general(kd, v, (((0,), (0,)), ((), ())), preferred_element_type=F32)
        if state_is_zero:
            s_ref[...] = s_new
        else:
            qd = (q.astype(F32) * lanes(qdec_ref[d], RET_DK)).astype(BF16)
            s_old = s_ref[...]
            o = o + jnp.dot(qd, s_old.astype(BF16), preferred_element_type=F32)
            s_ref[...] = s_old * lanes(cdec_ref[d][0:1, :], RET_DV) + s_new
        mu = jnp.mean(o, axis=-1, keepdims=True)
        dlt = o - mu
        var = jnp.mean(dlt * dlt, axis=-1, keepdims=True)
        yn = (dlt * lax.rsqrt(var + EPS)) * gn_ref[d:d + 1, :]
        return g_ref[rows, :].astype(F32) * yn

    def scores(rows):
        return lax.dot_general(q_ref[rows, :], k_ref[rows, :], (((1,), (1,)), ((), ())),
                               preferred_element_type=F32)

    if nc == 1:
        rows = pl.ds(0, C)
        att = scores(rows)
        y_ref[...] = (contribution(rows, 0, att) + contribution(rows, 1, att)).astype(y_ref.dtype)
    else:
        def pair(i, second_visit):
            for d, c in ((0, i), (1, nc - 1 - i)):
                rows = pl.ds(pl.multiple_of(c * C, C), C)
                contrib = contribution(rows, d, scores(rows))
                if second_visit:
                    y_ref[rows, :] = (yacc_ref[rows, :] + contrib).astype(y_ref.dtype)
                else:
                    yacc_ref[rows, :] = contrib

        def first(i, carry):
            pair(i, False)
            return carry

        def second(i, carry):
            pair(i, True)
            return carry

        lax.fori_loop(0, nc // 2, first, 0)
        lax.fori_loop(nc // 2, nc, second, 0)


def _retention_scan(proj, dec, gn, batch, seq, s0f=None, s0b=None, s0_slot=0, state_slot=None,
                    state_bufs=None, n_slots=1):
    has_s0 = s0f is not None
    want_state = state_slot is not None
    aliased = state_bufs is not None
    hv = RET_HEADS * RET_DV
    nkb = RET_HEADS
    nvb = 2 * RET_HEADS * RET_DK // RET_DV
    C = SCAN_CHUNK
    assert seq % C == 0 and (seq == C or (seq // C) % 2 == 0)
    in_specs = [
        pl.BlockSpec((seq, RET_DK), lambda h, b: (b, h)),
        pl.BlockSpec((seq, RET_DK), lambda h, b: (b, nkb + h)),
        pl.BlockSpec((seq, RET_DV), lambda h, b: (b, nvb + h)),
        pl.BlockSpec((seq, RET_DV), lambda h, b: (b, nvb + RET_HEADS + h)),
        pl.BlockSpec((seq, RET_DV), lambda h, b: (b, nvb + 2 * RET_HEADS + h)),
        pl.BlockSpec((2, None, C, C), lambda h, b: (0, h, 0, 0)),
        pl.BlockSpec((2, RET_DV), lambda h, b: (0, h)),
    ]
    args = [proj, proj, proj, proj, proj, dec, gn]
    if has_s0:
        st_spec = pl.BlockSpec((None, None, None, RET_DK, RET_DV), lambda h, b: (b, s0_slot, h, 0, 0))
        in_specs += [st_spec, st_spec]
        args += [s0f, s0b]
    aliases = {}
    if aliased:
        in_specs += [pl.BlockSpec(memory_space=pl.ANY)] * 2
        aliases = {len(args): 1, len(args) + 1: 2}
        args += list(state_bufs)
    out_shape = [jax.ShapeDtypeStruct((batch * seq, hv), BF16)]
    out_specs = [pl.BlockSpec((seq, RET_DV), lambda h, b: (b, h))]
    scratch = [pltpu.VMEM((2, C, C), F32), pltpu.VMEM((2, C, C), F32), pltpu.VMEM((2, C, C), F32),
               pltpu.VMEM((2, 8, C), F32), pltpu.VMEM((seq, RET_DV), F32)]
    if want_state:
        st = jax.ShapeDtypeStruct((batch, n_slots, RET_HEADS, RET_DK, RET_DV), F32)
        out_shape += [st, st]
        st_spec = pl.BlockSpec((None, None, None, RET_DK, RET_DV), lambda h, b: (b, state_slot, h, 0, 0))
        out_specs += [st_spec, st_spec]
    else:
        scratch = [pltpu.VMEM((RET_DK, RET_DV), F32)] * 2 + scratch
    return pl.pallas_call(
        functools.partial(_scan_kernel, seq=seq, has_s0=has_s0, aliased=aliased),
        out_shape=out_shape,
        grid=(RET_HEADS, batch),
        in_specs=in_specs,
        out_specs=out_specs,
        scratch_shapes=scratch,
        input_output_aliases=aliases,
        compiler_params=_params(("arbitrary", "arbitrary")),
        name="ret_scan",
    )(*args)


def _route(logits, bias):
    s = _sigmoid(logits)
    sb = s + bias
    lane = lax.broadcasted_iota(jnp.int32, sb.shape, 1)
    lane_f = lane.astype(F32)
    neg = -jnp.inf
    big = float(LANES)

    def top2(vals):
        m1 = jnp.max(vals, axis=-1, keepdims=True)
        i1 = jnp.min(jnp.where(vals == m1, lane_f, big), axis=-1, keepdims=True)
        vals2 = jnp.where(lane_f == i1, neg, vals)
        m2 = jnp.max(vals2, axis=-1, keepdims=True)
        i2 = jnp.min(jnp.where(vals2 == m2, lane_f, big), axis=-1, keepdims=True)
        return m1 + m2, i1, i2

    best = None
    for g in range(N_GROUPS):
        in_group = (lane // EXPERTS_PER_GROUP) == g
        gs, i1, i2 = top2(jnp.where(in_group, sb, neg))
        if best is None:
            best = (gs, i1, i2)
        else:
            take = gs > best[0]
            best = (jnp.where(take, gs, best[0]), jnp.where(take, i1, best[1]),
                    jnp.where(take, i2, best[2]))
    _, e1, e2 = best
    oh1 = lane_f == e1
    oh2 = lane_f == e2
    w1 = jnp.sum(jnp.where(oh1, s, 0.0), axis=-1, keepdims=True)
    w2 = jnp.sum(jnp.where(oh2, s, 0.0), axis=-1, keepdims=True)
    den = w1 + w2
    return jnp.where(oh1, w1 / den, 0.0) + jnp.where(oh2, w2 / den, 0.0)


def _split_bf16(a):
    hi = a.astype(BF16)
    lo = (a - hi.astype(F32)).astype(BF16)
    return hi, lo


def _mix_out_kernel(y_ref, w_ref, x_ref, g1_ref, n2_ref, sc_ref, sh_ref, wr_ref, rb_ref,
                    xo_ref, h_ref, gate_ref):
    w_hi, w_lo = _split_bf16(wr_ref[...])
    w_cat = jnp.concatenate([w_hi, w_lo], axis=1)
    for r in range(y_ref.shape[0] // MIX_SUB):
        rows = slice(r * MIX_SUB, (r + 1) * MIX_SUB)
        x = x_ref[rows, :] + g1_ref[...] * jnp.dot(y_ref[rows, :], w_ref[...], preferred_element_type=F32)
        xo_ref[rows, :] = x
        h = _normmod(x, n2_ref[...], sc_ref[...], sh_ref[...])
        h_ref[rows, :] = h.astype(BF16)
        h_hi, h_lo = _split_bf16(h)
        both = jnp.dot(h_hi, w_cat, preferred_element_type=F32)
        logits = both[:, :LANES] + both[:, LANES:] + jnp.dot(h_lo, w_hi, preferred_element_type=F32)
        gate_ref[rows, :] = _route(logits, rb_ref[...])


def _mix_out(y, w, x, mod, norm2_g, wr_pad, rb_pad, rows_per_cond):
    n, kdim = y.shape
    tm = 512
    full = lambda i: (0, 0)
    return pl.pallas_call(
        _mix_out_kernel,
        out_shape=[jax.ShapeDtypeStruct((n, D_MODEL), F32),
                   jax.ShapeDtypeStruct((n, D_MODEL), BF16),
                   jax.ShapeDtypeStruct((n, LANES), F32)],
        grid=(n // tm,),
        in_specs=[
            pl.BlockSpec((tm, kdim), lambda i: (i, 0)),
            pl.BlockSpec((kdim, D_MODEL), full),
            pl.BlockSpec((tm, D_MODEL), lambda i: (i, 0)),
            _mod_spec(2, tm, rows_per_cond, 1),
            pl.BlockSpec((1, D_MODEL), full),
            _mod_spec(4, tm, rows_per_cond, 1),
            _mod_spec(3, tm, rows_per_cond, 1),
            pl.BlockSpec((D_MODEL, LANES), full),
            pl.BlockSpec((1, LANES), full),
        ],
        out_specs=[pl.BlockSpec((tm, D_MODEL), lambda i: (i, 0)),
                   pl.BlockSpec((tm, D_MODEL), lambda i: (i, 0)),
                   pl.BlockSpec((tm, LANES), lambda i: (i, 0))],
        compiler_params=_params(("arbitrary",)),
        name="mix_out",
    )(y, w, x, mod, norm2_g, mod, mod, wr_pad, rb_pad)


def _moe_kernel(x_ref, h_ref, gate_ref, g2_ref, wg_ref, wu_ref, wd_ref, nf_ref, o_ref, acc_ref,
                *, final_norm):
    e = pl.program_id(1)

    @pl.when(e == 0)
    def _():
        acc_ref[...] = jnp.zeros_like(acc_ref)

    h = h_ref[...]
    a = jnp.dot(h, wg_ref[0], preferred_element_type=F32)
    u = jnp.dot(h, wu_ref[0], preferred_element_type=F32)
    gate = gate_ref[...]
    lane = lax.broadcasted_iota(jnp.int32, gate.shape, 1)
    ge = jnp.sum(jnp.where(lane == e, gate, 0.0), axis=-1, keepdims=True)
    act = ((a * _sigmoid(a)) * u * ge).astype(BF16)
    acc_ref[...] += jnp.dot(act, wd_ref[0], preferred_element_type=F32)

    @pl.when(e == N_EXPERTS - 1)
    def _():
        x = x_ref[...] + g2_ref[...] * acc_ref[...]
        if final_norm:
            x = _rms(x, nf_ref[...])
        o_ref[...] = x


def _moe(x, h, gate, mod, wg, wu, wd, norm_f, rows_per_cond, final_norm):
    n = x.shape[0]
    tm = 1024
    return pl.pallas_call(
        functools.partial(_moe_kernel, final_norm=final_norm),
        out_shape=jax.ShapeDtypeStruct((n, D_MODEL), F32),
        grid=(n // tm, N_EXPERTS),
        in_specs=[
            pl.BlockSpec((tm, D_MODEL), lambda i, e: (i, 0)),
            pl.BlockSpec((tm, D_MODEL), lambda i, e: (i, 0)),
            pl.BlockSpec((tm, LANES), lambda i, e: (i, 0)),
            _mod_spec(5, tm, rows_per_cond, 2),
            pl.BlockSpec((1, D_MODEL, EXPERT_FF), lambda i, e: (e, 0, 0)),
            pl.BlockSpec((1, D_MODEL, EXPERT_FF), lambda i, e: (e, 0, 0)),
            pl.BlockSpec((1, EXPERT_FF, D_MODEL), lambda i, e: (e, 0, 0)),
            pl.BlockSpec((1, D_MODEL), lambda i, e: (0, 0)),
        ],
        out_specs=pl.BlockSpec((tm, D_MODEL), lambda i, e: (i, 0)),
        scratch_shapes=[pltpu.VMEM((tm, D_MODEL), F32)],
        compiler_params=_params(("arbitrary", "arbitrary")),
        name="moe",
    )(x, h, gate, mod, wg, wu, wd, norm_f)


def _swap_pairs(x):
    lane = lax.broadcasted_iota(jnp.int32, x.shape, 1)
    first = (lane % MLA_ROPE) < (MLA_ROPE // 2)
    return jnp.where(first, pltpu.roll(x, LANES - MLA_ROPE // 2, axis=1),
                     pltpu.roll(x, MLA_ROPE // 2, axis=1))


def _mla_proj_kernel(*refs, rope):
    x_ref, g_ref, sc_ref, sh_ref, win_ref, qn_ref, kvn_ref, wuq_ref = refs[:8]
    pos = 8
    if rope:
        cos_ref, sin_ref = refs[pos:pos + 2]
        pos += 2
    q_ref, ckv_ref, kpe_ref = refs[pos:pos + 3]

    h = _normmod(x_ref[...], g_ref[...], sc_ref[...], sh_ref[...]).astype(BF16)
    p = jnp.dot(h, win_ref[...], preferred_element_type=F32)
    cq = _rms(p[:, :MLA_Q_RANK], qn_ref[...]).astype(BF16)
    ckv_ref[...] = _rms(p[:, MLA_Q_RANK:MLA_Q_RANK + MLA_KV_RANK], kvn_ref[...])
    kpe = p[:, MLA_Q_RANK + MLA_KV_RANK:]
    q = jnp.dot(cq, wuq_ref[...], preferred_element_type=F32) * (
        (MLA_NOPE + MLA_ROPE) ** -0.5 * float(np.log2(np.e)))
    if rope:
        cos = cos_ref[...]
        sin = sin_ref[...]
        kpe = kpe * cos + _swap_pairs(kpe) * sin
    kpe_ref[...] = kpe
    for hd in range(MLA_HEADS):
        lo = hd * HEAD_W
        q_ref[:, lo:lo + LANES] = q[:, lo:lo + LANES].astype(BF16)
        qp = q[:, lo + LANES:lo + HEAD_W]
        if rope:
            qp = qp * cos + _swap_pairs(qp) * sin
        q_ref[:, lo + LANES:lo + HEAD_W] = qp.astype(BF16)


def _mla_proj(x, mod, norm_g, w_in, q_norm, kv_norm, w_uq, rows_per_cond, cos=None, sin=None):
    n = x.shape[0]
    tm = 512
    rope = cos is not None
    full = lambda i: (0, 0)
    nin = w_in.shape[1]
    in_specs = [
        pl.BlockSpec((tm, D_MODEL), lambda i: (i, 0)),
        pl.BlockSpec((1, D_MODEL), full),
        _mod_spec(1, tm, rows_per_cond, 1),
        _mod_spec(0, tm, rows_per_cond, 1),
        pl.BlockSpec((D_MODEL, nin), full),
        pl.BlockSpec((1, MLA_Q_RANK), full),
        pl.BlockSpec((1, MLA_KV_RANK), full),
        pl.BlockSpec((MLA_Q_RANK, MLA_HEADS * HEAD_W), full),
    ]
    args = [x, norm_g, mod, mod, w_in, q_norm, kv_norm, w_uq]
    if rope:
        nt = cos.shape[0] // tm
        tab = pl.BlockSpec((tm, LANES), lambda i: (i % nt, 0))
        in_specs += [tab, tab]
        args += [cos, sin]
    return pl.pallas_call(
        functools.partial(_mla_proj_kernel, rope=rope),
        out_shape=[jax.ShapeDtypeStruct((n, MLA_HEADS * HEAD_W), BF16),
                   jax.ShapeDtypeStruct((n, MLA_KV_RANK), F32),
                   jax.ShapeDtypeStruct((n, LANES), F32)],
        grid=(n // tm,),
        in_specs=in_specs,
        out_specs=[pl.BlockSpec((tm, MLA_HEADS * HEAD_W), lambda i: (i, 0)),
                   pl.BlockSpec((tm, MLA_KV_RANK), lambda i: (i, 0)),
                   pl.BlockSpec((tm, LANES), lambda i: (i, 0))],
        compiler_params=_params(("arbitrary",)),
        name="mla_proj",
    )(*args)


def _kv_up_kernel(ckv_ref, kpe_ref, wuk_ref, wuv_ref, k_ref, v_ref):
    c = ckv_ref[...].astype(BF16)
    kn = jnp.dot(c, wuk_ref[...], preferred_element_type=F32).astype(BF16)
    vv = jnp.dot(c, wuv_ref[...], preferred_element_type=F32).astype(BF16)
    kpe = kpe_ref[...].astype(BF16)
    lane = lax.broadcasted_iota(jnp.int32, kpe.shape, 1)
    ones_col = jnp.where(lane == 0, 1.0, 0.0).astype(BF16)
    for hd in range(MLA_HEADS):
        k_ref[:, hd * HEAD_W:hd * HEAD_W + LANES] = kn[:, hd * MLA_NOPE:(hd + 1) * MLA_NOPE]
        k_ref[:, hd * HEAD_W + LANES:(hd + 1) * HEAD_W] = kpe
        v_ref[:, hd * HEAD_W:hd * HEAD_W + LANES] = vv[:, hd * MLA_V:(hd + 1) * MLA_V]
        v_ref[:, hd * HEAD_W + LANES:(hd + 1) * HEAD_W] = ones_col


def _kv_up(ckv, kpe, w_uk, w_uv):
    n = ckv.shape[0]
    tk = 512
    full = lambda i: (0, 0)
    return pl.pallas_call(
        _kv_up_kernel,
        out_shape=[jax.ShapeDtypeStruct((n, MLA_HEADS * HEAD_W), BF16),
                   jax.ShapeDtypeStruct((n, MLA_HEADS * HEAD_W), BF16)],
        grid=(n // tk,),
        in_specs=[
            pl.BlockSpec((tk, MLA_KV_RANK), lambda i: (i, 0)),
            pl.BlockSpec((tk, LANES), lambda i: (i, 0)),
            pl.BlockSpec((MLA_KV_RANK, MLA_HEADS * MLA_NOPE), full),
            pl.BlockSpec((MLA_KV_RANK, MLA_HEADS * MLA_V), full),
        ],
        out_specs=[pl.BlockSpec((tk, MLA_HEADS * HEAD_W), lambda i: (i, 0)),
                   pl.BlockSpec((tk, MLA_HEADS * HEAD_W), lambda i: (i, 0))],
        compiler_params=_params(("arbitrary",)),
        name="kv_up",
    )(ckv, kpe, w_uk, w_uv)


def _attn_kernel(q_ref, k_ref, v_ref, o_ref, *, heads):
    for hd in range(heads):
        k = k_ref[:, hd * HEAD_W:(hd + 1) * HEAD_W]
        v = v_ref[:, hd * HEAD_W:(hd + 1) * HEAD_W]
        for r in range(q_ref.shape[0] // ATTN_SUB):
            rows = slice(r * ATTN_SUB, (r + 1) * ATTN_SUB)
            q = q_ref[rows, hd * HEAD_W:(hd + 1) * HEAD_W]
            s = lax.dot_general(q, k, (((1,), (1,)), ((), ())), preferred_element_type=F32)
            m = jnp.max(s, axis=-1, keepdims=True)
            p = jnp.exp2(s - m).astype(BF16)
            o = jnp.dot(p, v, preferred_element_type=F32)
            o_ref[rows, hd * MLA_V:(hd + 1) * MLA_V] = (
                o[:, :MLA_V] / o[:, MLA_V:MLA_V + 1]).astype(o_ref.dtype)


def _attention(q, k, v, batch, tq_total, tk_total, heads_per_step, tq):
    nq = tq_total // tq
    nh = MLA_HEADS // heads_per_step
    hp = heads_per_step
    return pl.pallas_call(
        functools.partial(_attn_kernel, heads=hp),
        out_shape=jax.ShapeDtypeStruct((batch * tq_total, MLA_HEADS * MLA_V), BF16),
        grid=(batch, nh, nq),
        in_specs=[
            pl.BlockSpec((tq, hp * HEAD_W), lambda b, h, i: (b * nq + i, h)),
            pl.BlockSpec((tk_total, hp * HEAD_W), lambda b, h, i: (b, h)),
            pl.BlockSpec((tk_total, hp * HEAD_W), lambda b, h, i: (b, h)),
        ],
        out_specs=pl.BlockSpec((tq, hp * MLA_V), lambda b, h, i: (b * nq + i, h)),
        compiler_params=_params(("arbitrary", "arbitrary", "arbitrary")),
        name="attn",
    )(q, k, v)


_ROPE_PERM = np.concatenate([np.arange(0, 16), np.arange(32, 48), np.arange(16, 32), np.arange(48, 64)])


def _rope_tables(seq):
    pos = jnp.arange(seq)
    row = (pos // GRID_W).astype(F32)
    col = (pos % GRID_W).astype(F32)
    half = MLA_ROPE // 2
    inv = ROPE_BASE ** (-jnp.arange(0, half, 2, dtype=F32) / half)
    ang = jnp.concatenate([row[:, None] * inv, col[:, None] * inv], axis=1)
    cos = jnp.cos(ang)
    sin = jnp.sin(ang)
    pad = jnp.zeros((seq, LANES - MLA_ROPE), F32)
    return (jnp.concatenate([cos, cos, pad], axis=1),
            jnp.concatenate([-sin, sin, pad], axis=1))


def _prep_mla_weights(w_in, w_uq, w_uk, w_uv):
    nq = MLA_Q_RANK + MLA_KV_RANK
    w_in_p = jnp.concatenate(
        [w_in[:, :nq], w_in[:, nq:][:, _ROPE_PERM], jnp.zeros((D_MODEL, LANES - MLA_ROPE), F32)], axis=1)
    wq = w_uq.reshape(MLA_Q_RANK, MLA_HEADS, MLA_NOPE + MLA_ROPE)
    wq_p = jnp.concatenate(
        [wq[:, :, :MLA_NOPE], wq[:, :, MLA_NOPE:][:, :, _ROPE_PERM],
         jnp.zeros((MLA_Q_RANK, MLA_HEADS, HEAD_W - MLA_NOPE - MLA_ROPE), F32)], axis=2)
    return (w_in_p.astype(BF16), wq_p.reshape(MLA_Q_RANK, MLA_HEADS * HEAD_W).astype(BF16),
            w_uk.reshape(MLA_KV_RANK, MLA_HEADS * MLA_NOPE).astype(BF16),
            w_uv.reshape(MLA_KV_RANK, MLA_HEADS * MLA_V).astype(BF16))


def kernel(x_prompt, x_sample, state_ret_fwd, state_ret_bwd, cache_mla_ckv, cache_mla_kpe, c, c_ctx,
           w_ada, b_ada, norm1, norm2, norm_final, ret_w_in, ret_decay, ret_gn, ret_w_out,
           mla_w_in, mla_q_norm, mla_kv_norm, mla_w_uq, mla_w_uk, mla_w_uv, mla_w_o,
           w_router, router_bias, moe_w_gate, moe_w_up, moe_w_down):
    bp, tp, _ = x_prompt.shape
    bs, ts, _ = x_sample.shape
    past = cache_mla_ckv.shape[2]
    xp = x_prompt.reshape(bp * tp, D_MODEL)
    xs = x_sample.reshape(bs * ts, D_MODEL)

    conds = jnp.concatenate([c_ctx[None, :], c, jnp.zeros((8 - 1 - bs, D_MODEL), F32)], axis=0)
    mods = _ada_all(conds, w_ada, b_ada)

    wr_pad = jnp.pad(w_router, ((0, 0), (0, LANES - N_EXPERTS)))
    rb_pad = jnp.pad(router_bias, (0, LANES - N_EXPERTS)).reshape(1, LANES)
    nf = norm_final.reshape(1, D_MODEL)
    cos_t, sin_t = _rope_tables(ts)

    groups = {
        "p": dict(x=xp, rows_per_cond=bp * tp, lo=0, hi=1),
        "s": dict(x=xs, rows_per_cond=ts, lo=1, hi=1 + bs),
    }
    out_ckv, out_kpe = [], []
    ret_states = None
    n_ret = state_ret_fwd.shape[1]

    for i in range(DEPTH):
        j = i // 2
        n1 = norm1[i].reshape(1, D_MODEL)
        n2 = norm2[i].reshape(1, D_MODEL)
        wg = moe_w_gate[i].astype(BF16)
        wu = moe_w_up[i].astype(BF16)
        wd = moe_w_down[i].astype(BF16)
        mixed = {}
        if i % 2 == 0:
            w_in = ret_w_in[j].astype(BF16)
            w_out = ret_w_out[j].astype(BF16)
            dec = jnp.broadcast_to(ret_decay[j][:, :, None, None], (2, RET_HEADS, SCAN_CHUNK, SCAN_CHUNK))
            gn = ret_gn[j]
            for name, gr in groups.items():
                mod = mods[i, gr["lo"]:gr["hi"]].reshape(-1, 1, 6 * D_MODEL)
                proj = _proj(gr["x"], mod, n1, w_in, gr["rows_per_cond"],
                             2 * RET_HEADS * RET_DK + RET_HEADS * RET_DV)
                if name == "p":
                    y, *ret_states = _retention_scan(proj, dec, gn, bp, tp, state_slot=j,
                                                     state_bufs=ret_states, n_slots=n_ret)
                else:
                    (y,) = _retention_scan(proj, dec, gn, bs, ts, state_ret_fwd, state_ret_bwd, s0_slot=j)
                mixed[name] = (y, w_out, mod)
        else:
            w_in_p, w_uq_p, w_uk, w_uv = _prep_mla_weights(mla_w_in[j], mla_w_uq[j], mla_w_uk[j], mla_w_uv[j])
            w_o = mla_w_o[j].astype(BF16)
            qn = mla_q_norm[j].reshape(1, MLA_Q_RANK)
            kvn = mla_kv_norm[j].reshape(1, MLA_KV_RANK)
            for name, gr in groups.items():
                mod = mods[i, gr["lo"]:gr["hi"]].reshape(-1, 1, 6 * D_MODEL)
                if name == "p":
                    q, ckv, kpe = _mla_proj(gr["x"], mod, n1, w_in_p, qn, kvn, w_uq_p, gr["rows_per_cond"])
                    out_ckv.append(ckv.reshape(bp, tp, MLA_KV_RANK))
                    out_kpe.append(kpe[:, :MLA_ROPE][:, _ROPE_PERM].reshape(bp, tp, MLA_ROPE))
                    k, v = _kv_up(ckv, kpe, w_uk, w_uv)
                    o = _attention(q, k, v, bp, tp, tp, MLA_HEADS, tp)
                else:
                    q, ckv, kpe = _mla_proj(gr["x"], mod, n1, w_in_p, qn, kvn, w_uq_p, gr["rows_per_cond"],
                                            cos_t, sin_t)
                    cache_kpe = jnp.pad(cache_mla_kpe[:, j][:, :, _ROPE_PERM],
                                        ((0, 0), (0, 0), (0, LANES - MLA_ROPE)))
                    ckv_all = jnp.concatenate([ckv.reshape(bs, ts, MLA_KV_RANK), cache_mla_ckv[:, j]], axis=1)
                    kpe_all = jnp.concatenate([kpe.reshape(bs, ts, LANES), cache_kpe], axis=1)
                    tk = ts + past
                    k, v = _kv_up(ckv_all.reshape(bs * tk, MLA_KV_RANK), kpe_all.reshape(bs * tk, LANES),
                                  w_uk, w_uv)
                    o = _attention(q, k, v, bs, ts, tk, 1, 2048)
                mixed[name] = (o, w_o, mod)

        for name, gr in groups.items():
            y, w_o, mod = mixed[name]
            x_new, h2, gate = _mix_out(y, w_o, gr["x"], mod, n2, wr_pad, rb_pad, gr["rows_per_cond"])
            gr["x"] = _moe(x_new, h2, gate, mod, wg, wu, wd, nf, gr["rows_per_cond"], i == DEPTH - 1)

    return (groups["p"]["x"].reshape(bp, tp, D_MODEL), groups["s"]["x"].reshape(bs, ts, D_MODEL),
            ret_states[0], ret_states[1],
            jnp.stack(out_ckv, axis=1), jnp.stack(out_kpe, axis=1))
```

```python
import functools

import numpy as np
import jax
import jax.numpy as jnp
from jax import lax
from jax.experimental import pallas as pl
from jax.experimental.pallas import tpu as pltpu

F32 = jnp.float32
BF16 = jnp.bfloat16

D_MODEL = 1024
DEPTH = 4
GRID_W = 64
RET_HEADS = 4
RET_DK = 256
RET_DV = 512
SCAN_CHUNK = 256
MLA_HEADS = 8
MLA_Q_RANK = 512
MLA_KV_RANK = 256
MLA_NOPE = 128
MLA_ROPE = 64
MLA_V = 128
ROPE_BASE = 10000.0
N_EXPERTS = 16
N_GROUPS = 4
EXPERTS_PER_GROUP = N_EXPERTS // N_GROUPS
EXPERT_FF = 256
EPS = 1e-6

LANES = 128
HEAD_W = 2 * LANES
VMEM_LIMIT = 48 * 1024 * 1024
MIX_SUB = 256
ATTN_SUB = 256
MOE_EG = 4


def _params(sem):
    return pltpu.CompilerParams(dimension_semantics=sem, vmem_limit_bytes=VMEM_LIMIT)


def _sigmoid(x):
    return 1.0 / (1.0 + jnp.exp(-x))


def _normmod(x, g, sc, sh):
    y = x * lax.rsqrt(jnp.mean(x * x, axis=-1, keepdims=True) + EPS)
    return (y * g) * (1.0 + sc) + sh


def _rms(x, g):
    return x * lax.rsqrt(jnp.mean(x * x, axis=-1, keepdims=True) + EPS) * g


def _ada_kernel(c_ref, w_ref, b_ref, o_ref):
    c = c_ref[...]
    s = (c * _sigmoid(c)).astype(BF16)
    o_ref[0] = jnp.dot(s, w_ref[0].astype(BF16), preferred_element_type=F32) + b_ref[0]


def _ada_all(conds, w_ada, b_ada):
    tn = 1536
    n6 = 6 * D_MODEL
    return pl.pallas_call(
        _ada_kernel,
        out_shape=jax.ShapeDtypeStruct((DEPTH, 8, n6), F32),
        grid=(DEPTH, n6 // tn),
        in_specs=[
            pl.BlockSpec((8, D_MODEL), lambda l, j: (0, 0)),
            pl.BlockSpec((1, D_MODEL, tn), lambda l, j: (l, 0, j)),
            pl.BlockSpec((1, 1, tn), lambda l, j: (l, 0, j)),
        ],
        out_specs=pl.BlockSpec((1, 8, tn), lambda l, j: (l, 0, j)),
        compiler_params=_params(("arbitrary", "arbitrary")),
        name="ada",
    )(conds, w_ada, b_ada.reshape(DEPTH, 1, n6))


def _mod_spec(which, tm, rows_per_cond, ngrid):
    if ngrid == 1:
        return pl.BlockSpec((None, 1, D_MODEL), lambda i: (i * tm // rows_per_cond, 0, which))
    return pl.BlockSpec((None, 1, D_MODEL), lambda i, j: (i * tm // rows_per_cond, 0, which))


def _silu(g):
    hg = 0.5 * g
    return hg + hg * jnp.tanh(hg)


def _proj_kernel(x_ref, g_ref, sc_ref, sh_ref, w_ref, o_ref, h_ref, *, silu_from):
    j = pl.program_id(1)

    @pl.when(j == 0)
    def _():
        h_ref[...] = _normmod(x_ref[...], g_ref[...], sc_ref[...], sh_ref[...]).astype(BF16)

    acc = jnp.dot(h_ref[...], w_ref[...], preferred_element_type=F32)
    o_ref[...] = jnp.where(j >= silu_from, _silu(acc), acc).astype(o_ref.dtype)


def _proj(x, mod, norm_g, w, rows_per_cond, silu_from_col):
    n = x.shape[0]
    nout = w.shape[1]
    tm, tn = 1024, 2048
    return pl.pallas_call(
        functools.partial(_proj_kernel, silu_from=silu_from_col // tn),
        out_shape=jax.ShapeDtypeStruct((n, nout), BF16),
        grid=(n // tm, nout // tn),
        in_specs=[
            pl.BlockSpec((tm, D_MODEL), lambda i, j: (i, 0)),
            pl.BlockSpec((1, D_MODEL), lambda i, j: (0, 0)),
            _mod_spec(1, tm, rows_per_cond, 2),
            _mod_spec(0, tm, rows_per_cond, 2),
            pl.BlockSpec((D_MODEL, tn), lambda i, j: (0, j)),
        ],
        out_specs=pl.BlockSpec((tm, tn), lambda i, j: (i, j)),
        scratch_shapes=[pltpu.VMEM((tm, D_MODEL), BF16)],
        compiler_params=_params(("arbitrary", "arbitrary")),
        name="ret_proj",
    )(x, norm_g, mod, mod, w)


def _scan_kernel(*refs, seq, has_s0, aliased):
    q_ref, k_ref, v_ref, gf_ref, gb_ref, dec_ref, gn_ref = refs[:7]
    pos = 7
    if has_s0:
        s0_refs = refs[pos:pos + 2]
        pos += 2
    if aliased:
        pos += 2
    y_ref = refs[pos]
    s_refs = refs[pos + 1:pos + 3]
    intra_ref, qdec_ref, kdec_ref, cdec_ref, yacc_ref = refs[pos + 3:pos + 8]

    C = SCAN_CHUNK
    nc = seq // C

    @pl.when(pl.program_id(1) == 0)
    def _():
        ri = lax.broadcasted_iota(jnp.int32, (C, C), 0).astype(F32)
        ci = lax.broadcasted_iota(jnp.int32, (C, C), 1).astype(F32)
        k_scale = RET_DK ** -0.5
        for d in range(2):
            lg = -jnp.exp(dec_ref[d])
            if d == 1:
                diff, q_pow, k_pow = ci - ri, C - ri, ri
            else:
                diff, q_pow, k_pow = ri - ci, ri + 1.0, C - 1.0 - ri
            intra_ref[d] = jnp.where(diff >= 0, jnp.exp(lg * jnp.maximum(diff, 0.0)), 0.0) * k_scale
            qdec_ref[d] = jnp.exp(lg * q_pow)
            kdec_ref[d] = jnp.exp(lg * k_pow) * k_scale
            cdec_ref[d] = jnp.exp(lg[0:8, :] * float(C))

    state_is_zero = not has_s0 and nc == 1
    for d in range(2):
        if has_s0:
            s_refs[d][...] = s0_refs[d][...]
        elif not state_is_zero:
            s_refs[d][...] = jnp.zeros_like(s_refs[d])

    def lanes(a, width):
        return jnp.concatenate([a] * (width // C), axis=1)

    def contribution(rows, d, att):
        g_ref = (gf_ref, gb_ref)[d]
        s_ref = s_refs[d]
        q = q_ref[rows, :]
        k = k_ref[rows, :]
        v = v_ref[rows, :]
        att = (att * intra_ref[d]).astype(BF16)
        o = jnp.dot(att, v, preferred_element_type=F32)
        kd = (k.astype(F32) * lanes(kdec_ref[d], RET_DK)).astype(BF16)
        s_new = lax.dot_general(kd, v, (((0,), (0,)), ((), ())), preferred_element_type=F32)
        if state_is_zero:
            s_ref[...] = s_new
        else:
            qd = (q.astype(F32) * lanes(qdec_ref[d], RET_DK)).astype(BF16)
            s_old = s_ref[...]
            o = o + jnp.dot(qd, s_old.astype(BF16), preferred_element_type=F32)
            s_ref[...] = s_old * lanes(cdec_ref[d][0:1, :], RET_DV) + s_new
        mu = jnp.mean(o, axis=-1, keepdims=True)
        dlt = o - mu
        var = jnp.mean(dlt * dlt, axis=-1, keepdims=True)
        yn = (dlt * lax.rsqrt(var + EPS)) * gn_ref[d:d + 1, :]
        return g_ref[rows, :].astype(F32) * yn

    def scores(rows):
        return lax.dot_general(q_ref[rows, :], k_ref[rows, :], (((1,), (1,)), ((), ())),
                               preferred_element_type=F32)

    if nc == 1:
        rows = pl.ds(0, C)
        att = scores(rows)
        y_ref[...] = (contribution(rows, 0, att) + contribution(rows, 1, att)).astype(y_ref.dtype)
    else:
        def pair(i, second_visit):
            for d, c in ((0, i), (1, nc - 1 - i)):
                rows = pl.ds(pl.multiple_of(c * C, C), C)
                contrib = contribution(rows, d, scores(rows))
                if second_visit:
                    y_ref[rows, :] = (yacc_ref[rows, :] + contrib).astype(y_ref.dtype)
                else:
                    yacc_ref[rows, :] = contrib

        def first(i, carry):
            pair(i, False)
            return carry

        def second(i, carry):
            pair(i, True)
            return carry

        lax.fori_loop(0, nc // 2, first, 0)
        lax.fori_loop(nc // 2, nc, second, 0)


def _retention_scan(proj, dec, gn, batch, seq, s0f=None, s0b=None, s0_slot=0, state_slot=None,
                    state_bufs=None, n_slots=1):
    has_s0 = s0f is not None
    want_state = state_slot is not None
    aliased = state_bufs is not None
    hv = RET_HEADS * RET_DV
    nkb = RET_HEADS
    nvb = 2 * RET_HEADS * RET_DK // RET_DV
    C = SCAN_CHUNK
    assert seq % C == 0 and (seq == C or (seq // C) % 2 == 0)
    in_specs = [
        pl.BlockSpec((seq, RET_DK), lambda h, b: (b, h)),
        pl.BlockSpec((seq, RET_DK), lambda h, b: (b, nkb + h)),
        pl.BlockSpec((seq, RET_DV), lambda h, b: (b, nvb + h)),
        pl.BlockSpec((seq, RET_DV), lambda h, b: (b, nvb + RET_HEADS + h)),
        pl.BlockSpec((seq, RET_DV), lambda h, b: (b, nvb + 2 * RET_HEADS + h)),
        pl.BlockSpec((2, None, C, C), lambda h, b: (0, h, 0, 0)),
        pl.BlockSpec((2, RET_DV), lambda h, b: (0, h)),
    ]
    args = [proj, proj, proj, proj, proj, dec, gn]
    if has_s0:
        st_spec = pl.BlockSpec((None, None, None, RET_DK, RET_DV), lambda h, b: (b, s0_slot, h, 0, 0))
        in_specs += [st_spec, st_spec]
        args += [s0f, s0b]
    aliases = {}
    if aliased:
        in_specs += [pl.BlockSpec(memory_space=pl.ANY)] * 2
        aliases = {len(args): 1, len(args) + 1: 2}
        args += list(state_bufs)
    out_shape = [jax.ShapeDtypeStruct((batch * seq, hv), BF16)]
    out_specs = [pl.BlockSpec((seq, RET_DV), lambda h, b: (b, h))]
    scratch = [pltpu.VMEM((2, C, C), F32), pltpu.VMEM((2, C, C), F32), pltpu.VMEM((2, C, C), F32),
               pltpu.VMEM((2, 8, C), F32), pltpu.VMEM((seq, RET_DV), F32)]
    if want_state:
        st = jax.ShapeDtypeStruct((batch, n_slots, RET_HEADS, RET_DK, RET_DV), F32)
        out_shape += [st, st]
        st_spec = pl.BlockSpec((None, None, None, RET_DK, RET_DV), lambda h, b: (b, state_slot, h, 0, 0))
        out_specs += [st_spec, st_spec]
    else:
        scratch = [pltpu.VMEM((RET_DK, RET_DV), F32)] * 2 + scratch
    return pl.pallas_call(
        functools.partial(_scan_kernel, seq=seq, has_s0=has_s0, aliased=aliased),
        out_shape=out_shape,
        grid=(RET_HEADS, batch),
        in_specs=in_specs,
        out_specs=out_specs,
        scratch_shapes=scratch,
        input_output_aliases=aliases,
        compiler_params=_params(("arbitrary", "arbitrary")),
        name="ret_scan",
    )(*args)


def _route(logits, bias):
    s = _sigmoid(logits)
    sb = s + bias
    lane = lax.broadcasted_iota(jnp.int32, sb.shape, 1)
    lane_f = lane.astype(F32)
    neg = -jnp.inf
    big = float(LANES)

    def top2(vals):
        m1 = jnp.max(vals, axis=-1, keepdims=True)
        i1 = jnp.min(jnp.where(vals == m1, lane_f, big), axis=-1, keepdims=True)
        vals2 = jnp.where(lane_f == i1, neg, vals)
        m2 = jnp.max(vals2, axis=-1, keepdims=True)
        i2 = jnp.min(jnp.where(vals2 == m2, lane_f, big), axis=-1, keepdims=True)
        return m1 + m2, i1, i2

    best = None
    for g in range(N_GROUPS):
        in_group = (lane // EXPERTS_PER_GROUP) == g
        gs, i1, i2 = top2(jnp.where(in_group, sb, neg))
        if best is None:
            best = (gs, i1, i2)
        else:
            take = gs > best[0]
            best = (jnp.where(take, gs, best[0]), jnp.where(take, i1, best[1]),
                    jnp.where(take, i2, best[2]))
    _, e1, e2 = best
    oh1 = lane_f == e1
    oh2 = lane_f == e2
    w1 = jnp.sum(jnp.where(oh1, s, 0.0), axis=-1, keepdims=True)
    w2 = jnp.sum(jnp.where(oh2, s, 0.0), axis=-1, keepdims=True)
    den = w1 + w2
    return jnp.where(oh1, w1 / den, 0.0) + jnp.where(oh2, w2 / den, 0.0)


def _split_bf16(a):
    hi = a.astype(BF16)
    lo = (a - hi.astype(F32)).astype(BF16)
    return hi, lo


def _mix_out_kernel(y_ref, w_ref, x_ref, g1_ref, n2_ref, sc_ref, sh_ref, wr_ref, rb_ref,
                    xo_ref, h_ref, gate_ref):
    w_hi, w_lo = _split_bf16(wr_ref[...])
    w_cat = jnp.concatenate([w_hi, w_lo], axis=1)
    for r in range(y_ref.shape[0] // MIX_SUB):
        rows = slice(r * MIX_SUB, (r + 1) * MIX_SUB)
        x = x_ref[rows, :] + g1_ref[...] * jnp.dot(y_ref[rows, :], w_ref[...], preferred_element_type=F32)
        xo_ref[rows, :] = x
        h = _normmod(x, n2_ref[...], sc_ref[...], sh_ref[...])
        h_ref[rows, :] = h.astype(BF16)
        h_hi, h_lo = _split_bf16(h)
        both = jnp.dot(h_hi, w_cat, preferred_element_type=F32)
        logits = both[:, :LANES] + both[:, LANES:] + jnp.dot(h_lo, w_hi, preferred_element_type=F32)
        gate_ref[rows, :] = _route(logits, rb_ref[...])


def _mix_out(y, w, x, mod, norm2_g, wr_pad, rb_pad, rows_per_cond):
    n, kdim = y.shape
    tm = 512
    full = lambda i: (0, 0)
    return pl.pallas_call(
        _mix_out_kernel,
        out_shape=[jax.ShapeDtypeStruct((n, D_MODEL), F32),
                   jax.ShapeDtypeStruct((n, D_MODEL), BF16),
                   jax.ShapeDtypeStruct((n, LANES), F32)],
        grid=(n // tm,),
        in_specs=[
            pl.BlockSpec((tm, kdim), lambda i: (i, 0)),
            pl.BlockSpec((kdim, D_MODEL), full),
            pl.BlockSpec((tm, D_MODEL), lambda i: (i, 0)),
            _mod_spec(2, tm, rows_per_cond, 1),
            pl.BlockSpec((1, D_MODEL), full),
            _mod_spec(4, tm, rows_per_cond, 1),
            _mod_spec(3, tm, rows_per_cond, 1),
            pl.BlockSpec((D_MODEL, LANES), full),
            pl.BlockSpec((1, LANES), full),
        ],
        out_specs=[pl.BlockSpec((tm, D_MODEL), lambda i: (i, 0)),
                   pl.BlockSpec((tm, D_MODEL), lambda i: (i, 0)),
                   pl.BlockSpec((tm, LANES), lambda i: (i, 0))],
        compiler_params=_params(("arbitrary",)),
        name="mix_out",
    )(y, w, x, mod, norm2_g, mod, mod, wr_pad, rb_pad)


def _moe_kernel(x_ref, h_ref, gate_ref, g2_ref, wg_ref, wu_ref, wd_ref, nf_ref, o_ref, acc_ref,
                *, final_norm):
    step = pl.program_id(1)

    @pl.when(step == 0)
    def _():
        acc_ref[...] = jnp.zeros_like(acc_ref)

    h = h_ref[...]
    gate = gate_ref[...]
    lane = lax.broadcasted_iota(jnp.int32, gate.shape, 1)
    acts = []
    for k in range(MOE_EG):
        a = jnp.dot(h, wg_ref[k], preferred_element_type=F32)
        u = jnp.dot(h, wu_ref[k], preferred_element_type=F32)
        ge = jnp.sum(jnp.where(lane == step * MOE_EG + k, gate, 0.0), axis=-1, keepdims=True)
        acts.append((_silu(a) * u * ge).astype(BF16))
    act = jnp.concatenate(acts, axis=1)
    acc_ref[...] += jnp.dot(act, wd_ref[...].reshape(MOE_EG * EXPERT_FF, D_MODEL),
                            preferred_element_type=F32)

    @pl.when(step == N_EXPERTS // MOE_EG - 1)
    def _():
        x = x_ref[...] + g2_ref[...] * acc_ref[...]
        if final_norm:
            x = _rms(x, nf_ref[...])
        o_ref[...] = x


def _moe(x, h, gate, mod, wg, wu, wd, layer, norm_f, rows_per_cond, final_norm):
    n = x.shape[0]
    tm = 1024
    return pl.pallas_call(
        functools.partial(_moe_kernel, final_norm=final_norm),
        out_shape=jax.ShapeDtypeStruct((n, D_MODEL), F32),
        grid=(n // tm, N_EXPERTS // MOE_EG),
        in_specs=[
            pl.BlockSpec((tm, D_MODEL), lambda i, e: (i, 0)),
            pl.BlockSpec((tm, D_MODEL), lambda i, e: (i, 0)),
            pl.BlockSpec((tm, LANES), lambda i, e: (i, 0)),
            _mod_spec(5, tm, rows_per_cond, 2),
            pl.BlockSpec((None, MOE_EG, D_MODEL, EXPERT_FF), lambda i, e: (layer, e, 0, 0)),
            pl.BlockSpec((None, MOE_EG, D_MODEL, EXPERT_FF), lambda i, e: (layer, e, 0, 0)),
            pl.BlockSpec((None, MOE_EG, EXPERT_FF, D_MODEL), lambda i, e: (layer, e, 0, 0)),
            pl.BlockSpec((1, D_MODEL), lambda i, e: (0, 0)),
        ],
        out_specs=pl.BlockSpec((tm, D_MODEL), lambda i, e: (i, 0)),
        scratch_shapes=[pltpu.VMEM((tm, D_MODEL), F32)],
        compiler_params=_params(("arbitrary", "arbitrary")),
        name="moe",
    )(x, h, gate, mod, wg, wu, wd, norm_f)


def _swap_pairs(x):
    lane = lax.broadcasted_iota(jnp.int32, x.shape, 1)
    first = (lane % MLA_ROPE) < (MLA_ROPE // 2)
    return jnp.where(first, pltpu.roll(x, LANES - MLA_ROPE // 2, axis=1),
                     pltpu.roll(x, MLA_ROPE // 2, axis=1))


def _mla_proj_kernel(*refs, rope):
    x_ref, g_ref, sc_ref, sh_ref, win_ref, qn_ref, kvn_ref, wuq_ref = refs[:8]
    pos = 8
    if rope:
        cos_ref, sin_ref = refs[pos:pos + 2]
        pos += 2
    q_ref, ckv_ref, kpe_ref = refs[pos:pos + 3]

    h = _normmod(x_ref[...], g_ref[...], sc_ref[...], sh_ref[...]).astype(BF16)
    p = jnp.dot(h, win_ref[...], preferred_element_type=F32)
    cq = _rms(p[:, :MLA_Q_RANK], qn_ref[...]).astype(BF16)
    ckv_ref[...] = _rms(p[:, MLA_Q_RANK:MLA_Q_RANK + MLA_KV_RANK], kvn_ref[...])
    kpe = p[:, MLA_Q_RANK + MLA_KV_RANK:]
    q = jnp.dot(cq, wuq_ref[...], preferred_element_type=F32) * (
        (MLA_NOPE + MLA_ROPE) ** -0.5 * float(np.log2(np.e)))
    if rope:
        cos = cos_ref[...]
        sin = sin_ref[...]
        kpe = kpe * cos + _swap_pairs(kpe) * sin
    kpe_ref[...] = kpe
    for hd in range(MLA_HEADS):
        lo = hd * HEAD_W
        q_ref[:, lo:lo + LANES] = q[:, lo:lo + LANES].astype(BF16)
        qp = q[:, lo + LANES:lo + HEAD_W]
        if rope:
            qp = qp * cos + _swap_pairs(qp) * sin
        q_ref[:, lo + LANES:lo + HEAD_W] = qp.astype(BF16)


def _mla_proj(x, mod, norm_g, w_in, q_norm, kv_norm, w_uq, rows_per_cond, cos=None, sin=None):
    n = x.shape[0]
    tm = 512
    rope = cos is not None
    full = lambda i: (0, 0)
    nin = w_in.shape[1]
    in_specs = [
        pl.BlockSpec((tm, D_MODEL), lambda i: (i, 0)),
        pl.BlockSpec((1, D_MODEL), full),
        _mod_spec(1, tm, rows_per_cond, 1),
        _mod_spec(0, tm, rows_per_cond, 1),
        pl.BlockSpec((D_MODEL, nin), full),
        pl.BlockSpec((1, MLA_Q_RANK), full),
        pl.BlockSpec((1, MLA_KV_RANK), full),
        pl.BlockSpec((MLA_Q_RANK, MLA_HEADS * HEAD_W), full),
    ]
    args = [x, norm_g, mod, mod, w_in, q_norm, kv_norm, w_uq]
    if rope:
        nt = cos.shape[0] // tm
        tab = pl.BlockSpec((tm, LANES), lambda i: (i % nt, 0))
        in_specs += [tab, tab]
        args += [cos, sin]
    return pl.pallas_call(
        functools.partial(_mla_proj_kernel, rope=rope),
        out_shape=[jax.ShapeDtypeStruct((n, MLA_HEADS * HEAD_W), BF16),
                   jax.ShapeDtypeStruct((n, MLA_KV_RANK), F32),
                   jax.ShapeDtypeStruct((n, LANES), F32)],
        grid=(n // tm,),
        in_specs=in_specs,
        out_specs=[pl.BlockSpec((tm, MLA_HEADS * HEAD_W), lambda i: (i, 0)),
                   pl.BlockSpec((tm, MLA_KV_RANK), lambda i: (i, 0)),
                   pl.BlockSpec((tm, LANES), lambda i: (i, 0))],
        compiler_params=_params(("arbitrary",)),
        name="mla_proj",
    )(*args)


def _kv_up_kernel(ckv_ref, kpe_ref, wuk_ref, wuv_ref, k_ref, v_ref):
    c = ckv_ref[...].astype(BF16)
    kn = jnp.dot(c, wuk_ref[...], preferred_element_type=F32).astype(BF16)
    vv = jnp.dot(c, wuv_ref[...], preferred_element_type=F32).astype(BF16)
    kpe = kpe_ref[...].astype(BF16)
    lane = lax.broadcasted_iota(jnp.int32, kpe.shape, 1)
    ones_col = jnp.where(lane == 0, 1.0, 0.0).astype(BF16)
    for hd in range(MLA_HEADS):
        k_ref[:, hd * HEAD_W:hd * HEAD_W + LANES] = kn[:, hd * MLA_NOPE:(hd + 1) * MLA_NOPE]
        k_ref[:, hd * HEAD_W + LANES:(hd + 1) * HEAD_W] = kpe
        v_ref[:, hd * HEAD_W:hd * HEAD_W + LANES] = vv[:, hd * MLA_V:(hd + 1) * MLA_V]
        v_ref[:, hd * HEAD_W + LANES:(hd + 1) * HEAD_W] = ones_col


def _kv_up(ckv, kpe, w_uk, w_uv):
    n = ckv.shape[0]
    tk = 512
    full = lambda i: (0, 0)
    return pl.pallas_call(
        _kv_up_kernel,
        out_shape=[jax.ShapeDtypeStruct((n, MLA_HEADS * HEAD_W), BF16),
                   jax.ShapeDtypeStruct((n, MLA_HEADS * HEAD_W), BF16)],
        grid=(n // tk,),
        in_specs=[
            pl.BlockSpec((tk, MLA_KV_RANK), lambda i: (i, 0)),
            pl.BlockSpec((tk, LANES), lambda i: (i, 0)),
            pl.BlockSpec((MLA_KV_RANK, MLA_HEADS * MLA_NOPE), full),
            pl.BlockSpec((MLA_KV_RANK, MLA_HEADS * MLA_V), full),
        ],
        out_specs=[pl.BlockSpec((tk, MLA_HEADS * HEAD_W), lambda i: (i, 0)),
                   pl.BlockSpec((tk, MLA_HEADS * HEAD_W), lambda i: (i, 0))],
        compiler_params=_params(("arbitrary",)),
        name="kv_up",
    )(ckv, kpe, w_uk, w_uv)


def _attn_kernel(q_ref, k_ref, v_ref, o_ref, *, heads):
    for hd in range(heads):
        k = k_ref[:, hd * HEAD_W:(hd + 1) * HEAD_W]
        v = v_ref[:, hd * HEAD_W:(hd + 1) * HEAD_W]
        for r in range(q_ref.shape[0] // ATTN_SUB):
            rows = slice(r * ATTN_SUB, (r + 1) * ATTN_SUB)
            q = q_ref[rows, hd * HEAD_W:(hd + 1) * HEAD_W]
            s = lax.dot_general(q, k, (((1,), (1,)), ((), ())), preferred_element_type=F32)
            m = jnp.max(s, axis=-1, keepdims=True)
            p = jnp.exp2(s - m).astype(BF16)
            o = jnp.dot(p, v, preferred_element_type=F32)
            o_ref[rows, hd * MLA_V:(hd + 1) * MLA_V] = (
                o[:, :MLA_V] / o[:, MLA_V:MLA_V + 1]).astype(o_ref.dtype)


def _attention(q, k, v, batch, tq_total, tk_total, heads_per_step, tq):
    nq = tq_total // tq
    nh = MLA_HEADS // heads_per_step
    hp = heads_per_step
    return pl.pallas_call(
        functools.partial(_attn_kernel, heads=hp),
        out_shape=jax.ShapeDtypeStruct((batch * tq_total, MLA_HEADS * MLA_V), BF16),
        grid=(batch, nh, nq),
        in_specs=[
            pl.BlockSpec((tq, hp * HEAD_W), lambda b, h, i: (b * nq + i, h)),
            pl.BlockSpec((tk_total, hp * HEAD_W), lambda b, h, i: (b, h)),
            pl.BlockSpec((tk_total, hp * HEAD_W), lambda b, h, i: (b, h)),
        ],
        out_specs=pl.BlockSpec((tq, hp * MLA_V), lambda b, h, i: (b * nq + i, h)),
        compiler_params=_params(("arbitrary", "arbitrary", "arbitrary")),
        name="attn",
    )(q, k, v)


_ROPE_PERM = np.concatenate([np.arange(0, 16), np.arange(32, 48), np.arange(16, 32), np.arange(48, 64)])


def _rope_tables(seq):
    pos = jnp.arange(seq)
    row = (pos // GRID_W).astype(F32)
    col = (pos % GRID_W).astype(F32)
    half = MLA_ROPE // 2
    inv = ROPE_BASE ** (-jnp.arange(0, half, 2, dtype=F32) / half)
    ang = jnp.concatenate([row[:, None] * inv, col[:, None] * inv], axis=1)
    cos = jnp.cos(ang)
    sin = jnp.sin(ang)
    pad = jnp.zeros((seq, LANES - MLA_ROPE), F32)
    return (jnp.concatenate([cos, cos, pad], axis=1),
            jnp.concatenate([-sin, sin, pad], axis=1))


def _prep_mla_weights(w_in, w_uq, w_uk, w_uv):
    nq = MLA_Q_RANK + MLA_KV_RANK
    w_in_p = jnp.concatenate(
        [w_in[:, :nq], w_in[:, nq:][:, _ROPE_PERM], jnp.zeros((D_MODEL, LANES - MLA_ROPE), F32)], axis=1)
    wq = w_uq.reshape(MLA_Q_RANK, MLA_HEADS, MLA_NOPE + MLA_ROPE)
    wq_p = jnp.concatenate(
        [wq[:, :, :MLA_NOPE], wq[:, :, MLA_NOPE:][:, :, _ROPE_PERM],
         jnp.zeros((MLA_Q_RANK, MLA_HEADS, HEAD_W - MLA_NOPE - MLA_ROPE), F32)], axis=2)
    return (w_in_p.astype(BF16), wq_p.reshape(MLA_Q_RANK, MLA_HEADS * HEAD_W).astype(BF16),
            w_uk.reshape(MLA_KV_RANK, MLA_HEADS * MLA_NOPE).astype(BF16),
            w_uv.reshape(MLA_KV_RANK, MLA_HEADS * MLA_V).astype(BF16))


def kernel(x_prompt, x_sample, state_ret_fwd, state_ret_bwd, cache_mla_ckv, cache_mla_kpe, c, c_ctx,
           w_ada, b_ada, norm1, norm2, norm_final, ret_w_in, ret_decay, ret_gn, ret_w_out,
           mla_w_in, mla_q_norm, mla_kv_norm, mla_w_uq, mla_w_uk, mla_w_uv, mla_w_o,
           w_router, router_bias, moe_w_gate, moe_w_up, moe_w_down):
    bp, tp, _ = x_prompt.shape
    bs, ts, _ = x_sample.shape
    past = cache_mla_ckv.shape[2]
    xp = x_prompt.reshape(bp * tp, D_MODEL)
    xs = x_sample.reshape(bs * ts, D_MODEL)

    conds = jnp.concatenate([c_ctx[None, :], c, jnp.zeros((8 - 1 - bs, D_MODEL), F32)], axis=0)
    mods = _ada_all(conds, w_ada, b_ada)

    wr_pad = jnp.pad(w_router, ((0, 0), (0, LANES - N_EXPERTS)))
    rb_pad = jnp.pad(router_bias, (0, LANES - N_EXPERTS)).reshape(1, LANES)
    nf = norm_final.reshape(1, D_MODEL)
    cos_t, sin_t = _rope_tables(ts)

    groups = {
        "p": dict(x=xp, rows_per_cond=bp * tp, lo=0, hi=1),
        "s": dict(x=xs, rows_per_cond=ts, lo=1, hi=1 + bs),
    }
    out_ckv, out_kpe = [], []
    ret_states = None
    n_ret = state_ret_fwd.shape[1]
    wg = moe_w_gate.astype(BF16)
    wu = moe_w_up.astype(BF16)
    wd = moe_w_down.astype(BF16)

    for i in range(DEPTH):
        j = i // 2
        n1 = norm1[i].reshape(1, D_MODEL)
        n2 = norm2[i].reshape(1, D_MODEL)
        mixed = {}
        if i % 2 == 0:
            w_in = ret_w_in[j].astype(BF16)
            w_out = ret_w_out[j].astype(BF16)
            dec = jnp.broadcast_to(ret_decay[j][:, :, None, None], (2, RET_HEADS, SCAN_CHUNK, SCAN_CHUNK))
            gn = ret_gn[j]
            for name, gr in groups.items():
                mod = mods[i, gr["lo"]:gr["hi"]].reshape(-1, 1, 6 * D_MODEL)
                proj = _proj(gr["x"], mod, n1, w_in, gr["rows_per_cond"],
                             2 * RET_HEADS * RET_DK + RET_HEADS * RET_DV)
                if name == "p":
                    y, *ret_states = _retention_scan(proj, dec, gn, bp, tp, state_slot=j,
                                                     state_bufs=ret_states, n_slots=n_ret)
                else:
                    (y,) = _retention_scan(proj, dec, gn, bs, ts, state_ret_fwd, state_ret_bwd, s0_slot=j)
                mixed[name] = (y, w_out, mod)
        else:
            w_in_p, w_uq_p, w_uk, w_uv = _prep_mla_weights(mla_w_in[j], mla_w_uq[j], mla_w_uk[j], mla_w_uv[j])
            w_o = mla_w_o[j].astype(BF16)
            qn = mla_q_norm[j].reshape(1, MLA_Q_RANK)
            kvn = mla_kv_norm[j].reshape(1, MLA_KV_RANK)
            for name, gr in groups.items():
                mod = mods[i, gr["lo"]:gr["hi"]].reshape(-1, 1, 6 * D_MODEL)
                if name == "p":
                    q, ckv, kpe = _mla_proj(gr["x"], mod, n1, w_in_p, qn, kvn, w_uq_p, gr["rows_per_cond"])
                    out_ckv.append(ckv.reshape(bp, tp, MLA_KV_RANK))
                    out_kpe.append(kpe[:, :MLA_ROPE][:, _ROPE_PERM].reshape(bp, tp, MLA_ROPE))
                    k, v = _kv_up(ckv, kpe, w_uk, w_uv)
                    o = _attention(q, k, v, bp, tp, tp, MLA_HEADS, tp)
                else:
                    q, ckv, kpe = _mla_proj(gr["x"], mod, n1, w_in_p, qn, kvn, w_uq_p, gr["rows_per_cond"],
                                            cos_t, sin_t)
                    cache_kpe = jnp.pad(cache_mla_kpe[:, j][:, :, _ROPE_PERM],
                                        ((0, 0), (0, 0), (0, LANES - MLA_ROPE)))
                    ckv_all = jnp.concatenate([ckv.reshape(bs, ts, MLA_KV_RANK), cache_mla_ckv[:, j]], axis=1)
                    kpe_all = jnp.concatenate([kpe.reshape(bs, ts, LANES), cache_kpe], axis=1)
                    tk = ts + past
                    k, v = _kv_up(ckv_all.reshape(bs * tk, MLA_KV_RANK), kpe_all.reshape(bs * tk, LANES),
                                  w_uk, w_uv)
                    o = _attention(q, k, v, bs, ts, tk, 1, 2048)
                mixed[name] = (o, w_o, mod)

        for name, gr in groups.items():
            y, w_o, mod = mixed[name]
            x_new, h2, gate = _mix_out(y, w_o, gr["x"], mod, n2, wr_pad, rb_pad, gr["rows_per_cond"])
            gr["x"] = _moe(x_new, h2, gate, mod, wg, wu, wd, i, nf, gr["rows_per_cond"], i == DEPTH - 1)

    return (groups["p"]["x"].reshape(bp, tp, D_MODEL), groups["s"]["x"].reshape(bs, ts, D_MODEL),
            ret_states[0], ret_states[1],
            jnp.stack(out_ckv, axis=1), jnp.stack(out_kpe, axis=1))
```

```python
import functools

import numpy as np
import jax
import jax.numpy as jnp
from jax import lax
from jax.experimental import pallas as pl
from jax.experimental.pallas import tpu as pltpu

F32 = jnp.float32
BF16 = jnp.bfloat16

D_MODEL = 1024
DEPTH = 4
GRID_W = 64
RET_HEADS = 4
RET_DK = 256
RET_DV = 512
SCAN_CHUNK = 256
MLA_HEADS = 8
MLA_Q_RANK = 512
MLA_KV_RANK = 256
MLA_NOPE = 128
MLA_ROPE = 64
MLA_V = 128
ROPE_BASE = 10000.0
N_EXPERTS = 16
N_GROUPS = 4
EXPERTS_PER_GROUP = N_EXPERTS // N_GROUPS
EXPERT_FF = 256
EPS = 1e-6

LANES = 128
HEAD_W = 2 * LANES
VMEM_LIMIT = 48 * 1024 * 1024
MIX_SUB = 256
ATTN_SUB = 256
MOE_EG = 4
PAIRS_PER_GROUP = EXPERTS_PER_GROUP * (EXPERTS_PER_GROUP - 1) // 2
N_BUCKETS = N_GROUPS * PAIRS_PER_GROUP
HX_W = D_MODEL + LANES
MOE_TILE = 256
PLAN_BLK = 256
PUSH_ROWS = 512
PULL_ROWS = 512

_PAIRS = [(a, b) for a in range(EXPERTS_PER_GROUP) for b in range(a + 1, EXPERTS_PER_GROUP)]
_BUCKET_LO = np.array([g * EXPERTS_PER_GROUP + a for g in range(N_GROUPS) for a, _ in _PAIRS], np.int32)
_BUCKET_HI = np.array([g * EXPERTS_PER_GROUP + b for g in range(N_GROUPS) for _, b in _PAIRS], np.int32)


def _params(sem):
    return pltpu.CompilerParams(dimension_semantics=sem, vmem_limit_bytes=VMEM_LIMIT)


def _sigmoid(x):
    return 1.0 / (1.0 + jnp.exp(-x))


def _normmod(x, g, sc, sh):
    y = x * lax.rsqrt(jnp.mean(x * x, axis=-1, keepdims=True) + EPS)
    return (y * g) * (1.0 + sc) + sh


def _rms(x, g):
    return x * lax.rsqrt(jnp.mean(x * x, axis=-1, keepdims=True) + EPS) * g


def _ada_kernel(c_ref, w_ref, b_ref, o_ref):
    c = c_ref[...]
    s = (c * _sigmoid(c)).astype(BF16)
    o_ref[0] = jnp.dot(s, w_ref[0].astype(BF16), preferred_element_type=F32) + b_ref[0]


def _ada_all(conds, w_ada, b_ada):
    tn = 1536
    n6 = 6 * D_MODEL
    return pl.pallas_call(
        _ada_kernel,
        out_shape=jax.ShapeDtypeStruct((DEPTH, 8, n6), F32),
        grid=(DEPTH, n6 // tn),
        in_specs=[
            pl.BlockSpec((8, D_MODEL), lambda l, j: (0, 0)),
            pl.BlockSpec((1, D_MODEL, tn), lambda l, j: (l, 0, j)),
            pl.BlockSpec((1, 1, tn), lambda l, j: (l, 0, j)),
        ],
        out_specs=pl.BlockSpec((1, 8, tn), lambda l, j: (l, 0, j)),
        compiler_params=_params(("arbitrary", "arbitrary")),
        name="ada",
    )(conds, w_ada, b_ada.reshape(DEPTH, 1, n6))


def _mod_spec(which, tm, rows_per_cond, ngrid):
    if ngrid == 1:
        return pl.BlockSpec((None, 1, D_MODEL), lambda i: (i * tm // rows_per_cond, 0, which))
    return pl.BlockSpec((None, 1, D_MODEL), lambda i, j: (i * tm // rows_per_cond, 0, which))


def _silu(g):
    hg = 0.5 * g
    return hg + hg * jnp.tanh(hg)


def _proj_kernel(x_ref, g_ref, sc_ref, sh_ref, w_ref, o_ref, h_ref, *, silu_from):
    j = pl.program_id(1)

    @pl.when(j == 0)
    def _():
        h_ref[...] = _normmod(x_ref[...], g_ref[...], sc_ref[...], sh_ref[...]).astype(BF16)

    acc = jnp.dot(h_ref[...], w_ref[...], preferred_element_type=F32)
    o_ref[...] = jnp.where(j >= silu_from, _silu(acc), acc).astype(o_ref.dtype)


def _proj(x, mod, norm_g, w, rows_per_cond, silu_from_col):
    n = x.shape[0]
    nout = w.shape[1]
    tm, tn = 1024, 2048
    return pl.pallas_call(
        functools.partial(_proj_kernel, silu_from=silu_from_col // tn),
        out_shape=jax.ShapeDtypeStruct((n, nout), BF16),
        grid=(n // tm, nout // tn),
        in_specs=[
            pl.BlockSpec((tm, D_MODEL), lambda i, j: (i, 0)),
            pl.BlockSpec((1, D_MODEL), lambda i, j: (0, 0)),
            _mod_spec(1, tm, rows_per_cond, 2),
            _mod_spec(0, tm, rows_per_cond, 2),
            pl.BlockSpec((D_MODEL, tn), lambda i, j: (0, j)),
        ],
        out_specs=pl.BlockSpec((tm, tn), lambda i, j: (i, j)),
        scratch_shapes=[pltpu.VMEM((tm, D_MODEL), BF16)],
        compiler_params=_params(("arbitrary", "arbitrary")),
        name="ret_proj",
    )(x, norm_g, mod, mod, w)


def _scan_kernel(*refs, seq, has_s0, aliased):
    q_ref, k_ref, v_ref, gf_ref, gb_ref, dec_ref, gn_ref = refs[:7]
    pos = 7
    if has_s0:
        s0_refs = refs[pos:pos + 2]
        pos += 2
    if aliased:
        pos += 2
    y_ref = refs[pos]
    s_refs = refs[pos + 1:pos + 3]
    intra_ref, qdec_ref, kdec_ref, cdec_ref, yacc_ref = refs[pos + 3:pos + 8]

    C = SCAN_CHUNK
    nc = seq // C

    @pl.when(pl.program_id(1) == 0)
    def _():
        ri = lax.broadcasted_iota(jnp.int32, (C, C), 0).astype(F32)
        ci = lax.broadcasted_iota(jnp.int32, (C, C), 1).astype(F32)
        k_scale = RET_DK ** -0.5
        for d in range(2):
            lg = -jnp.exp(dec_ref[d])
            if d == 1:
                diff, q_pow, k_pow = ci - ri, C - ri, ri
            else:
                diff, q_pow, k_pow = ri - ci, ri + 1.0, C - 1.0 - ri
            intra_ref[d] = jnp.where(diff >= 0, jnp.exp(lg * jnp.maximum(diff, 0.0)), 0.0) * k_scale
            qdec_ref[d] = jnp.exp(lg * q_pow)
            kdec_ref[d] = jnp.exp(lg * k_pow) * k_scale
            cdec_ref[d] = jnp.exp(lg[0:8, :] * float(C))

    state_is_zero = not has_s0 and nc == 1
    for d in range(2):
        if has_s0:
            s_refs[d][...] = s0_refs[d][...]
        elif not state_is_zero:
            s_refs[d][...] = jnp.zeros_like(s_refs[d])

    def lanes(a, width):
        return jnp.concatenate([a] * (width // C), axis=1)

    def contribution(rows, d, att):
        g_ref = (gf_ref, gb_ref)[d]
        s_ref = s_refs[d]
        q = q_ref[rows, :]
        k = k_ref[rows, :]
        v = v_ref[rows, :]
        att = (att * intra_ref[d]).astype(BF16)
        o = jnp.dot(att, v, preferred_element_type=F32)
        kd = (k.astype(F32) * lanes(kdec_ref[d], RET_DK)).astype(BF16)
        s_new = lax.dot_general(kd, v, (((0,), (0,)), ((), ())), preferred_element_type=F32)
        if state_is_zero:
            s_ref[...] = s_new
        else:
            qd = (q.astype(F32) * lanes(qdec_ref[d], RET_DK)).astype(BF16)
            s_old = s_ref[...]
            o = o + jnp.dot(qd, s_old.astype(BF16), preferred_element_type=F32)
            s_ref[...] = s_old * lanes(cdec_ref[d][0:1, :], RET_DV) + s_new
        mu = jnp.mean(o, axis=-1, keepdims=True)
        dlt = o - mu
        var = jnp.mean(dlt * dlt, axis=-1, keepdims=True)
        yn = (dlt * lax.rsqrt(var + EPS)) * gn_ref[d:d + 1, :]
        return g_ref[rows, :].astype(F32) * yn

    def scores(rows):
        return lax.dot_general(q_ref[rows, :], k_ref[rows, :], (((1,), (1,)), ((), ())),
                               preferred_element_type=F32)

    if nc == 1:
        rows = pl.ds(0, C)
        att = scores(rows)
        y_ref[...] = (contribution(rows, 0, att) + contribution(rows, 1, att)).astype(y_ref.dtype)
    else:
        def pair(i, second_visit):
            for d, c in ((0, i), (1, nc - 1 - i)):
                rows = pl.ds(pl.multiple_of(c * C, C), C)
                contrib = contribution(rows, d, scores(rows))
                if second_visit:
                    y_ref[rows, :] = (yacc_ref[rows, :] + contrib).astype(y_ref.dtype)
                else:
                    yacc_ref[rows, :] = contrib

        def first(i, carry):
            pair(i, False)
            return carry

        def second(i, carry):
            pair(i, True)
            return carry

        lax.fori_loop(0, nc // 2, first, 0)
        lax.fori_loop(nc // 2, nc, second, 0)


def _retention_scan(proj, dec, gn, batch, seq, s0f=None, s0b=None, s0_slot=0, state_slot=None,
                    state_bufs=None, n_slots=1):
    has_s0 = s0f is not None
    want_state = state_slot is not None
    aliased = state_bufs is not None
    hv = RET_HEADS * RET_DV
    nkb = RET_HEADS
    nvb = 2 * RET_HEADS * RET_DK // RET_DV
    C = SCAN_CHUNK
    assert seq % C == 0 and (seq == C or (seq // C) % 2 == 0)
    in_specs = [
        pl.BlockSpec((seq, RET_DK), lambda h, b: (b, h)),
        pl.BlockSpec((seq, RET_DK), lambda h, b: (b, nkb + h)),
        pl.BlockSpec((seq, RET_DV), lambda h, b: (b, nvb + h)),
        pl.BlockSpec((seq, RET_DV), lambda h, b: (b, nvb + RET_HEADS + h)),
        pl.BlockSpec((seq, RET_DV), lambda h, b: (b, nvb + 2 * RET_HEADS + h)),
        pl.BlockSpec((2, None, C, C), lambda h, b: (0, h, 0, 0)),
        pl.BlockSpec((2, RET_DV), lambda h, b: (0, h)),
    ]
    args = [proj, proj, proj, proj, proj, dec, gn]
    if has_s0:
        st_spec = pl.BlockSpec((None, None, None, RET_DK, RET_DV), lambda h, b: (b, s0_slot, h, 0, 0))
        in_specs += [st_spec, st_spec]
        args += [s0f, s0b]
    aliases = {}
    if aliased:
        in_specs += [pl.BlockSpec(memory_space=pl.ANY)] * 2
        aliases = {len(args): 1, len(args) + 1: 2}
        args += list(state_bufs)
    out_shape = [jax.ShapeDtypeStruct((batch * seq, hv), BF16)]
    out_specs = [pl.BlockSpec((seq, RET_DV), lambda h, b: (b, h))]
    scratch = [pltpu.VMEM((2, C, C), F32), pltpu.VMEM((2, C, C), F32), pltpu.VMEM((2, C, C), F32),
               pltpu.VMEM((2, 8, C), F32), pltpu.VMEM((seq, RET_DV), F32)]
    if want_state:
        st = jax.ShapeDtypeStruct((batch, n_slots, RET_HEADS, RET_DK, RET_DV), F32)
        out_shape += [st, st]
        st_spec = pl.BlockSpec((None, None, None, RET_DK, RET_DV), lambda h, b: (b, state_slot, h, 0, 0))
        out_specs += [st_spec, st_spec]
    else:
        scratch = [pltpu.VMEM((RET_DK, RET_DV), F32)] * 2 + scratch
    return pl.pallas_call(
        functools.partial(_scan_kernel, seq=seq, has_s0=has_s0, aliased=aliased),
        out_shape=out_shape,
        grid=(RET_HEADS, batch),
        in_specs=in_specs,
        out_specs=out_specs,
        scratch_shapes=scratch,
        input_output_aliases=aliases,
        compiler_params=_params(("arbitrary", "arbitrary")),
        name="ret_scan",
    )(*args)


def _route(logits, bias):
    s = _sigmoid(logits)
    sb = s + bias
    lane = lax.broadcasted_iota(jnp.int32, sb.shape, 1)
    lane_f = lane.astype(F32)
    neg = -jnp.inf
    big = float(LANES)

    def top2(vals):
        m1 = jnp.max(vals, axis=-1, keepdims=True)
        i1 = jnp.min(jnp.where(vals == m1, lane_f, big), axis=-1, keepdims=True)
        vals2 = jnp.where(lane_f == i1, neg, vals)
        m2 = jnp.max(vals2, axis=-1, keepdims=True)
        i2 = jnp.min(jnp.where(vals2 == m2, lane_f, big), axis=-1, keepdims=True)
        return m1 + m2, i1, i2

    best = None
    for g in range(N_GROUPS):
        in_group = (lane // EXPERTS_PER_GROUP) == g
        gs, i1, i2 = top2(jnp.where(in_group, sb, neg))
        if best is None:
            best = (gs, i1, i2)
        else:
            take = gs > best[0]
            best = (jnp.where(take, gs, best[0]), jnp.where(take, i1, best[1]),
                    jnp.where(take, i2, best[2]))
    _, e1, e2 = best
    w1 = jnp.sum(jnp.where(lane_f == e1, s, 0.0), axis=-1, keepdims=True)
    w2 = jnp.sum(jnp.where(lane_f == e2, s, 0.0), axis=-1, keepdims=True)
    den = w1 + w2
    grp = jnp.floor(e1 * (1.0 / EXPERTS_PER_GROUP))
    first_is_lo = e1 < e2
    lo = jnp.where(first_is_lo, e1, e2) - grp * EXPERTS_PER_GROUP
    hi = jnp.where(first_is_lo, e2, e1) - grp * EXPERTS_PER_GROUP
    bucket = grp * PAIRS_PER_GROUP + 3.0 * lo - 0.5 * lo * (lo - 1.0) + hi - lo - 1.0
    w_lo = jnp.where(first_is_lo, w1, w2) / den
    w_hi = jnp.where(first_is_lo, w2, w1) / den
    return (jnp.where(lane == 0, w_lo, 0.0) + jnp.where(lane == 1, w_hi, 0.0)
            + jnp.where(lane == 2, bucket, 0.0))


def _split_bf16(a):
    hi = a.astype(BF16)
    lo = (a - hi.astype(F32)).astype(BF16)
    return hi, lo


def _mix_out_kernel(y_ref, w_ref, x_ref, g1_ref, n2_ref, sc_ref, sh_ref, wr_ref, rb_ref,
                    xo_ref, hx_ref):
    w_hi, w_lo = _split_bf16(wr_ref[...])
    w_cat = jnp.concatenate([w_hi, w_lo], axis=1)
    for r in range(y_ref.shape[0] // MIX_SUB):
        rows = slice(r * MIX_SUB, (r + 1) * MIX_SUB)
        x = x_ref[rows, :] + g1_ref[...] * jnp.dot(y_ref[rows, :], w_ref[...], preferred_element_type=F32)
        xo_ref[rows, :] = x
        h = _normmod(x, n2_ref[...], sc_ref[...], sh_ref[...])
        hx_ref[rows, :D_MODEL] = h
        h_hi, h_lo = _split_bf16(h)
        both = jnp.dot(h_hi, w_cat, preferred_element_type=F32)
        logits = both[:, :LANES] + both[:, LANES:] + jnp.dot(h_lo, w_hi, preferred_element_type=F32)
        hx_ref[rows, D_MODEL:] = _route(logits, rb_ref[...])


def _mix_out(y, w, x, mod, norm2_g, wr_pad, rb_pad, rows_per_cond):
    n, kdim = y.shape
    tm = 512
    full = lambda i: (0, 0)
    return pl.pallas_call(
        _mix_out_kernel,
        out_shape=[jax.ShapeDtypeStruct((n, D_MODEL), F32),
                   jax.ShapeDtypeStruct((n, HX_W), F32)],
        grid=(n // tm,),
        in_specs=[
            pl.BlockSpec((tm, kdim), lambda i: (i, 0)),
            pl.BlockSpec((kdim, D_MODEL), full),
            pl.BlockSpec((tm, D_MODEL), lambda i: (i, 0)),
            _mod_spec(2, tm, rows_per_cond, 1),
            pl.BlockSpec((1, D_MODEL), full),
            _mod_spec(4, tm, rows_per_cond, 1),
            _mod_spec(3, tm, rows_per_cond, 1),
            pl.BlockSpec((D_MODEL, LANES), full),
            pl.BlockSpec((1, LANES), full),
        ],
        out_specs=[pl.BlockSpec((tm, D_MODEL), lambda i: (i, 0)),
                   pl.BlockSpec((tm, HX_W), lambda i: (i, 0))],
        compiler_params=_params(("arbitrary",)),
        name="mix_out",
    )(y, w, x, mod, norm2_g, mod, mod, wr_pad, rb_pad)


def _moe_kernel(x_ref, h_ref, gate_ref, g2_ref, wg_ref, wu_ref, wd_ref, nf_ref, o_ref, acc_ref,
                *, final_norm):
    step = pl.program_id(1)

    @pl.when(step == 0)
    def _():
        acc_ref[...] = jnp.zeros_like(acc_ref)

    h = h_ref[...]
    gate = gate_ref[...]
    lane = lax.broadcasted_iota(jnp.int32, gate.shape, 1)
    acts = []
    for k in range(MOE_EG):
        a = jnp.dot(h, wg_ref[k], preferred_element_type=F32)
        u = jnp.dot(h, wu_ref[k], preferred_element_type=F32)
        ge = jnp.sum(jnp.where(lane == step * MOE_EG + k, gate, 0.0), axis=-1, keepdims=True)
        acts.append((_silu(a) * u * ge).astype(BF16))
    act = jnp.concatenate(acts, axis=1)
    acc_ref[...] += jnp.dot(act, wd_ref[...].reshape(MOE_EG * EXPERT_FF, D_MODEL),
                            preferred_element_type=F32)

    @pl.when(step == N_EXPERTS // MOE_EG - 1)
    def _():
        x = x_ref[...] + g2_ref[...] * acc_ref[...]
        if final_norm:
            x = _rms(x, nf_ref[...])
        o_ref[...] = x


def _moe(x, h, gate, mod, wg, wu, wd, layer, norm_f, rows_per_cond, final_norm):
    n = x.shape[0]
    tm = 1024
    return pl.pallas_call(
        functools.partial(_moe_kernel, final_norm=final_norm),
        out_shape=jax.ShapeDtypeStruct((n, D_MODEL), F32),
        grid=(n // tm, N_EXPERTS // MOE_EG),
        in_specs=[
            pl.BlockSpec((tm, D_MODEL), lambda i, e: (i, 0)),
            pl.BlockSpec((tm, D_MODEL), lambda i, e: (i, 0)),
            pl.BlockSpec((tm, LANES), lambda i, e: (i, 0)),
            _mod_spec(5, tm, rows_per_cond, 2),
            pl.BlockSpec((None, MOE_EG, D_MODEL, EXPERT_FF), lambda i, e: (layer, e, 0, 0)),
            pl.BlockSpec((None, MOE_EG, D_MODEL, EXPERT_FF), lambda i, e: (layer, e, 0, 0)),
            pl.BlockSpec((None, MOE_EG, EXPERT_FF, D_MODEL), lambda i, e: (layer, e, 0, 0)),
            pl.BlockSpec((1, D_MODEL), lambda i, e: (0, 0)),
        ],
        out_specs=pl.BlockSpec((tm, D_MODEL), lambda i, e: (i, 0)),
        scratch_shapes=[pltpu.VMEM((tm, D_MODEL), F32)],
        compiler_params=_params(("arbitrary", "arbitrary")),
        name="moe",
    )(x, h, gate, mod, wg, wu, wd, norm_f)


def _plan_kernel(bk_ref, rank_ref, cnt_ref, carry_ref):
    i = pl.program_id(0)

    @pl.when(i == 0)
    def _():
        carry_ref[...] = jnp.zeros_like(carry_ref)

    b = bk_ref[0]
    sub = lax.broadcasted_iota(jnp.int32, (LANES, PLAN_BLK), 0)
    onehot = jnp.where(sub == b, 1.0, 0.0)
    ti = lax.broadcasted_iota(jnp.int32, (PLAN_BLK, PLAN_BLK), 0)
    tj = lax.broadcasted_iota(jnp.int32, (PLAN_BLK, PLAN_BLK), 1)
    upper = jnp.where(ti <= tj, 1.0, 0.0).astype(BF16)
    running = jnp.dot(onehot.astype(BF16), upper, preferred_element_type=F32) + carry_ref[...]
    rank_ref[0] = (jnp.sum(onehot * running, axis=0, keepdims=True) - 1.0).astype(jnp.int32)
    carry_ref[...] += jnp.sum(onehot, axis=1, keepdims=True)
    cnt_ref[...] = jnp.broadcast_to(carry_ref[...], cnt_ref.shape)


def _plan(bk):
    n = bk.shape[0]
    nb = n // PLAN_BLK
    rank, cnt = pl.pallas_call(
        _plan_kernel,
        out_shape=[jax.ShapeDtypeStruct((nb, 1, PLAN_BLK), jnp.int32),
                   jax.ShapeDtypeStruct((LANES, LANES), F32)],
        grid=(nb,),
        in_specs=[pl.BlockSpec((1, 1, PLAN_BLK), lambda i: (i, 0, 0))],
        out_specs=[pl.BlockSpec((1, 1, PLAN_BLK), lambda i: (i, 0, 0)),
                   pl.BlockSpec((LANES, LANES), lambda i: (0, 0))],
        scratch_shapes=[pltpu.VMEM((LANES, 1), F32)],
        compiler_params=_params(("arbitrary",)),
        name="moe_plan",
    )(bk.reshape(nb, 1, PLAN_BLK))
    return rank.reshape(n), cnt[:N_BUCKETS, 0].astype(jnp.int32)


def _row_copy(src_ref, src_row, dst_ref, dst_row, sem):
    return pltpu.make_async_copy(src_ref.at[pl.ds(src_row, 1), :], dst_ref.at[pl.ds(dst_row, 1), :], sem)


def _push_kernel(pos_ref, pad_lo_ref, pad_hi_ref, hx_ref, out_ref, zero_ref, sems, zsem):
    i = pl.program_id(0)
    nsteps = pl.num_programs(0)

    def issue(r, carry):
        t = i * PUSH_ROWS + r
        _row_copy(hx_ref, t, out_ref, pos_ref[t], sems.at[i % 2]).start()
        return carry

    lax.fori_loop(0, PUSH_ROWS, issue, 0, unroll=8)

    def wait_step(step):
        def body(r, carry):
            t = step * PUSH_ROWS + r
            _row_copy(hx_ref, t, out_ref, pos_ref[t], sems.at[step % 2]).wait()
            return carry
        lax.fori_loop(0, PUSH_ROWS, body, 0, unroll=8)

    @pl.when(i > 0)
    def _():
        wait_step(i - 1)

    @pl.when(i == nsteps - 1)
    def _():
        wait_step(i)
        zero_ref[...] = jnp.zeros_like(zero_ref)
        for k in range(N_BUCKETS):
            def zfill(p, carry):
                _row_copy(zero_ref, 0, out_ref, p, zsem).start()
                return carry

            def zwait(p, carry):
                _row_copy(zero_ref, 0, out_ref, p, zsem).wait()
                return carry

            lax.fori_loop(pad_lo_ref[k], pad_hi_ref[k], zfill, 0)
            lax.fori_loop(pad_lo_ref[k], pad_hi_ref[k], zwait, 0)


def _push(pos, pad_lo, pad_hi, hx, n_sorted):
    n = hx.shape[0]
    return pl.pallas_call(
        _push_kernel,
        out_shape=jax.ShapeDtypeStruct((n_sorted, HX_W), F32),
        grid_spec=pltpu.PrefetchScalarGridSpec(
            num_scalar_prefetch=3,
            grid=(n // PUSH_ROWS,),
            in_specs=[pl.BlockSpec(memory_space=pl.ANY)],
            out_specs=pl.BlockSpec(memory_space=pl.ANY),
            scratch_shapes=[pltpu.VMEM((8, HX_W), F32), pltpu.SemaphoreType.DMA((2,)),
                            pltpu.SemaphoreType.DMA(())],
        ),
        compiler_params=_params(("arbitrary",)),
        name="moe_push",
    )(pos, pad_lo, pad_hi, hx)


def _experts_kernel(blk_ref, elo_ref, ehi_ref, act_ref, s_ref, wgl_ref, wul_ref, wdl_ref,
                    wgh_ref, wuh_ref, wdh_ref, y_ref):
    @pl.when(act_ref[pl.program_id(0)] == 1)
    def _():
        hx = s_ref[...]
        h = hx[:, :D_MODEL].astype(BF16)
        y = None
        for lane, wg_ref, wu_ref, wd_ref in ((0, wgl_ref, wul_ref, wdl_ref), (1, wgh_ref, wuh_ref, wdh_ref)):
            a = jnp.dot(h, wg_ref[...], preferred_element_type=F32)
            u = jnp.dot(h, wu_ref[...], preferred_element_type=F32)
            w = hx[:, D_MODEL + lane:D_MODEL + lane + 1]
            act = (_silu(a) * u * w).astype(BF16)
            part = jnp.dot(act, wd_ref[...], preferred_element_type=F32)
            y = part if y is None else y + part
        y_ref[...] = y


def _experts(tile_blk, tile_elo, tile_ehi, tile_act, sorted_hx, wg, wu, wd, layer):
    n_sorted = sorted_hx.shape[0]
    row = lambda i, blk, elo, ehi, act: (blk[i], 0)
    w_lo = lambda i, blk, elo, ehi, act: (layer, elo[i], 0, 0)
    w_hi = lambda i, blk, elo, ehi, act: (layer, ehi[i], 0, 0)
    up_spec = lambda m: pl.BlockSpec((None, None, D_MODEL, EXPERT_FF), m)
    down_spec = lambda m: pl.BlockSpec((None, None, EXPERT_FF, D_MODEL), m)
    return pl.pallas_call(
        _experts_kernel,
        out_shape=jax.ShapeDtypeStruct((n_sorted, D_MODEL), F32),
        grid_spec=pltpu.PrefetchScalarGridSpec(
            num_scalar_prefetch=4,
            grid=(n_sorted // MOE_TILE,),
            in_specs=[pl.BlockSpec((MOE_TILE, HX_W), row),
                      up_spec(w_lo), up_spec(w_lo), down_spec(w_lo),
                      up_spec(w_hi), up_spec(w_hi), down_spec(w_hi)],
            out_specs=pl.BlockSpec((MOE_TILE, D_MODEL), row),
        ),
        compiler_params=_params(("arbitrary",)),
        name="moe_experts",
    )(tile_blk, tile_elo, tile_ehi, tile_act, sorted_hx, wg, wu, wd, wg, wu, wd)


def _pull_kernel(pos_ref, y_ref, x_ref, g2_ref, nf_ref, o_ref, buf_ref, sems, *, final_norm):
    i = pl.program_id(0)
    nsteps = pl.num_programs(0)

    def fetch(step, slot, wait):
        def body(r, carry):
            cp = _row_copy(y_ref, pos_ref[step * PULL_ROWS + r], buf_ref.at[slot], r, sems.at[slot])
            if wait:
                cp.wait()
            else:
                cp.start()
            return carry
        lax.fori_loop(0, PULL_ROWS, body, 0, unroll=8)

    @pl.when(i == 0)
    def _():
        fetch(0, 0, False)

    for slot in range(2):
        @pl.when(i % 2 == slot)
        def _():
            @pl.when(i + 1 < nsteps)
            def _():
                fetch(i + 1, 1 - slot, False)

            fetch(i, slot, True)
            x = x_ref[...] + g2_ref[...] * buf_ref[slot]
            if final_norm:
                x = _rms(x, nf_ref[...])
            o_ref[...] = x


def _pull(pos, y_sorted, x, mod, norm_f, rows_per_cond, final_norm):
    n = x.shape[0]
    tm = PULL_ROWS
    return pl.pallas_call(
        functools.partial(_pull_kernel, final_norm=final_norm),
        out_shape=jax.ShapeDtypeStruct((n, D_MODEL), F32),
        grid_spec=pltpu.PrefetchScalarGridSpec(
            num_scalar_prefetch=1,
            grid=(n // tm,),
            in_specs=[pl.BlockSpec(memory_space=pl.ANY),
                      pl.BlockSpec((tm, D_MODEL), lambda i, pos: (i, 0)),
                      pl.BlockSpec((None, 1, D_MODEL), lambda i, pos: (i * tm // rows_per_cond, 0, 5)),
                      pl.BlockSpec((1, D_MODEL), lambda i, pos: (0, 0))],
            out_specs=pl.BlockSpec((tm, D_MODEL), lambda i, pos: (i, 0)),
            scratch_shapes=[pltpu.VMEM((2, tm, D_MODEL), F32), pltpu.SemaphoreType.DMA((2,))],
        ),
        compiler_params=_params(("arbitrary",)),
        name="moe_pull",
    )(pos, y_sorted, x, mod, norm_f)


def _routed_moe(x, hx, mod, wg, wu, wd, layer, norm_f, rows_per_cond, final_norm):
    n = x.shape[0]
    n_sorted = n + N_BUCKETS * MOE_TILE
    n_tiles = n_sorted // MOE_TILE
    bk = hx[:, D_MODEL + 2].astype(jnp.int32)
    rank, counts = _plan(bk)
    padded = (counts + MOE_TILE - 1) // MOE_TILE * MOE_TILE
    ends = jnp.cumsum(padded)
    starts = ends - padded
    onehot = bk[:, None] == jnp.arange(N_BUCKETS, dtype=jnp.int32)[None, :]
    pos = jnp.sum(jnp.where(onehot, starts[None, :], 0), axis=1) + rank
    n_active = ends[-1] // MOE_TILE
    tile = jnp.arange(n_tiles, dtype=jnp.int32)
    tile_act = (tile < n_active).astype(jnp.int32)
    tile_blk = jnp.minimum(tile, n_active - 1)
    tile_bucket = jnp.minimum(jnp.sum(ends[None, :] <= (tile_blk * MOE_TILE)[:, None], axis=1), N_BUCKETS - 1)
    sorted_hx = _push(pos, starts + counts, ends, hx, n_sorted)
    y_sorted = _experts(tile_blk, jnp.asarray(_BUCKET_LO)[tile_bucket], jnp.asarray(_BUCKET_HI)[tile_bucket],
                        tile_act, sorted_hx, wg, wu, wd, layer)
    return _pull(pos, y_sorted, x, mod, norm_f, rows_per_cond, final_norm)


def _swap_pairs(x):
    lane = lax.broadcasted_iota(jnp.int32, x.shape, 1)
    first = (lane % MLA_ROPE) < (MLA_ROPE // 2)
    return jnp.where(first, pltpu.roll(x, LANES - MLA_ROPE // 2, axis=1),
                     pltpu.roll(x, MLA_ROPE // 2, axis=1))


def _mla_proj_kernel(*refs, rope):
    x_ref, g_ref, sc_ref, sh_ref, win_ref, qn_ref, kvn_ref, wuq_ref = refs[:8]
    pos = 8
    if rope:
        cos_ref, sin_ref = refs[pos:pos + 2]
        pos += 2
    q_ref, ckv_ref, kpe_ref = refs[pos:pos + 3]

    h = _normmod(x_ref[...], g_ref[...], sc_ref[...], sh_ref[...]).astype(BF16)
    p = jnp.dot(h, win_ref[...], preferred_element_type=F32)
    cq = _rms(p[:, :MLA_Q_RANK], qn_ref[...]).astype(BF16)
    ckv_ref[...] = _rms(p[:, MLA_Q_RANK:MLA_Q_RANK + MLA_KV_RANK], kvn_ref[...])
    kpe = p[:, MLA_Q_RANK + MLA_KV_RANK:]
    q = jnp.dot(cq, wuq_ref[...], preferred_element_type=F32) * (
        (MLA_NOPE + MLA_ROPE) ** -0.5 * float(np.log2(np.e)))
    if rope:
        cos = cos_ref[...]
        sin = sin_ref[...]
        kpe = kpe * cos + _swap_pairs(kpe) * sin
    kpe_ref[...] = kpe
    for hd in range(MLA_HEADS):
        lo = hd * HEAD_W
        q_ref[:, lo:lo + LANES] = q[:, lo:lo + LANES].astype(BF16)
        qp = q[:, lo + LANES:lo + HEAD_W]
        if rope:
            qp = qp * cos + _swap_pairs(qp) * sin
        q_ref[:, lo + LANES:lo + HEAD_W] = qp.astype(BF16)


def _mla_proj(x, mod, norm_g, w_in, q_norm, kv_norm, w_uq, rows_per_cond, cos=None, sin=None):
    n = x.shape[0]
    tm = 512
    rope = cos is not None
    full = lambda i: (0, 0)
    nin = w_in.shape[1]
    in_specs = [
        pl.BlockSpec((tm, D_MODEL), lambda i: (i, 0)),
        pl.BlockSpec((1, D_MODEL), full),
        _mod_spec(1, tm, rows_per_cond, 1),
        _mod_spec(0, tm, rows_per_cond, 1),
        pl.BlockSpec((D_MODEL, nin), full),
        pl.BlockSpec((1, MLA_Q_RANK), full),
        pl.BlockSpec((1, MLA_KV_RANK), full),
        pl.BlockSpec((MLA_Q_RANK, MLA_HEADS * HEAD_W), full),
    ]
    args = [x, norm_g, mod, mod, w_in, q_norm, kv_norm, w_uq]
    if rope:
        nt = cos.shape[0] // tm
        tab = pl.BlockSpec((tm, LANES), lambda i: (i % nt, 0))
        in_specs += [tab, tab]
        args += [cos, sin]
    return pl.pallas_call(
        functools.partial(_mla_proj_kernel, rope=rope),
        out_shape=[jax.ShapeDtypeStruct((n, MLA_HEADS * HEAD_W), BF16),
                   jax.ShapeDtypeStruct((n, MLA_KV_RANK), F32),
                   jax.ShapeDtypeStruct((n, LANES), F32)],
        grid=(n // tm,),
        in_specs=in_specs,
        out_specs=[pl.BlockSpec((tm, MLA_HEADS * HEAD_W), lambda i: (i, 0)),
                   pl.BlockSpec((tm, MLA_KV_RANK), lambda i: (i, 0)),
                   pl.BlockSpec((tm, LANES), lambda i: (i, 0))],
        compiler_params=_params(("arbitrary",)),
        name="mla_proj",
    )(*args)


def _kv_up_kernel(ckv_ref, kpe_ref, wuk_ref, wuv_ref, k_ref, v_ref):
    c = ckv_ref[...].astype(BF16)
    kn = jnp.dot(c, wuk_ref[...], preferred_element_type=F32).astype(BF16)
    vv = jnp.dot(c, wuv_ref[...], preferred_element_type=F32).astype(BF16)
    kpe = kpe_ref[...].astype(BF16)
    lane = lax.broadcasted_iota(jnp.int32, kpe.shape, 1)
    ones_col = jnp.where(lane == 0, 1.0, 0.0).astype(BF16)
    for hd in range(MLA_HEADS):
        k_ref[:, hd * HEAD_W:hd * HEAD_W + LANES] = kn[:, hd * MLA_NOPE:(hd + 1) * MLA_NOPE]
        k_ref[:, hd * HEAD_W + LANES:(hd + 1) * HEAD_W] = kpe
        v_ref[:, hd * HEAD_W:hd * HEAD_W + LANES] = vv[:, hd * MLA_V:(hd + 1) * MLA_V]
        v_ref[:, hd * HEAD_W + LANES:(hd + 1) * HEAD_W] = ones_col


def _kv_up(ckv, kpe, w_uk, w_uv):
    n = ckv.shape[0]
    tk = 512
    full = lambda i: (0, 0)
    return pl.pallas_call(
        _kv_up_kernel,
        out_shape=[jax.ShapeDtypeStruct((n, MLA_HEADS * HEAD_W), BF16),
                   jax.ShapeDtypeStruct((n, MLA_HEADS * HEAD_W), BF16)],
        grid=(n // tk,),
        in_specs=[
            pl.BlockSpec((tk, MLA_KV_RANK), lambda i: (i, 0)),
            pl.BlockSpec((tk, LANES), lambda i: (i, 0)),
            pl.BlockSpec((MLA_KV_RANK, MLA_HEADS * MLA_NOPE), full),
            pl.BlockSpec((MLA_KV_RANK, MLA_HEADS * MLA_V), full),
        ],
        out_specs=[pl.BlockSpec((tk, MLA_HEADS * HEAD_W), lambda i: (i, 0)),
                   pl.BlockSpec((tk, MLA_HEADS * HEAD_W), lambda i: (i, 0))],
        compiler_params=_params(("arbitrary",)),
        name="kv_up",
    )(ckv, kpe, w_uk, w_uv)


def _attn_kernel(q_ref, k_ref, v_ref, o_ref, *, heads):
    for hd in range(heads):
        k = k_ref[:, hd * HEAD_W:(hd + 1) * HEAD_W]
        v = v_ref[:, hd * HEAD_W:(hd + 1) * HEAD_W]
        for r in range(q_ref.shape[0] // ATTN_SUB):
            rows = slice(r * ATTN_SUB, (r + 1) * ATTN_SUB)
            q = q_ref[rows, hd * HEAD_W:(hd + 1) * HEAD_W]
            s = lax.dot_general(q, k, (((1,), (1,)), ((), ())), preferred_element_type=F32)
            m = jnp.max(s, axis=-1, keepdims=True)
            p = jnp.exp2(s - m).astype(BF16)
            o = jnp.dot(p, v, preferred_element_type=F32)
            o_ref[rows, hd * MLA_V:(hd + 1) * MLA_V] = (
                o[:, :MLA_V] / o[:, MLA_V:MLA_V + 1]).astype(o_ref.dtype)


def _attention(q, k, v, batch, tq_total, tk_total, heads_per_step, tq):
    nq = tq_total // tq
    nh = MLA_HEADS // heads_per_step
    hp = heads_per_step
    return pl.pallas_call(
        functools.partial(_attn_kernel, heads=hp),
        out_shape=jax.ShapeDtypeStruct((batch * tq_total, MLA_HEADS * MLA_V), BF16),
        grid=(batch, nh, nq),
        in_specs=[
            pl.BlockSpec((tq, hp * HEAD_W), lambda b, h, i: (b * nq + i, h)),
            pl.BlockSpec((tk_total, hp * HEAD_W), lambda b, h, i: (b, h)),
            pl.BlockSpec((tk_total, hp * HEAD_W), lambda b, h, i: (b, h)),
        ],
        out_specs=pl.BlockSpec((tq, hp * MLA_V), lambda b, h, i: (b * nq + i, h)),
        compiler_params=_params(("arbitrary", "arbitrary", "arbitrary")),
        name="attn",
    )(q, k, v)


_ROPE_PERM = np.concatenate([np.arange(0, 16), np.arange(32, 48), np.arange(16, 32), np.arange(48, 64)])


def _rope_tables(seq):
    pos = jnp.arange(seq)
    row = (pos // GRID_W).astype(F32)
    col = (pos % GRID_W).astype(F32)
    half = MLA_ROPE // 2
    inv = ROPE_BASE ** (-jnp.arange(0, half, 2, dtype=F32) / half)
    ang = jnp.concatenate([row[:, None] * inv, col[:, None] * inv], axis=1)
    cos = jnp.cos(ang)
    sin = jnp.sin(ang)
    pad = jnp.zeros((seq, LANES - MLA_ROPE), F32)
    return (jnp.concatenate([cos, cos, pad], axis=1),
            jnp.concatenate([-sin, sin, pad], axis=1))


def _prep_mla_weights(w_in, w_uq, w_uk, w_uv):
    nq = MLA_Q_RANK + MLA_KV_RANK
    w_in_p = jnp.concatenate(
        [w_in[:, :nq], w_in[:, nq:][:, _ROPE_PERM], jnp.zeros((D_MODEL, LANES - MLA_ROPE), F32)], axis=1)
    wq = w_uq.reshape(MLA_Q_RANK, MLA_HEADS, MLA_NOPE + MLA_ROPE)
    wq_p = jnp.concatenate(
        [wq[:, :, :MLA_NOPE], wq[:, :, MLA_NOPE:][:, :, _ROPE_PERM],
         jnp.zeros((MLA_Q_RANK, MLA_HEADS, HEAD_W - MLA_NOPE - MLA_ROPE), F32)], axis=2)
    return (w_in_p.astype(BF16), wq_p.reshape(MLA_Q_RANK, MLA_HEADS * HEAD_W).astype(BF16),
            w_uk.reshape(MLA_KV_RANK, MLA_HEADS * MLA_NOPE).astype(BF16),
            w_uv.reshape(MLA_KV_RANK, MLA_HEADS * MLA_V).astype(BF16))


def kernel(x_prompt, x_sample, state_ret_fwd, state_ret_bwd, cache_mla_ckv, cache_mla_kpe, c, c_ctx,
           w_ada, b_ada, norm1, norm2, norm_final, ret_w_in, ret_decay, ret_gn, ret_w_out,
           mla_w_in, mla_q_norm, mla_kv_norm, mla_w_uq, mla_w_uk, mla_w_uv, mla_w_o,
           w_router, router_bias, moe_w_gate, moe_w_up, moe_w_down):
    bp, tp, _ = x_prompt.shape
    bs, ts, _ = x_sample.shape
    past = cache_mla_ckv.shape[2]
    xp = x_prompt.reshape(bp * tp, D_MODEL)
    xs = x_sample.reshape(bs * ts, D_MODEL)

    conds = jnp.concatenate([c_ctx[None, :], c, jnp.zeros((8 - 1 - bs, D_MODEL), F32)], axis=0)
    mods = _ada_all(conds, w_ada, b_ada)

    wr_pad = jnp.pad(w_router, ((0, 0), (0, LANES - N_EXPERTS)))
    rb_pad = jnp.pad(router_bias, (0, LANES - N_EXPERTS)).reshape(1, LANES)
    nf = norm_final.reshape(1, D_MODEL)
    cos_t, sin_t = _rope_tables(ts)

    groups = {
        "p": dict(x=xp, rows_per_cond=bp * tp, lo=0, hi=1),
        "s": dict(x=xs, rows_per_cond=ts, lo=1, hi=1 + bs),
    }
    out_ckv, out_kpe = [], []
    ret_states = None
    n_ret = state_ret_fwd.shape[1]
    wg = moe_w_gate.astype(BF16)
    wu = moe_w_up.astype(BF16)
    wd = moe_w_down.astype(BF16)

    for i in range(DEPTH):
        j = i // 2
        n1 = norm1[i].reshape(1, D_MODEL)
        n2 = norm2[i].reshape(1, D_MODEL)
        mixed = {}
        if i % 2 == 0:
            w_in = ret_w_in[j].astype(BF16)
            w_out = ret_w_out[j].astype(BF16)
            dec = jnp.broadcast_to(ret_decay[j][:, :, None, None], (2, RET_HEADS, SCAN_CHUNK, SCAN_CHUNK))
            gn = ret_gn[j]
            for name, gr in groups.items():
                mod = mods[i, gr["lo"]:gr["hi"]].reshape(-1, 1, 6 * D_MODEL)
                proj = _proj(gr["x"], mod, n1, w_in, gr["rows_per_cond"],
                             2 * RET_HEADS * RET_DK + RET_HEADS * RET_DV)
                if name == "p":
                    y, *ret_states = _retention_scan(proj, dec, gn, bp, tp, state_slot=j,
                                                     state_bufs=ret_states, n_slots=n_ret)
                else:
                    (y,) = _retention_scan(proj, dec, gn, bs, ts, state_ret_fwd, state_ret_bwd, s0_slot=j)
                mixed[name] = (y, w_out, mod)
        else:
            w_in_p, w_uq_p, w_uk, w_uv = _prep_mla_weights(mla_w_in[j], mla_w_uq[j], mla_w_uk[j], mla_w_uv[j])
            w_o = mla_w_o[j].astype(BF16)
            qn = mla_q_norm[j].reshape(1, MLA_Q_RANK)
            kvn = mla_kv_norm[j].reshape(1, MLA_KV_RANK)
            for name, gr in groups.items():
                mod = mods[i, gr["lo"]:gr["hi"]].reshape(-1, 1, 6 * D_MODEL)
                if name == "p":
                    q, ckv, kpe = _mla_proj(gr["x"], mod, n1, w_in_p, qn, kvn, w_uq_p, gr["rows_per_cond"])
                    out_ckv.append(ckv.reshape(bp, tp, MLA_KV_RANK))
                    out_kpe.append(kpe[:, :MLA_ROPE][:, _ROPE_PERM].reshape(bp, tp, MLA_ROPE))
                    k, v = _kv_up(ckv, kpe, w_uk, w_uv)
                    o = _attention(q, k, v, bp, tp, tp, MLA_HEADS, tp)
                else:
                    q, ckv, kpe = _mla_proj(gr["x"], mod, n1, w_in_p, qn, kvn, w_uq_p, gr["rows_per_cond"],
                                            cos_t, sin_t)
                    cache_kpe = jnp.pad(cache_mla_kpe[:, j][:, :, _ROPE_PERM],
                                        ((0, 0), (0, 0), (0, LANES - MLA_ROPE)))
                    ckv_all = jnp.concatenate([ckv.reshape(bs, ts, MLA_KV_RANK), cache_mla_ckv[:, j]], axis=1)
                    kpe_all = jnp.concatenate([kpe.reshape(bs, ts, LANES), cache_kpe], axis=1)
                    tk = ts + past
                    k, v = _kv_up(ckv_all.reshape(bs * tk, MLA_KV_RANK), kpe_all.reshape(bs * tk, LANES),
                                  w_uk, w_uv)
                    o = _attention(q, k, v, bs, ts, tk, 1, 2048)
                mixed[name] = (o, w_o, mod)

        for name, gr in groups.items():
            y, w_o, mod = mixed[name]
            x_new, hx = _mix_out(y, w_o, gr["x"], mod, n2, wr_pad, rb_pad, gr["rows_per_cond"])
            gr["x"] = _routed_moe(x_new, hx, mod, wg, wu, wd, i, nf, gr["rows_per_cond"], i == DEPTH - 1)

    return (groups["p"]["x"].reshape(bp, tp, D_MODEL), groups["s"]["x"].reshape(bs, ts, D_MODEL),
            ret_states[0], ret_states[1],
            jnp.stack(out_ckv, axis=1), jnp.stack(out_kpe, axis=1))
```

```python
import functools

import numpy as np
import jax
import jax.numpy as jnp
from jax import lax
from jax.experimental import pallas as pl
from jax.experimental.pallas import tpu as pltpu

F32 = jnp.float32
BF16 = jnp.bfloat16

D_MODEL = 1024
DEPTH = 4
GRID_W = 64
RET_HEADS = 4
RET_DK = 256
RET_DV = 512
SCAN_CHUNK = 256
MLA_HEADS = 8
MLA_Q_RANK = 512
MLA_KV_RANK = 256
MLA_NOPE = 128
MLA_ROPE = 64
MLA_V = 128
ROPE_BASE = 10000.0
N_EXPERTS = 16
N_GROUPS = 4
EXPERTS_PER_GROUP = N_EXPERTS // N_GROUPS
EXPERT_FF = 256
EPS = 1e-6

LANES = 128
HEAD_W = 2 * LANES
VMEM_LIMIT = 48 * 1024 * 1024
MIX_SUB = 256
ATTN_SUB = 256
MOE_EG = 4
PAIRS_PER_GROUP = EXPERTS_PER_GROUP * (EXPERTS_PER_GROUP - 1) // 2
N_BUCKETS = N_GROUPS * PAIRS_PER_GROUP
HX_W = D_MODEL + LANES
MOE_TILE = 256
PLAN_BLK = 256
PUSH_ROWS = 1024
PULL_ROWS = 512

_PAIRS = [(a, b) for a in range(EXPERTS_PER_GROUP) for b in range(a + 1, EXPERTS_PER_GROUP)]
_BUCKET_LO = np.array([g * EXPERTS_PER_GROUP + a for g in range(N_GROUPS) for a, _ in _PAIRS], np.int32)
_BUCKET_HI = np.array([g * EXPERTS_PER_GROUP + b for g in range(N_GROUPS) for _, b in _PAIRS], np.int32)


def _params(sem):
    return pltpu.CompilerParams(dimension_semantics=sem, vmem_limit_bytes=VMEM_LIMIT)


def _sigmoid(x):
    return 1.0 / (1.0 + jnp.exp(-x))


def _normmod(x, g, sc, sh):
    y = x * lax.rsqrt(jnp.mean(x * x, axis=-1, keepdims=True) + EPS)
    return (y * g) * (1.0 + sc) + sh


def _rms(x, g):
    return x * lax.rsqrt(jnp.mean(x * x, axis=-1, keepdims=True) + EPS) * g


def _ada_kernel(c_ref, w_ref, b_ref, o_ref):
    c = c_ref[...]
    s = (c * _sigmoid(c)).astype(BF16)
    o_ref[0] = jnp.dot(s, w_ref[0].astype(BF16), preferred_element_type=F32) + b_ref[0]


def _ada_all(conds, w_ada, b_ada):
    tn = 1536
    n6 = 6 * D_MODEL
    return pl.pallas_call(
        _ada_kernel,
        out_shape=jax.ShapeDtypeStruct((DEPTH, 8, n6), F32),
        grid=(DEPTH, n6 // tn),
        in_specs=[
            pl.BlockSpec((8, D_MODEL), lambda l, j: (0, 0)),
            pl.BlockSpec((1, D_MODEL, tn), lambda l, j: (l, 0, j)),
            pl.BlockSpec((1, 1, tn), lambda l, j: (l, 0, j)),
        ],
        out_specs=pl.BlockSpec((1, 8, tn), lambda l, j: (l, 0, j)),
        compiler_params=_params(("arbitrary", "arbitrary")),
        name="ada",
    )(conds, w_ada, b_ada.reshape(DEPTH, 1, n6))


def _mod_spec(which, tm, rows_per_cond, ngrid):
    if ngrid == 1:
        return pl.BlockSpec((None, 1, D_MODEL), lambda i: (i * tm // rows_per_cond, 0, which))
    return pl.BlockSpec((None, 1, D_MODEL), lambda i, j: (i * tm // rows_per_cond, 0, which))


def _silu(g):
    hg = 0.5 * g
    return hg + hg * jnp.tanh(hg)


def _proj_kernel(x_ref, g_ref, sc_ref, sh_ref, w_ref, o_ref, h_ref, *, silu_from):
    j = pl.program_id(1)

    @pl.when(j == 0)
    def _():
        h_ref[...] = _normmod(x_ref[...], g_ref[...], sc_ref[...], sh_ref[...]).astype(BF16)

    acc = jnp.dot(h_ref[...], w_ref[...], preferred_element_type=F32)
    o_ref[...] = jnp.where(j >= silu_from, _silu(acc), acc).astype(o_ref.dtype)


def _proj(x, mod, norm_g, w, rows_per_cond, silu_from_col):
    n = x.shape[0]
    nout = w.shape[1]
    tm, tn = 1024, 2048
    return pl.pallas_call(
        functools.partial(_proj_kernel, silu_from=silu_from_col // tn),
        out_shape=jax.ShapeDtypeStruct((n, nout), BF16),
        grid=(n // tm, nout // tn),
        in_specs=[
            pl.BlockSpec((tm, D_MODEL), lambda i, j: (i, 0)),
            pl.BlockSpec((1, D_MODEL), lambda i, j: (0, 0)),
            _mod_spec(1, tm, rows_per_cond, 2),
            _mod_spec(0, tm, rows_per_cond, 2),
            pl.BlockSpec((D_MODEL, tn), lambda i, j: (0, j)),
        ],
        out_specs=pl.BlockSpec((tm, tn), lambda i, j: (i, j)),
        scratch_shapes=[pltpu.VMEM((tm, D_MODEL), BF16)],
        compiler_params=_params(("arbitrary", "arbitrary")),
        name="ret_proj",
    )(x, norm_g, mod, mod, w)


def _scan_kernel(*refs, seq, has_s0, aliased):
    q_ref, k_ref, v_ref, gf_ref, gb_ref, dec_ref, gn_ref = refs[:7]
    pos = 7
    if has_s0:
        s0_refs = refs[pos:pos + 2]
        pos += 2
    if aliased:
        pos += 2
    y_ref = refs[pos]
    s_refs = refs[pos + 1:pos + 3]
    intra_ref, qdec_ref, kdec_ref, cdec_ref, yacc_ref = refs[pos + 3:pos + 8]

    C = SCAN_CHUNK
    nc = seq // C

    @pl.when(pl.program_id(1) == 0)
    def _():
        ri = lax.broadcasted_iota(jnp.int32, (C, C), 0).astype(F32)
        ci = lax.broadcasted_iota(jnp.int32, (C, C), 1).astype(F32)
        k_scale = RET_DK ** -0.5
        for d in range(2):
            lg = -jnp.exp(dec_ref[d])
            if d == 1:
                diff, q_pow, k_pow = ci - ri, C - ri, ri
            else:
                diff, q_pow, k_pow = ri - ci, ri + 1.0, C - 1.0 - ri
            intra_ref[d] = jnp.where(diff >= 0, jnp.exp(lg * jnp.maximum(diff, 0.0)), 0.0) * k_scale
            qdec_ref[d] = jnp.exp(lg * q_pow)
            kdec_ref[d] = jnp.exp(lg * k_pow) * k_scale
            cdec_ref[d] = jnp.exp(lg[0:8, :] * float(C))

    state_is_zero = not has_s0 and nc == 1
    for d in range(2):
        if has_s0:
            s_refs[d][...] = s0_refs[d][...]
        elif not state_is_zero:
            s_refs[d][...] = jnp.zeros_like(s_refs[d])

    def lanes(a, width):
        return jnp.concatenate([a] * (width // C), axis=1)

    def contribution(rows, d, att):
        g_ref = (gf_ref, gb_ref)[d]
        s_ref = s_refs[d]
        q = q_ref[rows, :]
        k = k_ref[rows, :]
        v = v_ref[rows, :]
        att = (att * intra_ref[d]).astype(BF16)
        o = jnp.dot(att, v, preferred_element_type=F32)
        kd = (k.astype(F32) * lanes(kdec_ref[d], RET_DK)).astype(BF16)
        s_new = lax.dot_general(kd, v, (((0,), (0,)), ((), ())), preferred_element_type=F32)
        if state_is_zero:
            s_ref[...] = s_new
        else:
            qd = (q.astype(F32) * lanes(qdec_ref[d], RET_DK)).astype(BF16)
            s_old = s_ref[...]
            o = o + jnp.dot(qd, s_old.astype(BF16), preferred_element_type=F32)
            s_ref[...] = s_old * lanes(cdec_ref[d][0:1, :], RET_DV) + s_new
        mu = jnp.mean(o, axis=-1, keepdims=True)
        dlt = o - mu
        var = jnp.mean(dlt * dlt, axis=-1, keepdims=True)
        yn = (dlt * lax.rsqrt(var + EPS)) * gn_ref[d:d + 1, :]
        return g_ref[rows, :].astype(F32) * yn

    def scores(rows):
        return lax.dot_general(q_ref[rows, :], k_ref[rows, :], (((1,), (1,)), ((), ())),
                               preferred_element_type=F32)

    if nc == 1:
        rows = pl.ds(0, C)
        att = scores(rows)
        y_ref[...] = (contribution(rows, 0, att) + contribution(rows, 1, att)).astype(y_ref.dtype)
    else:
        def pair(i, second_visit):
            for d, c in ((0, i), (1, nc - 1 - i)):
                rows = pl.ds(pl.multiple_of(c * C, C), C)
                contrib = contribution(rows, d, scores(rows))
                if second_visit:
                    y_ref[rows, :] = (yacc_ref[rows, :] + contrib).astype(y_ref.dtype)
                else:
                    yacc_ref[rows, :] = contrib

        def first(i, carry):
            pair(i, False)
            return carry

        def second(i, carry):
            pair(i, True)
            return carry

        lax.fori_loop(0, nc // 2, first, 0)
        lax.fori_loop(nc // 2, nc, second, 0)


def _retention_scan(proj, dec, gn, batch, seq, s0f=None, s0b=None, s0_slot=0, state_slot=None,
                    state_bufs=None, n_slots=1):
    has_s0 = s0f is not None
    want_state = state_slot is not None
    aliased = state_bufs is not None
    hv = RET_HEADS * RET_DV
    nkb = RET_HEADS
    nvb = 2 * RET_HEADS * RET_DK // RET_DV
    C = SCAN_CHUNK
    assert seq % C == 0 and (seq == C or (seq // C) % 2 == 0)
    in_specs = [
        pl.BlockSpec((seq, RET_DK), lambda h, b: (b, h)),
        pl.BlockSpec((seq, RET_DK), lambda h, b: (b, nkb + h)),
        pl.BlockSpec((seq, RET_DV), lambda h, b: (b, nvb + h)),
        pl.BlockSpec((seq, RET_DV), lambda h, b: (b, nvb + RET_HEADS + h)),
        pl.BlockSpec((seq, RET_DV), lambda h, b: (b, nvb + 2 * RET_HEADS + h)),
        pl.BlockSpec((2, None, C, C), lambda h, b: (0, h, 0, 0)),
        pl.BlockSpec((2, RET_DV), lambda h, b: (0, h)),
    ]
    args = [proj, proj, proj, proj, proj, dec, gn]
    if has_s0:
        st_spec = pl.BlockSpec((None, None, None, RET_DK, RET_DV), lambda h, b: (b, s0_slot, h, 0, 0))
        in_specs += [st_spec, st_spec]
        args += [s0f, s0b]
    aliases = {}
    if aliased:
        in_specs += [pl.BlockSpec(memory_space=pl.ANY)] * 2
        aliases = {len(args): 1, len(args) + 1: 2}
        args += list(state_bufs)
    out_shape = [jax.ShapeDtypeStruct((batch * seq, hv), BF16)]
    out_specs = [pl.BlockSpec((seq, RET_DV), lambda h, b: (b, h))]
    scratch = [pltpu.VMEM((2, C, C), F32), pltpu.VMEM((2, C, C), F32), pltpu.VMEM((2, C, C), F32),
               pltpu.VMEM((2, 8, C), F32), pltpu.VMEM((seq, RET_DV), F32)]
    if want_state:
        st = jax.ShapeDtypeStruct((batch, n_slots, RET_HEADS, RET_DK, RET_DV), F32)
        out_shape += [st, st]
        st_spec = pl.BlockSpec((None, None, None, RET_DK, RET_DV), lambda h, b: (b, state_slot, h, 0, 0))
        out_specs += [st_spec, st_spec]
    else:
        scratch = [pltpu.VMEM((RET_DK, RET_DV), F32)] * 2 + scratch
    return pl.pallas_call(
        functools.partial(_scan_kernel, seq=seq, has_s0=has_s0, aliased=aliased),
        out_shape=out_shape,
        grid=(RET_HEADS, batch),
        in_specs=in_specs,
        out_specs=out_specs,
        scratch_shapes=scratch,
        input_output_aliases=aliases,
        compiler_params=_params(("arbitrary", "arbitrary")),
        name="ret_scan",
    )(*args)


def _route(logits, bias):
    s = _sigmoid(logits)
    sb = s + bias
    lane = lax.broadcasted_iota(jnp.int32, sb.shape, 1)
    lane_f = lane.astype(F32)
    neg = -jnp.inf
    big = float(LANES)

    def top2(vals):
        m1 = jnp.max(vals, axis=-1, keepdims=True)
        i1 = jnp.min(jnp.where(vals == m1, lane_f, big), axis=-1, keepdims=True)
        vals2 = jnp.where(lane_f == i1, neg, vals)
        m2 = jnp.max(vals2, axis=-1, keepdims=True)
        i2 = jnp.min(jnp.where(vals2 == m2, lane_f, big), axis=-1, keepdims=True)
        return m1 + m2, i1, i2

    best = None
    for g in range(N_GROUPS):
        in_group = (lane // EXPERTS_PER_GROUP) == g
        gs, i1, i2 = top2(jnp.where(in_group, sb, neg))
        if best is None:
            best = (gs, i1, i2)
        else:
            take = gs > best[0]
            best = (jnp.where(take, gs, best[0]), jnp.where(take, i1, best[1]),
                    jnp.where(take, i2, best[2]))
    _, e1, e2 = best
    w1 = jnp.sum(jnp.where(lane_f == e1, s, 0.0), axis=-1, keepdims=True)
    w2 = jnp.sum(jnp.where(lane_f == e2, s, 0.0), axis=-1, keepdims=True)
    den = w1 + w2
    grp = jnp.floor(e1 * (1.0 / EXPERTS_PER_GROUP))
    first_is_lo = e1 < e2
    lo = jnp.where(first_is_lo, e1, e2) - grp * EXPERTS_PER_GROUP
    hi = jnp.where(first_is_lo, e2, e1) - grp * EXPERTS_PER_GROUP
    bucket = grp * PAIRS_PER_GROUP + 3.0 * lo - 0.5 * lo * (lo - 1.0) + hi - lo - 1.0
    w_lo = jnp.where(first_is_lo, w1, w2) / den
    w_hi = jnp.where(first_is_lo, w2, w1) / den
    return (jnp.where(lane == 0, w_lo, 0.0) + jnp.where(lane == 1, w_hi, 0.0)
            + jnp.where(lane == 2, bucket, 0.0))


def _split_bf16(a):
    hi = a.astype(BF16)
    lo = (a - hi.astype(F32)).astype(BF16)
    return hi, lo


def _mix_out_kernel(y_ref, w_ref, x_ref, g1_ref, n2_ref, sc_ref, sh_ref, wr_ref, rb_ref,
                    xo_ref, hx_ref):
    w_hi, w_lo = _split_bf16(wr_ref[...])
    w_cat = jnp.concatenate([w_hi, w_lo], axis=1)
    for r in range(y_ref.shape[0] // MIX_SUB):
        rows = slice(r * MIX_SUB, (r + 1) * MIX_SUB)
        x = x_ref[rows, :] + g1_ref[...] * jnp.dot(y_ref[rows, :], w_ref[...], preferred_element_type=F32)
        xo_ref[rows, :] = x
        h = _normmod(x, n2_ref[...], sc_ref[...], sh_ref[...])
        hx_ref[rows, :D_MODEL] = h
        h_hi, h_lo = _split_bf16(h)
        both = jnp.dot(h_hi, w_cat, preferred_element_type=F32)
        logits = both[:, :LANES] + both[:, LANES:] + jnp.dot(h_lo, w_hi, preferred_element_type=F32)
        hx_ref[rows, D_MODEL:] = _route(logits, rb_ref[...])


def _mix_out(y, w, x, mod, norm2_g, wr_pad, rb_pad, rows_per_cond):
    n, kdim = y.shape
    tm = 512
    full = lambda i: (0, 0)
    return pl.pallas_call(
        _mix_out_kernel,
        out_shape=[jax.ShapeDtypeStruct((n, D_MODEL), F32),
                   jax.ShapeDtypeStruct((n, HX_W), F32)],
        grid=(n // tm,),
        in_specs=[
            pl.BlockSpec((tm, kdim), lambda i: (i, 0)),
            pl.BlockSpec((kdim, D_MODEL), full),
            pl.BlockSpec((tm, D_MODEL), lambda i: (i, 0)),
            _mod_spec(2, tm, rows_per_cond, 1),
            pl.BlockSpec((1, D_MODEL), full),
            _mod_spec(4, tm, rows_per_cond, 1),
            _mod_spec(3, tm, rows_per_cond, 1),
            pl.BlockSpec((D_MODEL, LANES), full),
            pl.BlockSpec((1, LANES), full),
        ],
        out_specs=[pl.BlockSpec((tm, D_MODEL), lambda i: (i, 0)),
                   pl.BlockSpec((tm, HX_W), lambda i: (i, 0))],
        compiler_params=_params(("arbitrary",)),
        name="mix_out",
    )(y, w, x, mod, norm2_g, mod, mod, wr_pad, rb_pad)


def _moe_kernel(x_ref, h_ref, gate_ref, g2_ref, wg_ref, wu_ref, wd_ref, nf_ref, o_ref, acc_ref,
                *, final_norm):
    step = pl.program_id(1)

    @pl.when(step == 0)
    def _():
        acc_ref[...] = jnp.zeros_like(acc_ref)

    h = h_ref[...]
    gate = gate_ref[...]
    lane = lax.broadcasted_iota(jnp.int32, gate.shape, 1)
    acts = []
    for k in range(MOE_EG):
        a = jnp.dot(h, wg_ref[k], preferred_element_type=F32)
        u = jnp.dot(h, wu_ref[k], preferred_element_type=F32)
        ge = jnp.sum(jnp.where(lane == step * MOE_EG + k, gate, 0.0), axis=-1, keepdims=True)
        acts.append((_silu(a) * u * ge).astype(BF16))
    act = jnp.concatenate(acts, axis=1)
    acc_ref[...] += jnp.dot(act, wd_ref[...].reshape(MOE_EG * EXPERT_FF, D_MODEL),
                            preferred_element_type=F32)

    @pl.when(step == N_EXPERTS // MOE_EG - 1)
    def _():
        x = x_ref[...] + g2_ref[...] * acc_ref[...]
        if final_norm:
            x = _rms(x, nf_ref[...])
        o_ref[...] = x


def _moe(x, h, gate, mod, wg, wu, wd, layer, norm_f, rows_per_cond, final_norm):
    n = x.shape[0]
    tm = 1024
    return pl.pallas_call(
        functools.partial(_moe_kernel, final_norm=final_norm),
        out_shape=jax.ShapeDtypeStruct((n, D_MODEL), F32),
        grid=(n // tm, N_EXPERTS // MOE_EG),
        in_specs=[
            pl.BlockSpec((tm, D_MODEL), lambda i, e: (i, 0)),
            pl.BlockSpec((tm, D_MODEL), lambda i, e: (i, 0)),
            pl.BlockSpec((tm, LANES), lambda i, e: (i, 0)),
            _mod_spec(5, tm, rows_per_cond, 2),
            pl.BlockSpec((None, MOE_EG, D_MODEL, EXPERT_FF), lambda i, e: (layer, e, 0, 0)),
            pl.BlockSpec((None, MOE_EG, D_MODEL, EXPERT_FF), lambda i, e: (layer, e, 0, 0)),
            pl.BlockSpec((None, MOE_EG, EXPERT_FF, D_MODEL), lambda i, e: (layer, e, 0, 0)),
            pl.BlockSpec((1, D_MODEL), lambda i, e: (0, 0)),
        ],
        out_specs=pl.BlockSpec((tm, D_MODEL), lambda i, e: (i, 0)),
        scratch_shapes=[pltpu.VMEM((tm, D_MODEL), F32)],
        compiler_params=_params(("arbitrary", "arbitrary")),
        name="moe",
    )(x, h, gate, mod, wg, wu, wd, norm_f)


def _plan_kernel(bk_ref, rank_ref, cnt_ref, carry_ref):
    i = pl.program_id(0)

    @pl.when(i == 0)
    def _():
        carry_ref[...] = jnp.zeros_like(carry_ref)

    b = bk_ref[0]
    sub = lax.broadcasted_iota(jnp.int32, (LANES, PLAN_BLK), 0)
    onehot = jnp.where(sub == b, 1.0, 0.0)
    ti = lax.broadcasted_iota(jnp.int32, (PLAN_BLK, PLAN_BLK), 0)
    tj = lax.broadcasted_iota(jnp.int32, (PLAN_BLK, PLAN_BLK), 1)
    upper = jnp.where(ti <= tj, 1.0, 0.0).astype(BF16)
    running = jnp.dot(onehot.astype(BF16), upper, preferred_element_type=F32) + carry_ref[...]
    rank_ref[0] = (jnp.sum(onehot * running, axis=0, keepdims=True) - 1.0).astype(jnp.int32)
    carry_ref[...] += jnp.sum(onehot, axis=1, keepdims=True)
    cnt_ref[...] = jnp.broadcast_to(carry_ref[...], cnt_ref.shape)


def _plan(bk):
    n = bk.shape[0]
    nb = n // PLAN_BLK
    rank, cnt = pl.pallas_call(
        _plan_kernel,
        out_shape=[jax.ShapeDtypeStruct((nb, 1, PLAN_BLK), jnp.int32),
                   jax.ShapeDtypeStruct((LANES, LANES), F32)],
        grid=(nb,),
        in_specs=[pl.BlockSpec((1, 1, PLAN_BLK), lambda i: (i, 0, 0))],
        out_specs=[pl.BlockSpec((1, 1, PLAN_BLK), lambda i: (i, 0, 0)),
                   pl.BlockSpec((LANES, LANES), lambda i: (0, 0))],
        scratch_shapes=[pltpu.VMEM((LANES, 1), F32)],
        compiler_params=_params(("arbitrary",)),
        name="moe_plan",
    )(bk.reshape(nb, 1, PLAN_BLK))
    return rank.reshape(n), cnt[:N_BUCKETS, 0].astype(jnp.int32)


def _row_copy(src_ref, src_row, dst_ref, dst_row, sem):
    return pltpu.make_async_copy(src_ref.at[pl.ds(src_row, 1), :], dst_ref.at[pl.ds(dst_row, 1), :], sem)


def _push_kernel(pos_ref, pad_lo_ref, pad_hi_ref, hx_ref, out_ref, zero_ref, sem, zsem):
    i = pl.program_id(0)
    nsteps = pl.num_programs(0)
    base = i * PUSH_ROWS

    copies = [_row_copy(hx_ref, r, out_ref, pos_ref[base + r], sem) for r in range(PUSH_ROWS)]
    for r, cp in enumerate(copies):
        cp.start(priority=r % 2)
    for cp in copies:
        cp.wait()

    @pl.when(i == nsteps - 1)
    def _():
        zero_ref[...] = jnp.zeros_like(zero_ref)
        for k in range(N_BUCKETS):
            def zfill(p, carry):
                _row_copy(zero_ref, 0, out_ref, p, zsem).start()
                return carry

            def zwait(p, carry):
                _row_copy(zero_ref, 0, out_ref, p, zsem).wait()
                return carry

            lax.fori_loop(pad_lo_ref[k], pad_hi_ref[k], zfill, 0)
            lax.fori_loop(pad_lo_ref[k], pad_hi_ref[k], zwait, 0)


def _push(pos, pad_lo, pad_hi, hx, n_sorted):
    n = hx.shape[0]
    return pl.pallas_call(
        _push_kernel,
        out_shape=jax.ShapeDtypeStruct((n_sorted, HX_W), F32),
        grid_spec=pltpu.PrefetchScalarGridSpec(
            num_scalar_prefetch=3,
            grid=(n // PUSH_ROWS,),
            in_specs=[pl.BlockSpec((PUSH_ROWS, HX_W), lambda i, pos, lo, hi: (i, 0))],
            out_specs=pl.BlockSpec(memory_space=pl.ANY),
            scratch_shapes=[pltpu.VMEM((8, HX_W), F32), pltpu.SemaphoreType.DMA(()),
                            pltpu.SemaphoreType.DMA(())],
        ),
        compiler_params=_params(("arbitrary",)),
        name="moe_push",
    )(pos, pad_lo, pad_hi, hx)


def _experts_kernel(blk_ref, elo_ref, ehi_ref, act_ref, s_ref, wg_hbm, wu_hbm, wd_hbm, y_ref,
                    wg_ref, wu_ref, wd_ref, wsem, *, layer):
    i = pl.program_id(0)

    @pl.when(i == 0)
    def _():
        loads = [pltpu.make_async_copy(src.at[layer], dst, wsem.at[k])
                 for k, (src, dst) in enumerate(((wg_hbm, wg_ref), (wu_hbm, wu_ref), (wd_hbm, wd_ref)))]
        for cp in loads:
            cp.start()
        for cp in loads:
            cp.wait()

    @pl.when(act_ref[i] == 1)
    def _():
        hx = s_ref[...]
        h = hx[:, :D_MODEL].astype(BF16)
        y = None
        for lane, e in ((0, elo_ref[i]), (1, ehi_ref[i])):
            a = jnp.dot(h, wg_ref[e], preferred_element_type=F32)
            u = jnp.dot(h, wu_ref[e], preferred_element_type=F32)
            w = hx[:, D_MODEL + lane:D_MODEL + lane + 1]
            act = (_silu(a) * u * w).astype(BF16)
            part = jnp.dot(act, wd_ref[e], preferred_element_type=F32)
            y = part if y is None else y + part
        y_ref[...] = y


def _experts(tile_blk, tile_elo, tile_ehi, tile_act, sorted_hx, wg, wu, wd, layer):
    n_sorted = sorted_hx.shape[0]
    row = lambda i, blk, elo, ehi, act: (blk[i], 0)
    hbm = pl.BlockSpec(memory_space=pl.ANY)
    return pl.pallas_call(
        functools.partial(_experts_kernel, layer=layer),
        out_shape=jax.ShapeDtypeStruct((n_sorted, D_MODEL), F32),
        grid_spec=pltpu.PrefetchScalarGridSpec(
            num_scalar_prefetch=4,
            grid=(n_sorted // MOE_TILE,),
            in_specs=[pl.BlockSpec((MOE_TILE, HX_W), row), hbm, hbm, hbm],
            out_specs=pl.BlockSpec((MOE_TILE, D_MODEL), row),
            scratch_shapes=[pltpu.VMEM((N_EXPERTS, D_MODEL, EXPERT_FF), BF16),
                            pltpu.VMEM((N_EXPERTS, D_MODEL, EXPERT_FF), BF16),
                            pltpu.VMEM((N_EXPERTS, EXPERT_FF, D_MODEL), BF16),
                            pltpu.SemaphoreType.DMA((3,))],
        ),
        compiler_params=_params(("arbitrary",)),
        name="moe_experts",
    )(tile_blk, tile_elo, tile_ehi, tile_act, sorted_hx, wg, wu, wd)


def _pull_kernel(pos_ref, y_ref, x_ref, g2_ref, nf_ref, o_ref, buf_ref, sems, *, final_norm):
    i = pl.program_id(0)
    nsteps = pl.num_programs(0)

    def fetch(step, slot, wait):
        for r in range(PULL_ROWS):
            cp = _row_copy(y_ref, pos_ref[step * PULL_ROWS + r], buf_ref.at[slot], r, sems.at[slot])
            if wait:
                cp.wait()
            else:
                cp.start(priority=r % 2)

    @pl.when(i == 0)
    def _():
        fetch(0, 0, False)

    for slot in range(2):
        @pl.when(i % 2 == slot)
        def _():
            @pl.when(i + 1 < nsteps)
            def _():
                fetch(i + 1, 1 - slot, False)

            fetch(i, slot, True)
            x = x_ref[...] + g2_ref[...] * buf_ref[slot]
            if final_norm:
                x = _rms(x, nf_ref[...])
            o_ref[...] = x


def _pull(pos, y_sorted, x, mod, norm_f, rows_per_cond, final_norm):
    n = x.shape[0]
    tm = PULL_ROWS
    return pl.pallas_call(
        functools.partial(_pull_kernel, final_norm=final_norm),
        out_shape=jax.ShapeDtypeStruct((n, D_MODEL), F32),
        grid_spec=pltpu.PrefetchScalarGridSpec(
            num_scalar_prefetch=1,
            grid=(n // tm,),
            in_specs=[pl.BlockSpec(memory_space=pl.ANY),
                      pl.BlockSpec((tm, D_MODEL), lambda i, pos: (i, 0)),
                      pl.BlockSpec((None, 1, D_MODEL), lambda i, pos: (i * tm // rows_per_cond, 0, 5)),
                      pl.BlockSpec((1, D_MODEL), lambda i, pos: (0, 0))],
            out_specs=pl.BlockSpec((tm, D_MODEL), lambda i, pos: (i, 0)),
            scratch_shapes=[pltpu.VMEM((2, tm, D_MODEL), F32), pltpu.SemaphoreType.DMA((2,))],
        ),
        compiler_params=_params(("arbitrary",)),
        name="moe_pull",
    )(pos, y_sorted, x, mod, norm_f)


def _routed_moe(x, hx, mod, wg, wu, wd, layer, norm_f, rows_per_cond, final_norm):
    n = x.shape[0]
    n_sorted = n + N_BUCKETS * MOE_TILE
    n_tiles = n_sorted // MOE_TILE
    bk = hx[:, D_MODEL + 2].astype(jnp.int32)
    rank, counts = _plan(bk)
    padded = (counts + MOE_TILE - 1) // MOE_TILE * MOE_TILE
    ends = jnp.cumsum(padded)
    starts = ends - padded
    onehot = bk[:, None] == jnp.arange(N_BUCKETS, dtype=jnp.int32)[None, :]
    pos = jnp.sum(jnp.where(onehot, starts[None, :], 0), axis=1) + rank
    n_active = ends[-1] // MOE_TILE
    tile = jnp.arange(n_tiles, dtype=jnp.int32)
    tile_act = (tile < n_active).astype(jnp.int32)
    tile_blk = jnp.minimum(tile, n_active - 1)
    tile_bucket = jnp.minimum(jnp.sum(ends[None, :] <= (tile_blk * MOE_TILE)[:, None], axis=1), N_BUCKETS - 1)
    sorted_hx = _push(pos, starts + counts, ends, hx, n_sorted)
    y_sorted = _experts(tile_blk, jnp.asarray(_BUCKET_LO)[tile_bucket], jnp.asarray(_BUCKET_HI)[tile_bucket],
                        tile_act, sorted_hx, wg, wu, wd, layer)
    return _pull(pos, y_sorted, x, mod, norm_f, rows_per_cond, final_norm)


def _swap_pairs(x):
    lane = lax.broadcasted_iota(jnp.int32, x.shape, 1)
    first = (lane % MLA_ROPE) < (MLA_ROPE // 2)
    return jnp.where(first, pltpu.roll(x, LANES - MLA_ROPE // 2, axis=1),
                     pltpu.roll(x, MLA_ROPE // 2, axis=1))


def _mla_proj_kernel(*refs, rope):
    x_ref, g_ref, sc_ref, sh_ref, win_ref, qn_ref, kvn_ref, wuq_ref = refs[:8]
    pos = 8
    if rope:
        cos_ref, sin_ref = refs[pos:pos + 2]
        pos += 2
    q_ref, ckv_ref, kpe_ref = refs[pos:pos + 3]

    h = _normmod(x_ref[...], g_ref[...], sc_ref[...], sh_ref[...]).astype(BF16)
    p = jnp.dot(h, win_ref[...], preferred_element_type=F32)
    cq = _rms(p[:, :MLA_Q_RANK], qn_ref[...]).astype(BF16)
    ckv_ref[...] = _rms(p[:, MLA_Q_RANK:MLA_Q_RANK + MLA_KV_RANK], kvn_ref[...])
    kpe = p[:, MLA_Q_RANK + MLA_KV_RANK:]
    q = jnp.dot(cq, wuq_ref[...], preferred_element_type=F32) * (
        (MLA_NOPE + MLA_ROPE) ** -0.5 * float(np.log2(np.e)))
    if rope:
        cos = cos_ref[...]
        sin = sin_ref[...]
        kpe = kpe * cos + _swap_pairs(kpe) * sin
    kpe_ref[...] = kpe
    for hd in range(MLA_HEADS):
        lo = hd * HEAD_W
        q_ref[:, lo:lo + LANES] = q[:, lo:lo + LANES].astype(BF16)
        qp = q[:, lo + LANES:lo + HEAD_W]
        if rope:
            qp = qp * cos + _swap_pairs(qp) * sin
        q_ref[:, lo + LANES:lo + HEAD_W] = qp.astype(BF16)


def _mla_proj(x, mod, norm_g, w_in, q_norm, kv_norm, w_uq, rows_per_cond, cos=None, sin=None):
    n = x.shape[0]
    tm = 512
    rope = cos is not None
    full = lambda i: (0, 0)
    nin = w_in.shape[1]
    in_specs = [
        pl.BlockSpec((tm, D_MODEL), lambda i: (i, 0)),
        pl.BlockSpec((1, D_MODEL), full),
        _mod_spec(1, tm, rows_per_cond, 1),
        _mod_spec(0, tm, rows_per_cond, 1),
        pl.BlockSpec((D_MODEL, nin), full),
        pl.BlockSpec((1, MLA_Q_RANK), full),
        pl.BlockSpec((1, MLA_KV_RANK), full),
        pl.BlockSpec((MLA_Q_RANK, MLA_HEADS * HEAD_W), full),
    ]
    args = [x, norm_g, mod, mod, w_in, q_norm, kv_norm, w_uq]
    if rope:
        nt = cos.shape[0] // tm
        tab = pl.BlockSpec((tm, LANES), lambda i: (i % nt, 0))
        in_specs += [tab, tab]
        args += [cos, sin]
    return pl.pallas_call(
        functools.partial(_mla_proj_kernel, rope=rope),
        out_shape=[jax.ShapeDtypeStruct((n, MLA_HEADS * HEAD_W), BF16),
                   jax.ShapeDtypeStruct((n, MLA_KV_RANK), F32),
                   jax.ShapeDtypeStruct((n, LANES), F32)],
        grid=(n // tm,),
        in_specs=in_specs,
        out_specs=[pl.BlockSpec((tm, MLA_HEADS * HEAD_W), lambda i: (i, 0)),
                   pl.BlockSpec((tm, MLA_KV_RANK), lambda i: (i, 0)),
                   pl.BlockSpec((tm, LANES), lambda i: (i, 0))],
        compiler_params=_params(("arbitrary",)),
        name="mla_proj",
    )(*args)


def _kv_up_kernel(ckv_ref, kpe_ref, wuk_ref, wuv_ref, k_ref, v_ref):
    c = ckv_ref[...].astype(BF16)
    kn = jnp.dot(c, wuk_ref[...], preferred_element_type=F32).astype(BF16)
    vv = jnp.dot(c, wuv_ref[...], preferred_element_type=F32).astype(BF16)
    kpe = kpe_ref[...].astype(BF16)
    lane = lax.broadcasted_iota(jnp.int32, kpe.shape, 1)
    ones_col = jnp.where(lane == 0, 1.0, 0.0).astype(BF16)
    for hd in range(MLA_HEADS):
        k_ref[:, hd * HEAD_W:hd * HEAD_W + LANES] = kn[:, hd * MLA_NOPE:(hd + 1) * MLA_NOPE]
        k_ref[:, hd * HEAD_W + LANES:(hd + 1) * HEAD_W] = kpe
        v_ref[:, hd * HEAD_W:hd * HEAD_W + LANES] = vv[:, hd * MLA_V:(hd + 1) * MLA_V]
        v_ref[:, hd * HEAD_W + LANES:(hd + 1) * HEAD_W] = ones_col


def _kv_up(ckv, kpe, w_uk, w_uv):
    n = ckv.shape[0]
    tk = 512
    full = lambda i: (0, 0)
    return pl.pallas_call(
        _kv_up_kernel,
        out_shape=[jax.ShapeDtypeStruct((n, MLA_HEADS * HEAD_W), BF16),
                   jax.ShapeDtypeStruct((n, MLA_HEADS * HEAD_W), BF16)],
        grid=(n // tk,),
        in_specs=[
            pl.BlockSpec((tk, MLA_KV_RANK), lambda i: (i, 0)),
            pl.BlockSpec((tk, LANES), lambda i: (i, 0)),
            pl.BlockSpec((MLA_KV_RANK, MLA_HEADS * MLA_NOPE), full),
            pl.BlockSpec((MLA_KV_RANK, MLA_HEADS * MLA_V), full),
        ],
        out_specs=[pl.BlockSpec((tk, MLA_HEADS * HEAD_W), lambda i: (i, 0)),
                   pl.BlockSpec((tk, MLA_HEADS * HEAD_W), lambda i: (i, 0))],
        compiler_params=_params(("arbitrary",)),
        name="kv_up",
    )(ckv, kpe, w_uk, w_uv)


def _attn_kernel(q_ref, k_ref, v_ref, o_ref, *, heads):
    for hd in range(heads):
        k = k_ref[:, hd * HEAD_W:(hd + 1) * HEAD_W]
        v = v_ref[:, hd * HEAD_W:(hd + 1) * HEAD_W]
        for r in range(q_ref.shape[0] // ATTN_SUB):
            rows = slice(r * ATTN_SUB, (r + 1) * ATTN_SUB)
            q = q_ref[rows, hd * HEAD_W:(hd + 1) * HEAD_W]
            s = lax.dot_general(q, k, (((1,), (1,)), ((), ())), preferred_element_type=F32)
            m = jnp.max(s, axis=-1, keepdims=True)
            p = jnp.exp2(s - m).astype(BF16)
            o = jnp.dot(p, v, preferred_element_type=F32)
            o_ref[rows, hd * MLA_V:(hd + 1) * MLA_V] = (
                o[:, :MLA_V] / o[:, MLA_V:MLA_V + 1]).astype(o_ref.dtype)


def _attention(q, k, v, batch, tq_total, tk_total, heads_per_step, tq):
    nq = tq_total // tq
    nh = MLA_HEADS // heads_per_step
    hp = heads_per_step
    return pl.pallas_call(
        functools.partial(_attn_kernel, heads=hp),
        out_shape=jax.ShapeDtypeStruct((batch * tq_total, MLA_HEADS * MLA_V), BF16),
        grid=(batch, nh, nq),
        in_specs=[
            pl.BlockSpec((tq, hp * HEAD_W), lambda b, h, i: (b * nq + i, h)),
            pl.BlockSpec((tk_total, hp * HEAD_W), lambda b, h, i: (b, h)),
            pl.BlockSpec((tk_total, hp * HEAD_W), lambda b, h, i: (b, h)),
        ],
        out_specs=pl.BlockSpec((tq, hp * MLA_V), lambda b, h, i: (b * nq + i, h)),
        compiler_params=_params(("arbitrary", "arbitrary", "arbitrary")),
        name="attn",
    )(q, k, v)


_ROPE_PERM = np.concatenate([np.arange(0, 16), np.arange(32, 48), np.arange(16, 32), np.arange(48, 64)])


def _rope_tables(seq):
    pos = jnp.arange(seq)
    row = (pos // GRID_W).astype(F32)
    col = (pos % GRID_W).astype(F32)
    half = MLA_ROPE // 2
    inv = ROPE_BASE ** (-jnp.arange(0, half, 2, dtype=F32) / half)
    ang = jnp.concatenate([row[:, None] * inv, col[:, None] * inv], axis=1)
    cos = jnp.cos(ang)
    sin = jnp.sin(ang)
    pad = jnp.zeros((seq, LANES - MLA_ROPE), F32)
    return (jnp.concatenate([cos, cos, pad], axis=1),
            jnp.concatenate([-sin, sin, pad], axis=1))


def _prep_mla_weights(w_in, w_uq, w_uk, w_uv):
    nq = MLA_Q_RANK + MLA_KV_RANK
    w_in_p = jnp.concatenate(
        [w_in[:, :nq], w_in[:, nq:][:, _ROPE_PERM], jnp.zeros((D_MODEL, LANES - MLA_ROPE), F32)], axis=1)
    wq = w_uq.reshape(MLA_Q_RANK, MLA_HEADS, MLA_NOPE + MLA_ROPE)
    wq_p = jnp.concatenate(
        [wq[:, :, :MLA_NOPE], wq[:, :, MLA_NOPE:][:, :, _ROPE_PERM],
         jnp.zeros((MLA_Q_RANK, MLA_HEADS, HEAD_W - MLA_NOPE - MLA_ROPE), F32)], axis=2)
    return (w_in_p.astype(BF16), wq_p.reshape(MLA_Q_RANK, MLA_HEADS * HEAD_W).astype(BF16),
            w_uk.reshape(MLA_KV_RANK, MLA_HEADS * MLA_NOPE).astype(BF16),
            w_uv.reshape(MLA_KV_RANK, MLA_HEADS * MLA_V).astype(BF16))


def kernel(x_prompt, x_sample, state_ret_fwd, state_ret_bwd, cache_mla_ckv, cache_mla_kpe, c, c_ctx,
           w_ada, b_ada, norm1, norm2, norm_final, ret_w_in, ret_decay, ret_gn, ret_w_out,
           mla_w_in, mla_q_norm, mla_kv_norm, mla_w_uq, mla_w_uk, mla_w_uv, mla_w_o,
           w_router, router_bias, moe_w_gate, moe_w_up, moe_w_down):
    bp, tp, _ = x_prompt.shape
    bs, ts, _ = x_sample.shape
    past = cache_mla_ckv.shape[2]
    xp = x_prompt.reshape(bp * tp, D_MODEL)
    xs = x_sample.reshape(bs * ts, D_MODEL)

    conds = jnp.concatenate([c_ctx[None, :], c, jnp.zeros((8 - 1 - bs, D_MODEL), F32)], axis=0)
    mods = _ada_all(conds, w_ada, b_ada)

    wr_pad = jnp.pad(w_router, ((0, 0), (0, LANES - N_EXPERTS)))
    rb_pad = jnp.pad(router_bias, (0, LANES - N_EXPERTS)).reshape(1, LANES)
    nf = norm_final.reshape(1, D_MODEL)
    cos_t, sin_t = _rope_tables(ts)

    groups = {
        "p": dict(x=xp, rows_per_cond=bp * tp, lo=0, hi=1),
        "s": dict(x=xs, rows_per_cond=ts, lo=1, hi=1 + bs),
    }
    out_ckv, out_kpe = [], []
    ret_states = None
    n_ret = state_ret_fwd.shape[1]
    wg = moe_w_gate.astype(BF16)
    wu = moe_w_up.astype(BF16)
    wd = moe_w_down.astype(BF16)

    for i in range(DEPTH):
        j = i // 2
        n1 = norm1[i].reshape(1, D_MODEL)
        n2 = norm2[i].reshape(1, D_MODEL)
        mixed = {}
        if i % 2 == 0:
            w_in = ret_w_in[j].astype(BF16)
            w_out = ret_w_out[j].astype(BF16)
            dec = jnp.broadcast_to(ret_decay[j][:, :, None, None], (2, RET_HEADS, SCAN_CHUNK, SCAN_CHUNK))
            gn = ret_gn[j]
            for name, gr in groups.items():
                mod = mods[i, gr["lo"]:gr["hi"]].reshape(-1, 1, 6 * D_MODEL)
                proj = _proj(gr["x"], mod, n1, w_in, gr["rows_per_cond"],
                             2 * RET_HEADS * RET_DK + RET_HEADS * RET_DV)
                if name == "p":
                    y, *ret_states = _retention_scan(proj, dec, gn, bp, tp, state_slot=j,
                                                     state_bufs=ret_states, n_slots=n_ret)
                else:
                    (y,) = _retention_scan(proj, dec, gn, bs, ts, state_ret_fwd, state_ret_bwd, s0_slot=j)
                mixed[name] = (y, w_out, mod)
        else:
            w_in_p, w_uq_p, w_uk, w_uv = _prep_mla_weights(mla_w_in[j], mla_w_uq[j], mla_w_uk[j], mla_w_uv[j])
            w_o = mla_w_o[j].astype(BF16)
            qn = mla_q_norm[j].reshape(1, MLA_Q_RANK)
            kvn = mla_kv_norm[j].reshape(1, MLA_KV_RANK)
            for name, gr in groups.items():
                mod = mods[i, gr["lo"]:gr["hi"]].reshape(-1, 1, 6 * D_MODEL)
                if name == "p":
                    q, ckv, kpe = _mla_proj(gr["x"], mod, n1, w_in_p, qn, kvn, w_uq_p, gr["rows_per_cond"])
                    out_ckv.append(ckv.reshape(bp, tp, MLA_KV_RANK))
                    out_kpe.append(kpe[:, :MLA_ROPE][:, _ROPE_PERM].reshape(bp, tp, MLA_ROPE))
                    k, v = _kv_up(ckv, kpe, w_uk, w_uv)
                    o = _attention(q, k, v, bp, tp, tp, MLA_HEADS, tp)
                else:
                    q, ckv, kpe = _mla_proj(gr["x"], mod, n1, w_in_p, qn, kvn, w_uq_p, gr["rows_per_cond"],
                                            cos_t, sin_t)
                    cache_kpe = jnp.pad(cache_mla_kpe[:, j][:, :, _ROPE_PERM],
                                        ((0, 0), (0, 0), (0, LANES - MLA_ROPE)))
                    ckv_all = jnp.concatenate([ckv.reshape(bs, ts, MLA_KV_RANK), cache_mla_ckv[:, j]], axis=1)
                    kpe_all = jnp.concatenate([kpe.reshape(bs, ts, LANES), cache_kpe], axis=1)
                    tk = ts + past
                    k, v = _kv_up(ckv_all.reshape(bs * tk, MLA_KV_RANK), kpe_all.reshape(bs * tk, LANES),
                                  w_uk, w_uv)
                    o = _attention(q, k, v, bs, ts, tk, 1, 2048)
                mixed[name] = (o, w_o, mod)

        for name, gr in groups.items():
            y, w_o, mod = mixed[name]
            x_new, hx = _mix_out(y, w_o, gr["x"], mod, n2, wr_pad, rb_pad, gr["rows_per_cond"])
            gr["x"] = _routed_moe(x_new, hx, mod, wg, wu, wd, i, nf, gr["rows_per_cond"], i == DEPTH - 1)

    return (groups["p"]["x"].reshape(bp, tp, D_MODEL), groups["s"]["x"].reshape(bs, ts, D_MODEL),
            ret_states[0], ret_states[1],
            jnp.stack(out_ckv, axis=1), jnp.stack(out_kpe, axis=1))
```

```python
import functools

import numpy as np
import jax
import jax.numpy as jnp
from jax import lax
from jax.experimental import pallas as pl
from jax.experimental.pallas import tpu as pltpu

F32 = jnp.float32
BF16 = jnp.bfloat16

D_MODEL = 1024
DEPTH = 4
GRID_W = 64
RET_HEADS = 4
RET_DK = 256
RET_DV = 512
SCAN_CHUNK = 256
MLA_HEADS = 8
MLA_Q_RANK = 512
MLA_KV_RANK = 256
MLA_NOPE = 128
MLA_ROPE = 64
MLA_V = 128
ROPE_BASE = 10000.0
N_EXPERTS = 16
N_GROUPS = 4
EXPERTS_PER_GROUP = N_EXPERTS // N_GROUPS
EXPERT_FF = 256
EPS = 1e-6

LANES = 128
HEAD_W = 2 * LANES
VMEM_LIMIT = 48 * 1024 * 1024
MIX_SUB = 256
ATTN_SUB = 256
MOE_EG = 4
PAIRS_PER_GROUP = EXPERTS_PER_GROUP * (EXPERTS_PER_GROUP - 1) // 2
N_BUCKETS = N_GROUPS * PAIRS_PER_GROUP
HX_W = D_MODEL + LANES
MOE_TILE = 256
PLAN_BLK = 256
PUSH_ROWS = 1024
PULL_ROWS = 512

_PAIRS = [(a, b) for a in range(EXPERTS_PER_GROUP) for b in range(a + 1, EXPERTS_PER_GROUP)]
_BUCKET_LO = np.array([g * EXPERTS_PER_GROUP + a for g in range(N_GROUPS) for a, _ in _PAIRS], np.int32)
_BUCKET_HI = np.array([g * EXPERTS_PER_GROUP + b for g in range(N_GROUPS) for _, b in _PAIRS], np.int32)


def _params(sem):
    return pltpu.CompilerParams(dimension_semantics=sem, vmem_limit_bytes=VMEM_LIMIT)


def _sigmoid(x):
    return 1.0 / (1.0 + jnp.exp(-x))


def _normmod(x, g, sc, sh):
    y = x * lax.rsqrt(jnp.mean(x * x, axis=-1, keepdims=True) + EPS)
    return (y * g) * (1.0 + sc) + sh


def _rms(x, g):
    return x * lax.rsqrt(jnp.mean(x * x, axis=-1, keepdims=True) + EPS) * g


def _ada_kernel(c_ref, w_ref, b_ref, o_ref):
    c = c_ref[...]
    s = (c * _sigmoid(c)).astype(BF16)
    o_ref[0] = jnp.dot(s, w_ref[0].astype(BF16), preferred_element_type=F32) + b_ref[0]


def _ada_all(conds, w_ada, b_ada):
    tn = 1536
    n6 = 6 * D_MODEL
    return pl.pallas_call(
        _ada_kernel,
        out_shape=jax.ShapeDtypeStruct((DEPTH, 8, n6), F32),
        grid=(DEPTH, n6 // tn),
        in_specs=[
            pl.BlockSpec((8, D_MODEL), lambda l, j: (0, 0)),
            pl.BlockSpec((1, D_MODEL, tn), lambda l, j: (l, 0, j)),
            pl.BlockSpec((1, 1, tn), lambda l, j: (l, 0, j)),
        ],
        out_specs=pl.BlockSpec((1, 8, tn), lambda l, j: (l, 0, j)),
        compiler_params=_params(("arbitrary", "arbitrary")),
        name="ada",
    )(conds, w_ada, b_ada.reshape(DEPTH, 1, n6))


def _mod_spec(which, tm, rows_per_cond, ngrid):
    if ngrid == 1:
        return pl.BlockSpec((None, 1, D_MODEL), lambda i: (i * tm // rows_per_cond, 0, which))
    return pl.BlockSpec((None, 1, D_MODEL), lambda i, j: (i * tm // rows_per_cond, 0, which))


def _silu(g):
    hg = 0.5 * g
    return hg + hg * jnp.tanh(hg)


def _proj_kernel(x_ref, g_ref, sc_ref, sh_ref, w_ref, o_ref, h_ref, *, silu_from):
    j = pl.program_id(1)

    @pl.when(j == 0)
    def _():
        h_ref[...] = _normmod(x_ref[...], g_ref[...], sc_ref[...], sh_ref[...]).astype(BF16)

    acc = jnp.dot(h_ref[...], w_ref[...], preferred_element_type=F32)
    o_ref[...] = jnp.where(j >= silu_from, _silu(acc), acc).astype(o_ref.dtype)


def _proj(x, mod, norm_g, w, rows_per_cond, silu_from_col):
    n = x.shape[0]
    nout = w.shape[1]
    tm, tn = 1024, 2048
    return pl.pallas_call(
        functools.partial(_proj_kernel, silu_from=silu_from_col // tn),
        out_shape=jax.ShapeDtypeStruct((n, nout), BF16),
        grid=(n // tm, nout // tn),
        in_specs=[
            pl.BlockSpec((tm, D_MODEL), lambda i, j: (i, 0)),
            pl.BlockSpec((1, D_MODEL), lambda i, j: (0, 0)),
            _mod_spec(1, tm, rows_per_cond, 2),
            _mod_spec(0, tm, rows_per_cond, 2),
            pl.BlockSpec((D_MODEL, tn), lambda i, j: (0, j)),
        ],
        out_specs=pl.BlockSpec((tm, tn), lambda i, j: (i, j)),
        scratch_shapes=[pltpu.VMEM((tm, D_MODEL), BF16)],
        compiler_params=_params(("arbitrary", "arbitrary")),
        name="ret_proj",
    )(x, norm_g, mod, mod, w)


def _scan_kernel(*refs, seq, has_s0, aliased):
    q_ref, k_ref, v_ref, gf_ref, gb_ref, dec_ref, gn_ref = refs[:7]
    pos = 7
    if has_s0:
        s0_refs = refs[pos:pos + 2]
        pos += 2
    if aliased:
        pos += 2
    y_ref = refs[pos]
    s_refs = refs[pos + 1:pos + 3]
    intra_ref, qdec_ref, kdec_ref, cdec_ref, yacc_ref = refs[pos + 3:pos + 8]

    C = SCAN_CHUNK
    nc = seq // C

    @pl.when(pl.program_id(1) == 0)
    def _():
        ri = lax.broadcasted_iota(jnp.int32, (C, C), 0).astype(F32)
        ci = lax.broadcasted_iota(jnp.int32, (C, C), 1).astype(F32)
        k_scale = RET_DK ** -0.5
        for d in range(2):
            lg = -jnp.exp(dec_ref[d])
            if d == 1:
                diff, q_pow, k_pow = ci - ri, C - ri, ri
            else:
                diff, q_pow, k_pow = ri - ci, ri + 1.0, C - 1.0 - ri
            intra_ref[d] = jnp.where(diff >= 0, jnp.exp(lg * jnp.maximum(diff, 0.0)), 0.0) * k_scale
            qdec_ref[d] = jnp.exp(lg * q_pow)
            kdec_ref[d] = jnp.exp(lg * k_pow) * k_scale
            cdec_ref[d] = jnp.exp(lg[0:8, :] * float(C))

    state_is_zero = not has_s0 and nc == 1
    for d in range(2):
        if has_s0:
            s_refs[d][...] = s0_refs[d][...]
        elif not state_is_zero:
            s_refs[d][...] = jnp.zeros_like(s_refs[d])

    def lanes(a, width):
        return jnp.concatenate([a] * (width // C), axis=1)

    def contribution(rows, d, att):
        g_ref = (gf_ref, gb_ref)[d]
        s_ref = s_refs[d]
        q = q_ref[rows, :]
        k = k_ref[rows, :]
        v = v_ref[rows, :]
        att = (att * intra_ref[d]).astype(BF16)
        o = jnp.dot(att, v, preferred_element_type=F32)
        kd = (k.astype(F32) * lanes(kdec_ref[d], RET_DK)).astype(BF16)
        s_new = lax.dot_general(kd, v, (((0,), (0,)), ((), ())), preferred_element_type=F32)
        if state_is_zero:
            s_ref[...] = s_new
        else:
            qd = (q.astype(F32) * lanes(qdec_ref[d], RET_DK)).astype(BF16)
            s_old = s_ref[...]
            o = o + jnp.dot(qd, s_old.astype(BF16), preferred_element_type=F32)
            s_ref[...] = s_old * lanes(cdec_ref[d][0:1, :], RET_DV) + s_new
        mu = jnp.mean(o, axis=-1, keepdims=True)
        dlt = o - mu
        var = jnp.mean(dlt * dlt, axis=-1, keepdims=True)
        yn = (dlt * lax.rsqrt(var + EPS)) * gn_ref[d:d + 1, :]
        return g_ref[rows, :].astype(F32) * yn

    def scores(rows):
        return lax.dot_general(q_ref[rows, :], k_ref[rows, :], (((1,), (1,)), ((), ())),
                               preferred_element_type=F32)

    if nc == 1:
        rows = pl.ds(0, C)
        att = scores(rows)
        y_ref[...] = (contribution(rows, 0, att) + contribution(rows, 1, att)).astype(y_ref.dtype)
    else:
        def pair(i, second_visit):
            for d, c in ((0, i), (1, nc - 1 - i)):
                rows = pl.ds(pl.multiple_of(c * C, C), C)
                contrib = contribution(rows, d, scores(rows))
                if second_visit:
                    y_ref[rows, :] = (yacc_ref[rows, :] + contrib).astype(y_ref.dtype)
                else:
                    yacc_ref[rows, :] = contrib

        def first(i, carry):
            pair(i, False)
            return carry

        def second(i, carry):
            pair(i, True)
            return carry

        lax.fori_loop(0, nc // 2, first, 0)
        lax.fori_loop(nc // 2, nc, second, 0)


def _retention_scan(proj, dec, gn, batch, seq, s0f=None, s0b=None, s0_slot=0, state_slot=None,
                    state_bufs=None, n_slots=1):
    has_s0 = s0f is not None
    want_state = state_slot is not None
    aliased = state_bufs is not None
    hv = RET_HEADS * RET_DV
    nkb = RET_HEADS
    nvb = 2 * RET_HEADS * RET_DK // RET_DV
    C = SCAN_CHUNK
    assert seq % C == 0 and (seq == C or (seq // C) % 2 == 0)
    in_specs = [
        pl.BlockSpec((seq, RET_DK), lambda h, b: (b, h)),
        pl.BlockSpec((seq, RET_DK), lambda h, b: (b, nkb + h)),
        pl.BlockSpec((seq, RET_DV), lambda h, b: (b, nvb + h)),
        pl.BlockSpec((seq, RET_DV), lambda h, b: (b, nvb + RET_HEADS + h)),
        pl.BlockSpec((seq, RET_DV), lambda h, b: (b, nvb + 2 * RET_HEADS + h)),
        pl.BlockSpec((2, None, C, C), lambda h, b: (0, h, 0, 0)),
        pl.BlockSpec((2, RET_DV), lambda h, b: (0, h)),
    ]
    args = [proj, proj, proj, proj, proj, dec, gn]
    if has_s0:
        st_spec = pl.BlockSpec((None, None, None, RET_DK, RET_DV), lambda h, b: (b, s0_slot, h, 0, 0))
        in_specs += [st_spec, st_spec]
        args += [s0f, s0b]
    aliases = {}
    if aliased:
        in_specs += [pl.BlockSpec(memory_space=pl.ANY)] * 2
        aliases = {len(args): 1, len(args) + 1: 2}
        args += list(state_bufs)
    out_shape = [jax.ShapeDtypeStruct((batch * seq, hv), BF16)]
    out_specs = [pl.BlockSpec((seq, RET_DV), lambda h, b: (b, h))]
    scratch = [pltpu.VMEM((2, C, C), F32), pltpu.VMEM((2, C, C), F32), pltpu.VMEM((2, C, C), F32),
               pltpu.VMEM((2, 8, C), F32), pltpu.VMEM((seq, RET_DV), F32)]
    if want_state:
        st = jax.ShapeDtypeStruct((batch, n_slots, RET_HEADS, RET_DK, RET_DV), F32)
        out_shape += [st, st]
        st_spec = pl.BlockSpec((None, None, None, RET_DK, RET_DV), lambda h, b: (b, state_slot, h, 0, 0))
        out_specs += [st_spec, st_spec]
    else:
        scratch = [pltpu.VMEM((RET_DK, RET_DV), F32)] * 2 + scratch
    return pl.pallas_call(
        functools.partial(_scan_kernel, seq=seq, has_s0=has_s0, aliased=aliased),
        out_shape=out_shape,
        grid=(RET_HEADS, batch),
        in_specs=in_specs,
        out_specs=out_specs,
        scratch_shapes=scratch,
        input_output_aliases=aliases,
        compiler_params=_params(("arbitrary", "arbitrary")),
        name="ret_scan",
    )(*args)


def _route(logits, bias):
    s = _sigmoid(logits)
    sb = s + bias
    lane = lax.broadcasted_iota(jnp.int32, sb.shape, 1)
    lane_f = lane.astype(F32)
    neg = -jnp.inf
    big = float(LANES)

    def top2(vals):
        m1 = jnp.max(vals, axis=-1, keepdims=True)
        i1 = jnp.min(jnp.where(vals == m1, lane_f, big), axis=-1, keepdims=True)
        vals2 = jnp.where(lane_f == i1, neg, vals)
        m2 = jnp.max(vals2, axis=-1, keepdims=True)
        i2 = jnp.min(jnp.where(vals2 == m2, lane_f, big), axis=-1, keepdims=True)
        return m1 + m2, i1, i2

    best = None
    for g in range(N_GROUPS):
        in_group = (lane // EXPERTS_PER_GROUP) == g
        gs, i1, i2 = top2(jnp.where(in_group, sb, neg))
        if best is None:
            best = (gs, i1, i2)
        else:
            take = gs > best[0]
            best = (jnp.where(take, gs, best[0]), jnp.where(take, i1, best[1]),
                    jnp.where(take, i2, best[2]))
    _, e1, e2 = best
    w1 = jnp.sum(jnp.where(lane_f == e1, s, 0.0), axis=-1, keepdims=True)
    w2 = jnp.sum(jnp.where(lane_f == e2, s, 0.0), axis=-1, keepdims=True)
    den = w1 + w2
    grp = jnp.floor(e1 * (1.0 / EXPERTS_PER_GROUP))
    first_is_lo = e1 < e2
    lo = jnp.where(first_is_lo, e1, e2) - grp * EXPERTS_PER_GROUP
    hi = jnp.where(first_is_lo, e2, e1) - grp * EXPERTS_PER_GROUP
    bucket = grp * PAIRS_PER_GROUP + 3.0 * lo - 0.5 * lo * (lo - 1.0) + hi - lo - 1.0
    w_lo = jnp.where(first_is_lo, w1, w2) / den
    w_hi = jnp.where(first_is_lo, w2, w1) / den
    return (jnp.where(lane == 0, w_lo, 0.0) + jnp.where(lane == 1, w_hi, 0.0)
            + jnp.where(lane == 2, bucket, 0.0))


def _split_bf16(a):
    hi = a.astype(BF16)
    lo = (a - hi.astype(F32)).astype(BF16)
    return hi, lo


def _mix_out_kernel(y_ref, w_ref, x_ref, g1_ref, n2_ref, sc_ref, sh_ref, wr_ref, rb_ref,
                    xo_ref, hx_ref):
    w_hi, w_lo = _split_bf16(wr_ref[...])
    w_cat = jnp.concatenate([w_hi, w_lo], axis=1)
    for r in range(y_ref.shape[0] // MIX_SUB):
        rows = slice(r * MIX_SUB, (r + 1) * MIX_SUB)
        x = x_ref[rows, :] + g1_ref[...] * jnp.dot(y_ref[rows, :], w_ref[...], preferred_element_type=F32)
        xo_ref[rows, :] = x
        h = _normmod(x, n2_ref[...], sc_ref[...], sh_ref[...])
        hx_ref[rows, :D_MODEL] = h
        h_hi, h_lo = _split_bf16(h)
        both = jnp.dot(h_hi, w_cat, preferred_element_type=F32)
        logits = both[:, :LANES] + both[:, LANES:] + jnp.dot(h_lo, w_hi, preferred_element_type=F32)
        hx_ref[rows, D_MODEL:] = _route(logits, rb_ref[...])


def _mix_out(y, w, x, mod, norm2_g, wr_pad, rb_pad, rows_per_cond):
    n, kdim = y.shape
    tm = 512
    full = lambda i: (0, 0)
    return pl.pallas_call(
        _mix_out_kernel,
        out_shape=[jax.ShapeDtypeStruct((n, D_MODEL), F32),
                   jax.ShapeDtypeStruct((n, HX_W), F32)],
        grid=(n // tm,),
        in_specs=[
            pl.BlockSpec((tm, kdim), lambda i: (i, 0)),
            pl.BlockSpec((kdim, D_MODEL), full),
            pl.BlockSpec((tm, D_MODEL), lambda i: (i, 0)),
            _mod_spec(2, tm, rows_per_cond, 1),
            pl.BlockSpec((1, D_MODEL), full),
            _mod_spec(4, tm, rows_per_cond, 1),
            _mod_spec(3, tm, rows_per_cond, 1),
            pl.BlockSpec((D_MODEL, LANES), full),
            pl.BlockSpec((1, LANES), full),
        ],
        out_specs=[pl.BlockSpec((tm, D_MODEL), lambda i: (i, 0)),
                   pl.BlockSpec((tm, HX_W), lambda i: (i, 0))],
        compiler_params=_params(("arbitrary",)),
        name="mix_out",
    )(y, w, x, mod, norm2_g, mod, mod, wr_pad, rb_pad)


def _moe_kernel(x_ref, h_ref, gate_ref, g2_ref, wg_ref, wu_ref, wd_ref, nf_ref, o_ref, acc_ref,
                *, final_norm):
    step = pl.program_id(1)

    @pl.when(step == 0)
    def _():
        acc_ref[...] = jnp.zeros_like(acc_ref)

    h = h_ref[...]
    gate = gate_ref[...]
    lane = lax.broadcasted_iota(jnp.int32, gate.shape, 1)
    acts = []
    for k in range(MOE_EG):
        a = jnp.dot(h, wg_ref[k], preferred_element_type=F32)
        u = jnp.dot(h, wu_ref[k], preferred_element_type=F32)
        ge = jnp.sum(jnp.where(lane == step * MOE_EG + k, gate, 0.0), axis=-1, keepdims=True)
        acts.append((_silu(a) * u * ge).astype(BF16))
    act = jnp.concatenate(acts, axis=1)
    acc_ref[...] += jnp.dot(act, wd_ref[...].reshape(MOE_EG * EXPERT_FF, D_MODEL),
                            preferred_element_type=F32)

    @pl.when(step == N_EXPERTS // MOE_EG - 1)
    def _():
        x = x_ref[...] + g2_ref[...] * acc_ref[...]
        if final_norm:
            x = _rms(x, nf_ref[...])
        o_ref[...] = x


def _moe(x, h, gate, mod, wg, wu, wd, layer, norm_f, rows_per_cond, final_norm):
    n = x.shape[0]
    tm = 1024
    return pl.pallas_call(
        functools.partial(_moe_kernel, final_norm=final_norm),
        out_shape=jax.ShapeDtypeStruct((n, D_MODEL), F32),
        grid=(n // tm, N_EXPERTS // MOE_EG),
        in_specs=[
            pl.BlockSpec((tm, D_MODEL), lambda i, e: (i, 0)),
            pl.BlockSpec((tm, D_MODEL), lambda i, e: (i, 0)),
            pl.BlockSpec((tm, LANES), lambda i, e: (i, 0)),
            _mod_spec(5, tm, rows_per_cond, 2),
            pl.BlockSpec((None, MOE_EG, D_MODEL, EXPERT_FF), lambda i, e: (layer, e, 0, 0)),
            pl.BlockSpec((None, MOE_EG, D_MODEL, EXPERT_FF), lambda i, e: (layer, e, 0, 0)),
            pl.BlockSpec((None, MOE_EG, EXPERT_FF, D_MODEL), lambda i, e: (layer, e, 0, 0)),
            pl.BlockSpec((1, D_MODEL), lambda i, e: (0, 0)),
        ],
        out_specs=pl.BlockSpec((tm, D_MODEL), lambda i, e: (i, 0)),
        scratch_shapes=[pltpu.VMEM((tm, D_MODEL), F32)],
        compiler_params=_params(("arbitrary", "arbitrary")),
        name="moe",
    )(x, h, gate, mod, wg, wu, wd, norm_f)


def _plan_kernel(bk_ref, rank_ref, cnt_ref, carry_ref):
    i = pl.program_id(0)

    @pl.when(i == 0)
    def _():
        carry_ref[...] = jnp.zeros_like(carry_ref)

    b = bk_ref[0]
    sub = lax.broadcasted_iota(jnp.int32, (LANES, PLAN_BLK), 0)
    onehot = jnp.where(sub == b, 1.0, 0.0)
    ti = lax.broadcasted_iota(jnp.int32, (PLAN_BLK, PLAN_BLK), 0)
    tj = lax.broadcasted_iota(jnp.int32, (PLAN_BLK, PLAN_BLK), 1)
    upper = jnp.where(ti <= tj, 1.0, 0.0).astype(BF16)
    running = jnp.dot(onehot.astype(BF16), upper, preferred_element_type=F32) + carry_ref[...]
    rank_ref[0] = (jnp.sum(onehot * running, axis=0, keepdims=True) - 1.0).astype(jnp.int32)
    carry_ref[...] += jnp.sum(onehot, axis=1, keepdims=True)
    cnt_ref[...] = jnp.broadcast_to(carry_ref[...], cnt_ref.shape)


def _plan(bk):
    n = bk.shape[0]
    nb = n // PLAN_BLK
    rank, cnt = pl.pallas_call(
        _plan_kernel,
        out_shape=[jax.ShapeDtypeStruct((nb, 1, PLAN_BLK), jnp.int32),
                   jax.ShapeDtypeStruct((LANES, LANES), F32)],
        grid=(nb,),
        in_specs=[pl.BlockSpec((1, 1, PLAN_BLK), lambda i: (i, 0, 0))],
        out_specs=[pl.BlockSpec((1, 1, PLAN_BLK), lambda i: (i, 0, 0)),
                   pl.BlockSpec((LANES, LANES), lambda i: (0, 0))],
        scratch_shapes=[pltpu.VMEM((LANES, 1), F32)],
        compiler_params=_params(("arbitrary",)),
        name="moe_plan",
    )(bk.reshape(nb, 1, PLAN_BLK))
    return rank.reshape(n), cnt[:N_BUCKETS, 0].astype(jnp.int32)


def _row_copy(src_ref, src_row, dst_ref, dst_row, sem):
    return pltpu.make_async_copy(src_ref.at[pl.ds(src_row, 1), :], dst_ref.at[pl.ds(dst_row, 1), :], sem)


def _invert_kernel(pos_ref, src_ref):
    def clear(s, carry):
        src_ref[s] = 0
        return carry

    def put(t, carry):
        src_ref[pos_ref[t]] = t
        return carry

    lax.fori_loop(0, src_ref.shape[0], clear, 0, unroll=8)
    lax.fori_loop(0, pos_ref.shape[0], put, 0, unroll=8)


def _invert(pos, n_sorted):
    return pl.pallas_call(
        _invert_kernel,
        out_shape=jax.ShapeDtypeStruct((n_sorted,), jnp.int32),
        grid_spec=pltpu.PrefetchScalarGridSpec(
            num_scalar_prefetch=1,
            grid=(1,),
            in_specs=[],
            out_specs=pl.BlockSpec(memory_space=pltpu.SMEM),
        ),
        compiler_params=_params(("arbitrary",)),
        name="moe_invert",
    )(pos)


def _experts_kernel(src_ref, elo_ref, ehi_ref, act_ref, hx_hbm, wg_hbm, wu_hbm, wd_hbm, y_ref,
                    wg_ref, wu_ref, wd_ref, buf_ref, wsem, sems, *, layer):
    i = pl.program_id(0)
    nsteps = pl.num_programs(0)

    def gather(step, slot, wait):
        for r in range(MOE_TILE):
            cp = _row_copy(hx_hbm, src_ref[step * MOE_TILE + r], buf_ref.at[slot], r, sems.at[slot])
            if wait:
                cp.wait()
            else:
                cp.start(priority=r % 2)

    @pl.when(i == 0)
    def _():
        gather(0, 0, False)
        loads = [pltpu.make_async_copy(src.at[layer], dst, wsem.at[k])
                 for k, (src, dst) in enumerate(((wg_hbm, wg_ref), (wu_hbm, wu_ref), (wd_hbm, wd_ref)))]
        for cp in loads:
            cp.start()
        for cp in loads:
            cp.wait()

    for slot in range(2):
        @pl.when(jnp.logical_and(act_ref[i] == 1, i % 2 == slot))
        def _():
            @pl.when(jnp.logical_and(i + 1 < nsteps, act_ref[jnp.minimum(i + 1, nsteps - 1)] == 1))
            def _():
                gather(i + 1, 1 - slot, False)

            gather(i, slot, True)
            _expert_pair(buf_ref[slot], elo_ref[i], ehi_ref[i], wg_ref, wu_ref, wd_ref, y_ref)


def _expert_pair(hx, e_lo, e_hi, wg_ref, wu_ref, wd_ref, y_ref):
    h = hx[:, :D_MODEL].astype(BF16)
    y = None
    for lane, e in ((0, e_lo), (1, e_hi)):
        a = jnp.dot(h, wg_ref[e], preferred_element_type=F32)
        u = jnp.dot(h, wu_ref[e], preferred_element_type=F32)
        w = hx[:, D_MODEL + lane:D_MODEL + lane + 1]
        act = (_silu(a) * u * w).astype(BF16)
        part = jnp.dot(act, wd_ref[e], preferred_element_type=F32)
        y = part if y is None else y + part
    y_ref[...] = y


def _experts(src, tile_blk, tile_elo, tile_ehi, tile_act, hx, wg, wu, wd, layer, n_sorted):
    row = lambda i, src, elo, ehi, act, blk: (blk[i], 0)
    hbm = pl.BlockSpec(memory_space=pl.ANY)

    def kern(src_ref, elo_ref, ehi_ref, act_ref, blk_ref, *refs):
        _experts_kernel(src_ref, elo_ref, ehi_ref, act_ref, *refs, layer=layer)

    return pl.pallas_call(
        kern,
        out_shape=jax.ShapeDtypeStruct((n_sorted, D_MODEL), F32),
        grid_spec=pltpu.PrefetchScalarGridSpec(
            num_scalar_prefetch=5,
            grid=(n_sorted // MOE_TILE,),
            in_specs=[hbm, hbm, hbm, hbm],
            out_specs=pl.BlockSpec((MOE_TILE, D_MODEL), row),
            scratch_shapes=[pltpu.VMEM((N_EXPERTS, D_MODEL, EXPERT_FF), BF16),
                            pltpu.VMEM((N_EXPERTS, D_MODEL, EXPERT_FF), BF16),
                            pltpu.VMEM((N_EXPERTS, EXPERT_FF, D_MODEL), BF16),
                            pltpu.VMEM((2, MOE_TILE, HX_W), F32),
                            pltpu.SemaphoreType.DMA((3,)),
                            pltpu.SemaphoreType.DMA((2,))],
        ),
        compiler_params=_params(("arbitrary",)),
        name="moe_experts",
    )(src, tile_elo, tile_ehi, tile_act, tile_blk, hx, wg, wu, wd)


def _pull_kernel(pos_ref, y_ref, x_ref, g2_ref, nf_ref, o_ref, buf_ref, sems, *, final_norm):
    i = pl.program_id(0)
    nsteps = pl.num_programs(0)

    def fetch(step, slot, wait):
        for r in range(PULL_ROWS):
            cp = _row_copy(y_ref, pos_ref[step * PULL_ROWS + r], buf_ref.at[slot], r, sems.at[slot])
            if wait:
                cp.wait()
            else:
                cp.start(priority=r % 2)

    @pl.when(i == 0)
    def _():
        fetch(0, 0, False)

    for slot in range(2):
        @pl.when(i % 2 == slot)
        def _():
            @pl.when(i + 1 < nsteps)
            def _():
                fetch(i + 1, 1 - slot, False)

            fetch(i, slot, True)
            x = x_ref[...] + g2_ref[...] * buf_ref[slot]
            if final_norm:
                x = _rms(x, nf_ref[...])
            o_ref[...] = x


def _pull(pos, y_sorted, x, mod, norm_f, rows_per_cond, final_norm):
    n = x.shape[0]
    tm = PULL_ROWS
    return pl.pallas_call(
        functools.partial(_pull_kernel, final_norm=final_norm),
        out_shape=jax.ShapeDtypeStruct((n, D_MODEL), F32),
        grid_spec=pltpu.PrefetchScalarGridSpec(
            num_scalar_prefetch=1,
            grid=(n // tm,),
            in_specs=[pl.BlockSpec(memory_space=pl.ANY),
                      pl.BlockSpec((tm, D_MODEL), lambda i, pos: (i, 0)),
                      pl.BlockSpec((None, 1, D_MODEL), lambda i, pos: (i * tm // rows_per_cond, 0, 5)),
                      pl.BlockSpec((1, D_MODEL), lambda i, pos: (0, 0))],
            out_specs=pl.BlockSpec((tm, D_MODEL), lambda i, pos: (i, 0)),
            scratch_shapes=[pltpu.VMEM((2, tm, D_MODEL), F32), pltpu.SemaphoreType.DMA((2,))],
        ),
        compiler_params=_params(("arbitrary",)),
        name="moe_pull",
    )(pos, y_sorted, x, mod, norm_f)


def _routed_moe(x, hx, mod, wg, wu, wd, layer, norm_f, rows_per_cond, final_norm):
    n = x.shape[0]
    n_sorted = n + N_BUCKETS * MOE_TILE
    n_tiles = n_sorted // MOE_TILE
    bk = hx[:, D_MODEL + 2].astype(jnp.int32)
    rank, counts = _plan(bk)
    padded = (counts + MOE_TILE - 1) // MOE_TILE * MOE_TILE
    ends = jnp.cumsum(padded)
    starts = ends - padded
    onehot = bk[:, None] == jnp.arange(N_BUCKETS, dtype=jnp.int32)[None, :]
    pos = jnp.sum(jnp.where(onehot, starts[None, :], 0), axis=1) + rank
    n_active = ends[-1] // MOE_TILE
    tile = jnp.arange(n_tiles, dtype=jnp.int32)
    tile_act = (tile < n_active).astype(jnp.int32)
    tile_blk = jnp.minimum(tile, n_active - 1)
    tile_bucket = jnp.minimum(jnp.sum(ends[None, :] <= (tile_blk * MOE_TILE)[:, None], axis=1), N_BUCKETS - 1)
    src = _invert(pos, n_sorted)
    y_sorted = _experts(src, tile_blk, jnp.asarray(_BUCKET_LO)[tile_bucket], jnp.asarray(_BUCKET_HI)[tile_bucket],
                        tile_act, hx, wg, wu, wd, layer, n_sorted)
    return _pull(pos, y_sorted, x, mod, norm_f, rows_per_cond, final_norm)


def _swap_pairs(x):
    lane = lax.broadcasted_iota(jnp.int32, x.shape, 1)
    first = (lane % MLA_ROPE) < (MLA_ROPE // 2)
    return jnp.where(first, pltpu.roll(x, LANES - MLA_ROPE // 2, axis=1),
                     pltpu.roll(x, MLA_ROPE // 2, axis=1))


def _mla_proj_kernel(*refs, rope):
    x_ref, g_ref, sc_ref, sh_ref, win_ref, qn_ref, kvn_ref, wuq_ref = refs[:8]
    pos = 8
    if rope:
        cos_ref, sin_ref = refs[pos:pos + 2]
        pos += 2
    q_ref, ckv_ref, kpe_ref = refs[pos:pos + 3]

    h = _normmod(x_ref[...], g_ref[...], sc_ref[...], sh_ref[...]).astype(BF16)
    p = jnp.dot(h, win_ref[...], preferred_element_type=F32)
    cq = _rms(p[:, :MLA_Q_RANK], qn_ref[...]).astype(BF16)
    ckv_ref[...] = _rms(p[:, MLA_Q_RANK:MLA_Q_RANK + MLA_KV_RANK], kvn_ref[...])
    kpe = p[:, MLA_Q_RANK + MLA_KV_RANK:]
    q = jnp.dot(cq, wuq_ref[...], preferred_element_type=F32) * (
        (MLA_NOPE + MLA_ROPE) ** -0.5 * float(np.log2(np.e)))
    if rope:
        cos = cos_ref[...]
        sin = sin_ref[...]
        kpe = kpe * cos + _swap_pairs(kpe) * sin
    kpe_ref[...] = kpe
    for hd in range(MLA_HEADS):
        lo = hd * HEAD_W
        q_ref[:, lo:lo + LANES] = q[:, lo:lo + LANES].astype(BF16)
        qp = q[:, lo + LANES:lo + HEAD_W]
        if rope:
            qp = qp * cos + _swap_pairs(qp) * sin
        q_ref[:, lo + LANES:lo + HEAD_W] = qp.astype(BF16)


def _mla_proj(x, mod, norm_g, w_in, q_norm, kv_norm, w_uq, rows_per_cond, cos=None, sin=None):
    n = x.shape[0]
    tm = 512
    rope = cos is not None
    full = lambda i: (0, 0)
    nin = w_in.shape[1]
    in_specs = [
        pl.BlockSpec((tm, D_MODEL), lambda i: (i, 0)),
        pl.BlockSpec((1, D_MODEL), full),
        _mod_spec(1, tm, rows_per_cond, 1),
        _mod_spec(0, tm, rows_per_cond, 1),
        pl.BlockSpec((D_MODEL, nin), full),
        pl.BlockSpec((1, MLA_Q_RANK), full),
        pl.BlockSpec((1, MLA_KV_RANK), full),
        pl.BlockSpec((MLA_Q_RANK, MLA_HEADS * HEAD_W), full),
    ]
    args = [x, norm_g, mod, mod, w_in, q_norm, kv_norm, w_uq]
    if rope:
        nt = cos.shape[0] // tm
        tab = pl.BlockSpec((tm, LANES), lambda i: (i % nt, 0))
        in_specs += [tab, tab]
        args += [cos, sin]
    return pl.pallas_call(
        functools.partial(_mla_proj_kernel, rope=rope),
        out_shape=[jax.ShapeDtypeStruct((n, MLA_HEADS * HEAD_W), BF16),
                   jax.ShapeDtypeStruct((n, MLA_KV_RANK), F32),
                   jax.ShapeDtypeStruct((n, LANES), F32)],
        grid=(n // tm,),
        in_specs=in_specs,
        out_specs=[pl.BlockSpec((tm, MLA_HEADS * HEAD_W), lambda i: (i, 0)),
                   pl.BlockSpec((tm, MLA_KV_RANK), lambda i: (i, 0)),
                   pl.BlockSpec((tm, LANES), lambda i: (i, 0))],
        compiler_params=_params(("arbitrary",)),
        name="mla_proj",
    )(*args)


def _kv_up_kernel(ckv_ref, kpe_ref, wuk_ref, wuv_ref, k_ref, v_ref):
    c = ckv_ref[...].astype(BF16)
    kn = jnp.dot(c, wuk_ref[...], preferred_element_type=F32).astype(BF16)
    vv = jnp.dot(c, wuv_ref[...], preferred_element_type=F32).astype(BF16)
    kpe = kpe_ref[...].astype(BF16)
    lane = lax.broadcasted_iota(jnp.int32, kpe.shape, 1)
    ones_col = jnp.where(lane == 0, 1.0, 0.0).astype(BF16)
    for hd in range(MLA_HEADS):
        k_ref[:, hd * HEAD_W:hd * HEAD_W + LANES] = kn[:, hd * MLA_NOPE:(hd + 1) * MLA_NOPE]
        k_ref[:, hd * HEAD_W + LANES:(hd + 1) * HEAD_W] = kpe
        v_ref[:, hd * HEAD_W:hd * HEAD_W + LANES] = vv[:, hd * MLA_V:(hd + 1) * MLA_V]
        v_ref[:, hd * HEAD_W + LANES:(hd + 1) * HEAD_W] = ones_col


def _kv_up(ckv, kpe, w_uk, w_uv):
    n = ckv.shape[0]
    tk = 512
    full = lambda i: (0, 0)
    return pl.pallas_call(
        _kv_up_kernel,
        out_shape=[jax.ShapeDtypeStruct((n, MLA_HEADS * HEAD_W), BF16),
                   jax.ShapeDtypeStruct((n, MLA_HEADS * HEAD_W), BF16)],
        grid=(n // tk,),
        in_specs=[
            pl.BlockSpec((tk, MLA_KV_RANK), lambda i: (i, 0)),
            pl.BlockSpec((tk, LANES), lambda i: (i, 0)),
            pl.BlockSpec((MLA_KV_RANK, MLA_HEADS * MLA_NOPE), full),
            pl.BlockSpec((MLA_KV_RANK, MLA_HEADS * MLA_V), full),
        ],
        out_specs=[pl.BlockSpec((tk, MLA_HEADS * HEAD_W), lambda i: (i, 0)),
                   pl.BlockSpec((tk, MLA_HEADS * HEAD_W), lambda i: (i, 0))],
        compiler_params=_params(("arbitrary",)),
        name="kv_up",
    )(ckv, kpe, w_uk, w_uv)


def _attn_kernel(q_ref, k_ref, v_ref, o_ref, *, heads):
    for hd in range(heads):
        k = k_ref[:, hd * HEAD_W:(hd + 1) * HEAD_W]
        v = v_ref[:, hd * HEAD_W:(hd + 1) * HEAD_W]
        for r in range(q_ref.shape[0] // ATTN_SUB):
            rows = slice(r * ATTN_SUB, (r + 1) * ATTN_SUB)
            q = q_ref[rows, hd * HEAD_W:(hd + 1) * HEAD_W]
            s = lax.dot_general(q, k, (((1,), (1,)), ((), ())), preferred_element_type=F32)
            m = jnp.max(s, axis=-1, keepdims=True)
            p = jnp.exp2(s - m).astype(BF16)
            o = jnp.dot(p, v, preferred_element_type=F32)
            o_ref[rows, hd * MLA_V:(hd + 1) * MLA_V] = (
                o[:, :MLA_V] / o[:, MLA_V:MLA_V + 1]).astype(o_ref.dtype)


def _attention(q, k, v, batch, tq_total, tk_total, heads_per_step, tq):
    nq = tq_total // tq
    nh = MLA_HEADS // heads_per_step
    hp = heads_per_step
    return pl.pallas_call(
        functools.partial(_attn_kernel, heads=hp),
        out_shape=jax.ShapeDtypeStruct((batch * tq_total, MLA_HEADS * MLA_V), BF16),
        grid=(batch, nh, nq),
        in_specs=[
            pl.BlockSpec((tq, hp * HEAD_W), lambda b, h, i: (b * nq + i, h)),
            pl.BlockSpec((tk_total, hp * HEAD_W), lambda b, h, i: (b, h)),
            pl.BlockSpec((tk_total, hp * HEAD_W), lambda b, h, i: (b, h)),
        ],
        out_specs=pl.BlockSpec((tq, hp * MLA_V), lambda b, h, i: (b * nq + i, h)),
        compiler_params=_params(("arbitrary", "arbitrary", "arbitrary")),
        name="attn",
    )(q, k, v)


_ROPE_PERM = np.concatenate([np.arange(0, 16), np.arange(32, 48), np.arange(16, 32), np.arange(48, 64)])


def _rope_tables(seq):
    pos = jnp.arange(seq)
    row = (pos // GRID_W).astype(F32)
    col = (pos % GRID_W).astype(F32)
    half = MLA_ROPE // 2
    inv = ROPE_BASE ** (-jnp.arange(0, half, 2, dtype=F32) / half)
    ang = jnp.concatenate([row[:, None] * inv, col[:, None] * inv], axis=1)
    cos = jnp.cos(ang)
    sin = jnp.sin(ang)
    pad = jnp.zeros((seq, LANES - MLA_ROPE), F32)
    return (jnp.concatenate([cos, cos, pad], axis=1),
            jnp.concatenate([-sin, sin, pad], axis=1))


def _prep_mla_weights(w_in, w_uq, w_uk, w_uv):
    nq = MLA_Q_RANK + MLA_KV_RANK
    w_in_p = jnp.concatenate(
        [w_in[:, :nq], w_in[:, nq:][:, _ROPE_PERM], jnp.zeros((D_MODEL, LANES - MLA_ROPE), F32)], axis=1)
    wq = w_uq.reshape(MLA_Q_RANK, MLA_HEADS, MLA_NOPE + MLA_ROPE)
    wq_p = jnp.concatenate(
        [wq[:, :, :MLA_NOPE], wq[:, :, MLA_NOPE:][:, :, _ROPE_PERM],
         jnp.zeros((MLA_Q_RANK, MLA_HEADS, HEAD_W - MLA_NOPE - MLA_ROPE), F32)], axis=2)
    return (w_in_p.astype(BF16), wq_p.reshape(MLA_Q_RANK, MLA_HEADS * HEAD_W).astype(BF16),
            w_uk.reshape(MLA_KV_RANK, MLA_HEADS * MLA_NOPE).astype(BF16),
            w_uv.reshape(MLA_KV_RANK, MLA_HEADS * MLA_V).astype(BF16))


def kernel(x_prompt, x_sample, state_ret_fwd, state_ret_bwd, cache_mla_ckv, cache_mla_kpe, c, c_ctx,
           w_ada, b_ada, norm1, norm2, norm_final, ret_w_in, ret_decay, ret_gn, ret_w_out,
           mla_w_in, mla_q_norm, mla_kv_norm, mla_w_uq, mla_w_uk, mla_w_uv, mla_w_o,
           w_router, router_bias, moe_w_gate, moe_w_up, moe_w_down):
    bp, tp, _ = x_prompt.shape
    bs, ts, _ = x_sample.shape
    past = cache_mla_ckv.shape[2]
    xp = x_prompt.reshape(bp * tp, D_MODEL)
    xs = x_sample.reshape(bs * ts, D_MODEL)

    conds = jnp.concatenate([c_ctx[None, :], c, jnp.zeros((8 - 1 - bs, D_MODEL), F32)], axis=0)
    mods = _ada_all(conds, w_ada, b_ada)

    wr_pad = jnp.pad(w_router, ((0, 0), (0, LANES - N_EXPERTS)))
    rb_pad = jnp.pad(router_bias, (0, LANES - N_EXPERTS)).reshape(1, LANES)
    nf = norm_final.reshape(1, D_MODEL)
    cos_t, sin_t = _rope_tables(ts)

    groups = {
        "p": dict(x=xp, rows_per_cond=bp * tp, lo=0, hi=1),
        "s": dict(x=xs, rows_per_cond=ts, lo=1, hi=1 + bs),
    }
    out_ckv, out_kpe = [], []
    ret_states = None
    n_ret = state_ret_fwd.shape[1]
    wg = moe_w_gate.astype(BF16)
    wu = moe_w_up.astype(BF16)
    wd = moe_w_down.astype(BF16)

    for i in range(DEPTH):
        j = i // 2
        n1 = norm1[i].reshape(1, D_MODEL)
        n2 = norm2[i].reshape(1, D_MODEL)
        mixed = {}
        if i % 2 == 0:
            w_in = ret_w_in[j].astype(BF16)
            w_out = ret_w_out[j].astype(BF16)
            dec = jnp.broadcast_to(ret_decay[j][:, :, None, None], (2, RET_HEADS, SCAN_CHUNK, SCAN_CHUNK))
            gn = ret_gn[j]
            for name, gr in groups.items():
                mod = mods[i, gr["lo"]:gr["hi"]].reshape(-1, 1, 6 * D_MODEL)
                proj = _proj(gr["x"], mod, n1, w_in, gr["rows_per_cond"],
                             2 * RET_HEADS * RET_DK + RET_HEADS * RET_DV)
                if name == "p":
                    y, *ret_states = _retention_scan(proj, dec, gn, bp, tp, state_slot=j,
                                                     state_bufs=ret_states, n_slots=n_ret)
                else:
                    (y,) = _retention_scan(proj, dec, gn, bs, ts, state_ret_fwd, state_ret_bwd, s0_slot=j)
                mixed[name] = (y, w_out, mod)
        else:
            w_in_p, w_uq_p, w_uk, w_uv = _prep_mla_weights(mla_w_in[j], mla_w_uq[j], mla_w_uk[j], mla_w_uv[j])
            w_o = mla_w_o[j].astype(BF16)
            qn = mla_q_norm[j].reshape(1, MLA_Q_RANK)
            kvn = mla_kv_norm[j].reshape(1, MLA_KV_RANK)
            for name, gr in groups.items():
                mod = mods[i, gr["lo"]:gr["hi"]].reshape(-1, 1, 6 * D_MODEL)
                if name == "p":
                    q, ckv, kpe = _mla_proj(gr["x"], mod, n1, w_in_p, qn, kvn, w_uq_p, gr["rows_per_cond"])
                    out_ckv.append(ckv.reshape(bp, tp, MLA_KV_RANK))
                    out_kpe.append(kpe[:, :MLA_ROPE][:, _ROPE_PERM].reshape(bp, tp, MLA_ROPE))
                    k, v = _kv_up(ckv, kpe, w_uk, w_uv)
                    o = _attention(q, k, v, bp, tp, tp, MLA_HEADS, tp)
                else:
                    q, ckv, kpe = _mla_proj(gr["x"], mod, n1, w_in_p, qn, kvn, w_uq_p, gr["rows_per_cond"],
                                            cos_t, sin_t)
                    cache_kpe = jnp.pad(cache_mla_kpe[:, j][:, :, _ROPE_PERM],
                                        ((0, 0), (0, 0), (0, LANES - MLA_ROPE)))
                    ckv_all = jnp.concatenate([ckv.reshape(bs, ts, MLA_KV_RANK), cache_mla_ckv[:, j]], axis=1)
                    kpe_all = jnp.concatenate([kpe.reshape(bs, ts, LANES), cache_kpe], axis=1)
                    tk = ts + past
                    k, v = _kv_up(ckv_all.reshape(bs * tk, MLA_KV_RANK), kpe_all.reshape(bs * tk, LANES),
                                  w_uk, w_uv)
                    o = _attention(q, k, v, bs, ts, tk, 1, 2048)
                mixed[name] = (o, w_o, mod)

        for name, gr in groups.items():
            y, w_o, mod = mixed[name]
            x_new, hx = _mix_out(y, w_o, gr["x"], mod, n2, wr_pad, rb_pad, gr["rows_per_cond"])
            gr["x"] = _routed_moe(x_new, hx, mod, wg, wu, wd, i, nf, gr["rows_per_cond"], i == DEPTH - 1)

    return (groups["p"]["x"].reshape(bp, tp, D_MODEL), groups["s"]["x"].reshape(bs, ts, D_MODEL),
            ret_states[0], ret_states[1],
            jnp.stack(out_ckv, axis=1), jnp.stack(out_kpe, axis=1))
```

```python
import functools

import numpy as np
import jax
import jax.numpy as jnp
from jax import lax
from jax.experimental import pallas as pl
from jax.experimental.pallas import tpu as pltpu

F32 = jnp.float32
BF16 = jnp.bfloat16

D_MODEL = 1024
DEPTH = 4
GRID_W = 64
RET_HEADS = 4
RET_DK = 256
RET_DV = 512
SCAN_CHUNK = 256
MLA_HEADS = 8
MLA_Q_RANK = 512
MLA_KV_RANK = 256
MLA_NOPE = 128
MLA_ROPE = 64
MLA_V = 128
ROPE_BASE = 10000.0
N_EXPERTS = 16
N_GROUPS = 4
EXPERTS_PER_GROUP = N_EXPERTS // N_GROUPS
EXPERT_FF = 256
EPS = 1e-6

LANES = 128
HEAD_W = 2 * LANES
VMEM_LIMIT = 48 * 1024 * 1024
MIX_SUB = 256
ATTN_SUB = 256
MOE_EG = 4
PAIRS_PER_GROUP = EXPERTS_PER_GROUP * (EXPERTS_PER_GROUP - 1) // 2
N_BUCKETS = N_GROUPS * PAIRS_PER_GROUP
HX_W = D_MODEL + LANES
MOE_TILE = 256
PLAN_BLK = 256
PUSH_ROWS = 1024
PULL_ROWS = 512

_PAIRS = [(a, b) for a in range(EXPERTS_PER_GROUP) for b in range(a + 1, EXPERTS_PER_GROUP)]
_BUCKET_LO = np.array([g * EXPERTS_PER_GROUP + a for g in range(N_GROUPS) for a, _ in _PAIRS], np.int32)
_BUCKET_HI = np.array([g * EXPERTS_PER_GROUP + b for g in range(N_GROUPS) for _, b in _PAIRS], np.int32)


def _params(sem):
    return pltpu.CompilerParams(dimension_semantics=sem, vmem_limit_bytes=VMEM_LIMIT)


def _sigmoid(x):
    return 1.0 / (1.0 + jnp.exp(-x))


def _normmod(x, g, sc, sh):
    y = x * lax.rsqrt(jnp.mean(x * x, axis=-1, keepdims=True) + EPS)
    return (y * g) * (1.0 + sc) + sh


def _rms(x, g):
    return x * lax.rsqrt(jnp.mean(x * x, axis=-1, keepdims=True) + EPS) * g


def _ada_kernel(c_ref, w_ref, b_ref, o_ref):
    c = c_ref[...]
    s = (c * _sigmoid(c)).astype(BF16)
    o_ref[0] = jnp.dot(s, w_ref[0].astype(BF16), preferred_element_type=F32) + b_ref[0]


def _ada_all(conds, w_ada, b_ada):
    tn = 1536
    n6 = 6 * D_MODEL
    return pl.pallas_call(
        _ada_kernel,
        out_shape=jax.ShapeDtypeStruct((DEPTH, 8, n6), F32),
        grid=(DEPTH, n6 // tn),
        in_specs=[
            pl.BlockSpec((8, D_MODEL), lambda l, j: (0, 0)),
            pl.BlockSpec((1, D_MODEL, tn), lambda l, j: (l, 0, j)),
            pl.BlockSpec((1, 1, tn), lambda l, j: (l, 0, j)),
        ],
        out_specs=pl.BlockSpec((1, 8, tn), lambda l, j: (l, 0, j)),
        compiler_params=_params(("arbitrary", "arbitrary")),
        name="ada",
    )(conds, w_ada, b_ada.reshape(DEPTH, 1, n6))


def _mod_spec(which, tm, rows_per_cond, ngrid):
    if ngrid == 1:
        return pl.BlockSpec((None, 1, D_MODEL), lambda i: (i * tm // rows_per_cond, 0, which))
    return pl.BlockSpec((None, 1, D_MODEL), lambda i, j: (i * tm // rows_per_cond, 0, which))


def _silu(g):
    hg = 0.5 * g
    return hg + hg * jnp.tanh(hg)


def _proj_kernel(x_ref, g_ref, sc_ref, sh_ref, w_ref, o_ref, h_ref, *, silu_from):
    j = pl.program_id(1)

    @pl.when(j == 0)
    def _():
        h_ref[...] = _normmod(x_ref[...], g_ref[...], sc_ref[...], sh_ref[...]).astype(BF16)

    acc = jnp.dot(h_ref[...], w_ref[...], preferred_element_type=F32)
    o_ref[...] = jnp.where(j >= silu_from, _silu(acc), acc).astype(o_ref.dtype)


def _proj(x, mod, norm_g, w, rows_per_cond, silu_from_col):
    n = x.shape[0]
    nout = w.shape[1]
    tm, tn = 1024, 2048
    return pl.pallas_call(
        functools.partial(_proj_kernel, silu_from=silu_from_col // tn),
        out_shape=jax.ShapeDtypeStruct((n, nout), BF16),
        grid=(n // tm, nout // tn),
        in_specs=[
            pl.BlockSpec((tm, D_MODEL), lambda i, j: (i, 0)),
            pl.BlockSpec((1, D_MODEL), lambda i, j: (0, 0)),
            _mod_spec(1, tm, rows_per_cond, 2),
            _mod_spec(0, tm, rows_per_cond, 2),
            pl.BlockSpec((D_MODEL, tn), lambda i, j: (0, j)),
        ],
        out_specs=pl.BlockSpec((tm, tn), lambda i, j: (i, j)),
        scratch_shapes=[pltpu.VMEM((tm, D_MODEL), BF16)],
        compiler_params=_params(("arbitrary", "arbitrary")),
        name="ret_proj",
    )(x, norm_g, mod, mod, w)


def _scan_kernel(*refs, seq, has_s0, aliased):
    q_ref, k_ref, v_ref, gf_ref, gb_ref, dec_ref, gn_ref = refs[:7]
    pos = 7
    if has_s0:
        s0_refs = refs[pos:pos + 2]
        pos += 2
    if aliased:
        pos += 2
    y_ref = refs[pos]
    s_refs = refs[pos + 1:pos + 3]
    intra_ref, qdec_ref, kdec_ref, cdec_ref, yacc_ref = refs[pos + 3:pos + 8]

    C = SCAN_CHUNK
    nc = seq // C

    @pl.when(pl.program_id(1) == 0)
    def _():
        ri = lax.broadcasted_iota(jnp.int32, (C, C), 0).astype(F32)
        ci = lax.broadcasted_iota(jnp.int32, (C, C), 1).astype(F32)
        k_scale = RET_DK ** -0.5
        for d in range(2):
            lg = -jnp.exp(dec_ref[d])
            if d == 1:
                diff, q_pow, k_pow = ci - ri, C - ri, ri
            else:
                diff, q_pow, k_pow = ri - ci, ri + 1.0, C - 1.0 - ri
            intra_ref[d] = jnp.where(diff >= 0, jnp.exp(lg * jnp.maximum(diff, 0.0)), 0.0) * k_scale
            qdec_ref[d] = jnp.exp(lg * q_pow)
            kdec_ref[d] = jnp.exp(lg * k_pow) * k_scale
            cdec_ref[d] = jnp.exp(lg[0:8, :] * float(C))

    state_is_zero = not has_s0 and nc == 1
    for d in range(2):
        if has_s0:
            s_refs[d][...] = s0_refs[d][...]
        elif not state_is_zero:
            s_refs[d][...] = jnp.zeros_like(s_refs[d])

    def lanes(a, width):
        return jnp.concatenate([a] * (width // C), axis=1)

    def contribution(rows, d, att):
        g_ref = (gf_ref, gb_ref)[d]
        s_ref = s_refs[d]
        q = q_ref[rows, :]
        k = k_ref[rows, :]
        v = v_ref[rows, :]
        att = (att * intra_ref[d]).astype(BF16)
        o = jnp.dot(att, v, preferred_element_type=F32)
        kd = (k.astype(F32) * lanes(kdec_ref[d], RET_DK)).astype(BF16)
        s_new = lax.dot_general(kd, v, (((0,), (0,)), ((), ())), preferred_element_type=F32)
        if state_is_zero:
            s_ref[...] = s_new
        else:
            qd = (q.astype(F32) * lanes(qdec_ref[d], RET_DK)).astype(BF16)
            s_old = s_ref[...]
            o = o + jnp.dot(qd, s_old.astype(BF16), preferred_element_type=F32)
            s_ref[...] = s_old * lanes(cdec_ref[d][0:1, :], RET_DV) + s_new
        mu = jnp.mean(o, axis=-1, keepdims=True)
        dlt = o - mu
        var = jnp.mean(dlt * dlt, axis=-1, keepdims=True)
        yn = (dlt * lax.rsqrt(var + EPS)) * gn_ref[d:d + 1, :]
        return g_ref[rows, :].astype(F32) * yn

    def scores(rows):
        return lax.dot_general(q_ref[rows, :], k_ref[rows, :], (((1,), (1,)), ((), ())),
                               preferred_element_type=F32)

    if nc == 1:
        rows = pl.ds(0, C)
        att = scores(rows)
        y_ref[...] = (contribution(rows, 0, att) + contribution(rows, 1, att)).astype(y_ref.dtype)
    else:
        def pair(i, second_visit):
            for d, c in ((0, i), (1, nc - 1 - i)):
                rows = pl.ds(pl.multiple_of(c * C, C), C)
                contrib = contribution(rows, d, scores(rows))
                if second_visit:
                    y_ref[rows, :] = (yacc_ref[rows, :] + contrib).astype(y_ref.dtype)
                else:
                    yacc_ref[rows, :] = contrib

        def first(i, carry):
            pair(i, False)
            return carry

        def second(i, carry):
            pair(i, True)
            return carry

        lax.fori_loop(0, nc // 2, first, 0)
        lax.fori_loop(nc // 2, nc, second, 0)


def _retention_scan(proj, dec, gn, batch, seq, s0f=None, s0b=None, s0_slot=0, state_slot=None,
                    state_bufs=None, n_slots=1):
    has_s0 = s0f is not None
    want_state = state_slot is not None
    aliased = state_bufs is not None
    hv = RET_HEADS * RET_DV
    nkb = RET_HEADS
    nvb = 2 * RET_HEADS * RET_DK // RET_DV
    C = SCAN_CHUNK
    assert seq % C == 0 and (seq == C or (seq // C) % 2 == 0)
    in_specs = [
        pl.BlockSpec((seq, RET_DK), lambda h, b: (b, h)),
        pl.BlockSpec((seq, RET_DK), lambda h, b: (b, nkb + h)),
        pl.BlockSpec((seq, RET_DV), lambda h, b: (b, nvb + h)),
        pl.BlockSpec((seq, RET_DV), lambda h, b: (b, nvb + RET_HEADS + h)),
        pl.BlockSpec((seq, RET_DV), lambda h, b: (b, nvb + 2 * RET_HEADS + h)),
        pl.BlockSpec((2, None, C, C), lambda h, b: (0, h, 0, 0)),
        pl.BlockSpec((2, RET_DV), lambda h, b: (0, h)),
    ]
    args = [proj, proj, proj, proj, proj, dec, gn]
    if has_s0:
        st_spec = pl.BlockSpec((None, None, None, RET_DK, RET_DV), lambda h, b: (b, s0_slot, h, 0, 0))
        in_specs += [st_spec, st_spec]
        args += [s0f, s0b]
    aliases = {}
    if aliased:
        in_specs += [pl.BlockSpec(memory_space=pl.ANY)] * 2
        aliases = {len(args): 1, len(args) + 1: 2}
        args += list(state_bufs)
    out_shape = [jax.ShapeDtypeStruct((batch * seq, hv), BF16)]
    out_specs = [pl.BlockSpec((seq, RET_DV), lambda h, b: (b, h))]
    scratch = [pltpu.VMEM((2, C, C), F32), pltpu.VMEM((2, C, C), F32), pltpu.VMEM((2, C, C), F32),
               pltpu.VMEM((2, 8, C), F32), pltpu.VMEM((seq, RET_DV), F32)]
    if want_state:
        st = jax.ShapeDtypeStruct((batch, n_slots, RET_HEADS, RET_DK, RET_DV), F32)
        out_shape += [st, st]
        st_spec = pl.BlockSpec((None, None, None, RET_DK, RET_DV), lambda h, b: (b, state_slot, h, 0, 0))
        out_specs += [st_spec, st_spec]
    else:
        scratch = [pltpu.VMEM((RET_DK, RET_DV), F32)] * 2 + scratch
    return pl.pallas_call(
        functools.partial(_scan_kernel, seq=seq, has_s0=has_s0, aliased=aliased),
        out_shape=out_shape,
        grid=(RET_HEADS, batch),
        in_specs=in_specs,
        out_specs=out_specs,
        scratch_shapes=scratch,
        input_output_aliases=aliases,
        compiler_params=_params(("arbitrary", "arbitrary")),
        name="ret_scan",
    )(*args)


def _route(logits, bias):
    s = _sigmoid(logits)
    sb = s + bias
    lane = lax.broadcasted_iota(jnp.int32, sb.shape, 1)
    lane_f = lane.astype(F32)
    neg = -jnp.inf
    big = float(LANES)

    def top2(vals):
        m1 = jnp.max(vals, axis=-1, keepdims=True)
        i1 = jnp.min(jnp.where(vals == m1, lane_f, big), axis=-1, keepdims=True)
        vals2 = jnp.where(lane_f == i1, neg, vals)
        m2 = jnp.max(vals2, axis=-1, keepdims=True)
        i2 = jnp.min(jnp.where(vals2 == m2, lane_f, big), axis=-1, keepdims=True)
        return m1 + m2, i1, i2

    best = None
    for g in range(N_GROUPS):
        in_group = (lane // EXPERTS_PER_GROUP) == g
        gs, i1, i2 = top2(jnp.where(in_group, sb, neg))
        if best is None:
            best = (gs, i1, i2)
        else:
            take = gs > best[0]
            best = (jnp.where(take, gs, best[0]), jnp.where(take, i1, best[1]),
                    jnp.where(take, i2, best[2]))
    _, e1, e2 = best
    w1 = jnp.sum(jnp.where(lane_f == e1, s, 0.0), axis=-1, keepdims=True)
    w2 = jnp.sum(jnp.where(lane_f == e2, s, 0.0), axis=-1, keepdims=True)
    den = w1 + w2
    grp = jnp.floor(e1 * (1.0 / EXPERTS_PER_GROUP))
    first_is_lo = e1 < e2
    lo = jnp.where(first_is_lo, e1, e2) - grp * EXPERTS_PER_GROUP
    hi = jnp.where(first_is_lo, e2, e1) - grp * EXPERTS_PER_GROUP
    bucket = grp * PAIRS_PER_GROUP + 3.0 * lo - 0.5 * lo * (lo - 1.0) + hi - lo - 1.0
    w_lo = jnp.where(first_is_lo, w1, w2) / den
    w_hi = jnp.where(first_is_lo, w2, w1) / den
    return (jnp.where(lane == 0, w_lo, 0.0) + jnp.where(lane == 1, w_hi, 0.0)
            + jnp.where(lane == 2, bucket, 0.0))


def _split_bf16(a):
    hi = a.astype(BF16)
    lo = (a - hi.astype(F32)).astype(BF16)
    return hi, lo


def _mix_out_kernel(y_ref, w_ref, x_ref, g1_ref, n2_ref, sc_ref, sh_ref, wr_ref, rb_ref,
                    xo_ref, h_ref, info_ref):
    w_hi, w_lo = _split_bf16(wr_ref[...])
    w_cat = jnp.concatenate([w_hi, w_lo], axis=1)
    for r in range(y_ref.shape[0] // MIX_SUB):
        rows = slice(r * MIX_SUB, (r + 1) * MIX_SUB)
        x = x_ref[rows, :] + g1_ref[...] * jnp.dot(y_ref[rows, :], w_ref[...], preferred_element_type=F32)
        xo_ref[rows, :] = x
        h = _normmod(x, n2_ref[...], sc_ref[...], sh_ref[...])
        h_ref[rows, :] = h
        h_hi, h_lo = _split_bf16(h)
        both = jnp.dot(h_hi, w_cat, preferred_element_type=F32)
        logits = both[:, :LANES] + both[:, LANES:] + jnp.dot(h_lo, w_hi, preferred_element_type=F32)
        info_ref[rows, :] = _route(logits, rb_ref[...])


def _mix_out(y, w, x, mod, norm2_g, wr_pad, rb_pad, rows_per_cond):
    n, kdim = y.shape
    tm = 512
    full = lambda i: (0, 0)
    return pl.pallas_call(
        _mix_out_kernel,
        out_shape=[jax.ShapeDtypeStruct((n, D_MODEL), F32),
                   jax.ShapeDtypeStruct((n, D_MODEL), F32),
                   jax.ShapeDtypeStruct((n, LANES), F32)],
        grid=(n // tm,),
        in_specs=[
            pl.BlockSpec((tm, kdim), lambda i: (i, 0)),
            pl.BlockSpec((kdim, D_MODEL), full),
            pl.BlockSpec((tm, D_MODEL), lambda i: (i, 0)),
            _mod_spec(2, tm, rows_per_cond, 1),
            pl.BlockSpec((1, D_MODEL), full),
            _mod_spec(4, tm, rows_per_cond, 1),
            _mod_spec(3, tm, rows_per_cond, 1),
            pl.BlockSpec((D_MODEL, LANES), full),
            pl.BlockSpec((1, LANES), full),
        ],
        out_specs=[pl.BlockSpec((tm, D_MODEL), lambda i: (i, 0)),
                   pl.BlockSpec((tm, D_MODEL), lambda i: (i, 0)),
                   pl.BlockSpec((tm, LANES), lambda i: (i, 0))],
        compiler_params=_params(("arbitrary",)),
        name="mix_out",
    )(y, w, x, mod, norm2_g, mod, mod, wr_pad, rb_pad)


def _moe_kernel(x_ref, h_ref, gate_ref, g2_ref, wg_ref, wu_ref, wd_ref, nf_ref, o_ref, acc_ref,
                *, final_norm):
    step = pl.program_id(1)

    @pl.when(step == 0)
    def _():
        acc_ref[...] = jnp.zeros_like(acc_ref)

    h = h_ref[...]
    gate = gate_ref[...]
    lane = lax.broadcasted_iota(jnp.int32, gate.shape, 1)
    acts = []
    for k in range(MOE_EG):
        a = jnp.dot(h, wg_ref[k], preferred_element_type=F32)
        u = jnp.dot(h, wu_ref[k], preferred_element_type=F32)
        ge = jnp.sum(jnp.where(lane == step * MOE_EG + k, gate, 0.0), axis=-1, keepdims=True)
        acts.append((_silu(a) * u * ge).astype(BF16))
    act = jnp.concatenate(acts, axis=1)
    acc_ref[...] += jnp.dot(act, wd_ref[...].reshape(MOE_EG * EXPERT_FF, D_MODEL),
                            preferred_element_type=F32)

    @pl.when(step == N_EXPERTS // MOE_EG - 1)
    def _():
        x = x_ref[...] + g2_ref[...] * acc_ref[...]
        if final_norm:
            x = _rms(x, nf_ref[...])
        o_ref[...] = x


def _moe(x, h, gate, mod, wg, wu, wd, layer, norm_f, rows_per_cond, final_norm):
    n = x.shape[0]
    tm = 1024
    return pl.pallas_call(
        functools.partial(_moe_kernel, final_norm=final_norm),
        out_shape=jax.ShapeDtypeStruct((n, D_MODEL), F32),
        grid=(n // tm, N_EXPERTS // MOE_EG),
        in_specs=[
            pl.BlockSpec((tm, D_MODEL), lambda i, e: (i, 0)),
            pl.BlockSpec((tm, D_MODEL), lambda i, e: (i, 0)),
            pl.BlockSpec((tm, LANES), lambda i, e: (i, 0)),
            _mod_spec(5, tm, rows_per_cond, 2),
            pl.BlockSpec((None, MOE_EG, D_MODEL, EXPERT_FF), lambda i, e: (layer, e, 0, 0)),
            pl.BlockSpec((None, MOE_EG, D_MODEL, EXPERT_FF), lambda i, e: (layer, e, 0, 0)),
            pl.BlockSpec((None, MOE_EG, EXPERT_FF, D_MODEL), lambda i, e: (layer, e, 0, 0)),
            pl.BlockSpec((1, D_MODEL), lambda i, e: (0, 0)),
        ],
        out_specs=pl.BlockSpec((tm, D_MODEL), lambda i, e: (i, 0)),
        scratch_shapes=[pltpu.VMEM((tm, D_MODEL), F32)],
        compiler_params=_params(("arbitrary", "arbitrary")),
        name="moe",
    )(x, h, gate, mod, wg, wu, wd, norm_f)


def _plan_kernel(bk_ref, rank_ref, cnt_ref, carry_ref):
    i = pl.program_id(0)

    @pl.when(i == 0)
    def _():
        carry_ref[...] = jnp.zeros_like(carry_ref)

    b = bk_ref[0]
    sub = lax.broadcasted_iota(jnp.int32, (LANES, PLAN_BLK), 0)
    onehot = jnp.where(sub == b, 1.0, 0.0)
    ti = lax.broadcasted_iota(jnp.int32, (PLAN_BLK, PLAN_BLK), 0)
    tj = lax.broadcasted_iota(jnp.int32, (PLAN_BLK, PLAN_BLK), 1)
    upper = jnp.where(ti <= tj, 1.0, 0.0).astype(BF16)
    running = jnp.dot(onehot.astype(BF16), upper, preferred_element_type=F32) + carry_ref[...]
    rank_ref[0] = (jnp.sum(onehot * running, axis=0, keepdims=True) - 1.0).astype(jnp.int32)
    carry_ref[...] += jnp.sum(onehot, axis=1, keepdims=True)
    cnt_ref[...] = jnp.broadcast_to(carry_ref[...], cnt_ref.shape)


def _plan(bk):
    n = bk.shape[0]
    nb = n // PLAN_BLK
    rank, cnt = pl.pallas_call(
        _plan_kernel,
        out_shape=[jax.ShapeDtypeStruct((nb, 1, PLAN_BLK), jnp.int32),
                   jax.ShapeDtypeStruct((LANES, LANES), F32)],
        grid=(nb,),
        in_specs=[pl.BlockSpec((1, 1, PLAN_BLK), lambda i: (i, 0, 0))],
        out_specs=[pl.BlockSpec((1, 1, PLAN_BLK), lambda i: (i, 0, 0)),
                   pl.BlockSpec((LANES, LANES), lambda i: (0, 0))],
        scratch_shapes=[pltpu.VMEM((LANES, 1), F32)],
        compiler_params=_params(("arbitrary",)),
        name="moe_plan",
    )(bk.reshape(nb, 1, PLAN_BLK))
    return rank.reshape(n), cnt[:N_BUCKETS, 0].astype(jnp.int32)


def _row_copy(src_ref, src_row, dst_ref, dst_row, sem):
    return pltpu.make_async_copy(src_ref.at[pl.ds(src_row, 1), :], dst_ref.at[pl.ds(dst_row, 1), :], sem)


def _invert_kernel(pos_ref, src_ref):
    def clear(s, carry):
        src_ref[s] = 0
        return carry

    def put(t, carry):
        src_ref[pos_ref[t]] = t
        return carry

    lax.fori_loop(0, src_ref.shape[0], clear, 0, unroll=8)
    lax.fori_loop(0, pos_ref.shape[0], put, 0, unroll=8)


def _invert(pos, n_sorted):
    return pl.pallas_call(
        _invert_kernel,
        out_shape=jax.ShapeDtypeStruct((n_sorted,), jnp.int32),
        grid_spec=pltpu.PrefetchScalarGridSpec(
            num_scalar_prefetch=1,
            grid=(1,),
            in_specs=[],
            out_specs=pl.BlockSpec(memory_space=pltpu.SMEM),
        ),
        compiler_params=_params(("arbitrary",)),
        name="moe_invert",
    )(pos)


def _experts_kernel(src_ref, elo_ref, ehi_ref, act_ref, h_hbm, wg_hbm, wu_hbm, wd_hbm, wrc_ref, wrh_ref,
                    y_ref, wg_ref, wu_ref, wd_ref, buf_ref, wsem, sems, *, layer):
    i = pl.program_id(0)

    def gather(step, slot, wait):
        for r in range(MOE_TILE):
            cp = _row_copy(h_hbm, src_ref[step * MOE_TILE + r], buf_ref.at[slot], r, sems.at[slot])
            if wait:
                cp.wait()
            else:
                cp.start(priority=r % 2)

    @pl.when(i == 0)
    def _():
        gather(0, 0, False)
        loads = [pltpu.make_async_copy(src.at[layer], dst, wsem.at[k])
                 for k, (src, dst) in enumerate(((wg_hbm, wg_ref), (wu_hbm, wu_ref), (wd_hbm, wd_ref)))]
        for cp in loads:
            cp.start()
        for cp in loads:
            cp.wait()

    for slot in range(2):
        @pl.when(jnp.logical_and(act_ref[i] == 1, i % 2 == slot))
        def _():
            gather(i, slot, True)
            gather(i + 1, 1 - slot, False)
            _expert_pair(buf_ref[slot], elo_ref[i], ehi_ref[i], wg_ref, wu_ref, wd_ref, wrc_ref, wrh_ref, y_ref)

        @pl.when(jnp.logical_and(jnp.logical_and(act_ref[i] == 0, i % 2 == slot),
                                 act_ref[jnp.maximum(i - 1, 0)] == 1))
        def _():
            gather(i, slot, True)


def _expert_pair(h, e_lo, e_hi, wg_ref, wu_ref, wd_ref, wrc_ref, wrh_ref, y_ref):
    h_hi, h_lo = _split_bf16(h)
    both = jnp.dot(h_hi, wrc_ref[...], preferred_element_type=F32)
    logits = both[:, :LANES] + both[:, LANES:] + jnp.dot(h_lo, wrh_ref[...], preferred_element_type=F32)
    s = _sigmoid(logits)
    lane = lax.broadcasted_iota(jnp.int32, s.shape, 1)
    s_lo = jnp.sum(jnp.where(lane == e_lo, s, 0.0), axis=-1, keepdims=True)
    s_hi = jnp.sum(jnp.where(lane == e_hi, s, 0.0), axis=-1, keepdims=True)
    den = s_lo + s_hi
    y = None
    for e, w in ((e_lo, s_lo / den), (e_hi, s_hi / den)):
        a = jnp.dot(h_hi, wg_ref[e], preferred_element_type=F32)
        u = jnp.dot(h_hi, wu_ref[e], preferred_element_type=F32)
        act = (_silu(a) * u * w).astype(BF16)
        part = jnp.dot(act, wd_ref[e], preferred_element_type=F32)
        y = part if y is None else y + part
    y_ref[...] = y


def _experts(src, tile_blk, tile_elo, tile_ehi, tile_act, h, wg, wu, wd, wr_cat, wr_hi, layer, n_sorted):
    row = lambda i, src, elo, ehi, act, blk: (blk[i], 0)
    full = lambda i, src, elo, ehi, act, blk: (0, 0)
    hbm = pl.BlockSpec(memory_space=pl.ANY)

    def kern(src_ref, elo_ref, ehi_ref, act_ref, blk_ref, *refs):
        _experts_kernel(src_ref, elo_ref, ehi_ref, act_ref, *refs, layer=layer)

    return pl.pallas_call(
        kern,
        out_shape=jax.ShapeDtypeStruct((n_sorted, D_MODEL), F32),
        grid_spec=pltpu.PrefetchScalarGridSpec(
            num_scalar_prefetch=5,
            grid=(n_sorted // MOE_TILE + 1,),
            in_specs=[hbm, hbm, hbm, hbm,
                      pl.BlockSpec((D_MODEL, 2 * LANES), full), pl.BlockSpec((D_MODEL, LANES), full)],
            out_specs=pl.BlockSpec((MOE_TILE, D_MODEL), row),
            scratch_shapes=[pltpu.VMEM((N_EXPERTS, D_MODEL, EXPERT_FF), BF16),
                            pltpu.VMEM((N_EXPERTS, D_MODEL, EXPERT_FF), BF16),
                            pltpu.VMEM((N_EXPERTS, EXPERT_FF, D_MODEL), BF16),
                            pltpu.VMEM((2, MOE_TILE, D_MODEL), F32),
                            pltpu.SemaphoreType.DMA((3,)),
                            pltpu.SemaphoreType.DMA((2,))],
        ),
        compiler_params=_params(("arbitrary",)),
        name="moe_experts",
    )(src, tile_elo, tile_ehi, tile_act, tile_blk, h, wg, wu, wd, wr_cat, wr_hi)


def _pull_kernel(pos_ref, y_ref, x_ref, g2_ref, nf_ref, o_ref, buf_ref, sems, *, final_norm):
    i = pl.program_id(0)
    nsteps = pl.num_programs(0)

    def fetch(step, slot, wait):
        for r in range(PULL_ROWS):
            cp = _row_copy(y_ref, pos_ref[step * PULL_ROWS + r], buf_ref.at[slot], r, sems.at[slot])
            if wait:
                cp.wait()
            else:
                cp.start(priority=r % 2)

    @pl.when(i == 0)
    def _():
        fetch(0, 0, False)

    for slot in range(2):
        @pl.when(i % 2 == slot)
        def _():
            @pl.when(i + 1 < nsteps)
            def _():
                fetch(i + 1, 1 - slot, False)

            fetch(i, slot, True)
            x = x_ref[...] + g2_ref[...] * buf_ref[slot]
            if final_norm:
                x = _rms(x, nf_ref[...])
            o_ref[...] = x


def _pull(pos, y_sorted, x, mod, norm_f, rows_per_cond, final_norm):
    n = x.shape[0]
    tm = PULL_ROWS
    return pl.pallas_call(
        functools.partial(_pull_kernel, final_norm=final_norm),
        out_shape=jax.ShapeDtypeStruct((n, D_MODEL), F32),
        grid_spec=pltpu.PrefetchScalarGridSpec(
            num_scalar_prefetch=1,
            grid=(n // tm,),
            in_specs=[pl.BlockSpec(memory_space=pl.ANY),
                      pl.BlockSpec((tm, D_MODEL), lambda i, pos: (i, 0)),
                      pl.BlockSpec((None, 1, D_MODEL), lambda i, pos: (i * tm // rows_per_cond, 0, 5)),
                      pl.BlockSpec((1, D_MODEL), lambda i, pos: (0, 0))],
            out_specs=pl.BlockSpec((tm, D_MODEL), lambda i, pos: (i, 0)),
            scratch_shapes=[pltpu.VMEM((2, tm, D_MODEL), F32), pltpu.SemaphoreType.DMA((2,))],
        ),
        compiler_params=_params(("arbitrary",)),
        name="moe_pull",
    )(pos, y_sorted, x, mod, norm_f)


def _routed_moe(x, h, info, mod, wg, wu, wd, wr_cat, wr_hi, layer, norm_f, rows_per_cond, final_norm):
    n = x.shape[0]
    n_sorted = n + N_BUCKETS * MOE_TILE
    n_tiles = n_sorted // MOE_TILE + 1
    bk = info[:, 2].astype(jnp.int32)
    rank, counts = _plan(bk)
    padded = (counts + MOE_TILE - 1) // MOE_TILE * MOE_TILE
    ends = jnp.cumsum(padded)
    starts = ends - padded
    onehot = bk[:, None] == jnp.arange(N_BUCKETS, dtype=jnp.int32)[None, :]
    pos = jnp.sum(jnp.where(onehot, starts[None, :], 0), axis=1) + rank
    n_active = ends[-1] // MOE_TILE
    tile = jnp.arange(n_tiles, dtype=jnp.int32)
    tile_act = (tile < n_active).astype(jnp.int32)
    tile_blk = jnp.minimum(tile, n_active - 1)
    tile_bucket = jnp.minimum(jnp.sum(ends[None, :] <= (tile_blk * MOE_TILE)[:, None], axis=1), N_BUCKETS - 1)
    src = _invert(pos, n_tiles * MOE_TILE)
    y_sorted = _experts(src, tile_blk, jnp.asarray(_BUCKET_LO)[tile_bucket], jnp.asarray(_BUCKET_HI)[tile_bucket],
                        tile_act, h, wg, wu, wd, wr_cat, wr_hi, layer, n_sorted)
    return _pull(pos, y_sorted, x, mod, norm_f, rows_per_cond, final_norm)


def _swap_pairs(x):
    lane = lax.broadcasted_iota(jnp.int32, x.shape, 1)
    first = (lane % MLA_ROPE) < (MLA_ROPE // 2)
    return jnp.where(first, pltpu.roll(x, LANES - MLA_ROPE // 2, axis=1),
                     pltpu.roll(x, MLA_ROPE // 2, axis=1))


def _mla_proj_kernel(*refs, rope):
    x_ref, g_ref, sc_ref, sh_ref, win_ref, qn_ref, kvn_ref, wuq_ref = refs[:8]
    pos = 8
    if rope:
        cos_ref, sin_ref = refs[pos:pos + 2]
        pos += 2
    q_ref, ckv_ref, kpe_ref = refs[pos:pos + 3]

    h = _normmod(x_ref[...], g_ref[...], sc_ref[...], sh_ref[...]).astype(BF16)
    p = jnp.dot(h, win_ref[...], preferred_element_type=F32)
    cq = _rms(p[:, :MLA_Q_RANK], qn_ref[...]).astype(BF16)
    ckv_ref[...] = _rms(p[:, MLA_Q_RANK:MLA_Q_RANK + MLA_KV_RANK], kvn_ref[...])
    kpe = p[:, MLA_Q_RANK + MLA_KV_RANK:]
    q = jnp.dot(cq, wuq_ref[...], preferred_element_type=F32) * (
        (MLA_NOPE + MLA_ROPE) ** -0.5 * float(np.log2(np.e)))
    if rope:
        cos = cos_ref[...]
        sin = sin_ref[...]
        kpe = kpe * cos + _swap_pairs(kpe) * sin
    kpe_ref[...] = kpe
    for hd in range(MLA_HEADS):
        lo = hd * HEAD_W
        q_ref[:, lo:lo + LANES] = q[:, lo:lo + LANES].astype(BF16)
        qp = q[:, lo + LANES:lo + HEAD_W]
        if rope:
            qp = qp * cos + _swap_pairs(qp) * sin
        q_ref[:, lo + LANES:lo + HEAD_W] = qp.astype(BF16)


def _mla_proj(x, mod, norm_g, w_in, q_norm, kv_norm, w_uq, rows_per_cond, cos=None, sin=None):
    n = x.shape[0]
    tm = 512
    rope = cos is not None
    full = lambda i: (0, 0)
    nin = w_in.shape[1]
    in_specs = [
        pl.BlockSpec((tm, D_MODEL), lambda i: (i, 0)),
        pl.BlockSpec((1, D_MODEL), full),
        _mod_spec(1, tm, rows_per_cond, 1),
        _mod_spec(0, tm, rows_per_cond, 1),
        pl.BlockSpec((D_MODEL, nin), full),
        pl.BlockSpec((1, MLA_Q_RANK), full),
        pl.BlockSpec((1, MLA_KV_RANK), full),
        pl.BlockSpec((MLA_Q_RANK, MLA_HEADS * HEAD_W), full),
    ]
    args = [x, norm_g, mod, mod, w_in, q_norm, kv_norm, w_uq]
    if rope:
        nt = cos.shape[0] // tm
        tab = pl.BlockSpec((tm, LANES), lambda i: (i % nt, 0))
        in_specs += [tab, tab]
        args += [cos, sin]
    return pl.pallas_call(
        functools.partial(_mla_proj_kernel, rope=rope),
        out_shape=[jax.ShapeDtypeStruct((n, MLA_HEADS * HEAD_W), BF16),
                   jax.ShapeDtypeStruct((n, MLA_KV_RANK), F32),
                   jax.ShapeDtypeStruct((n, LANES), F32)],
        grid=(n // tm,),
        in_specs=in_specs,
        out_specs=[pl.BlockSpec((tm, MLA_HEADS * HEAD_W), lambda i: (i, 0)),
                   pl.BlockSpec((tm, MLA_KV_RANK), lambda i: (i, 0)),
                   pl.BlockSpec((tm, LANES), lambda i: (i, 0))],
        compiler_params=_params(("arbitrary",)),
        name="mla_proj",
    )(*args)


def _kv_up_kernel(ckv_ref, kpe_ref, wuk_ref, wuv_ref, k_ref, v_ref):
    c = ckv_ref[...].astype(BF16)
    kn = jnp.dot(c, wuk_ref[...], preferred_element_type=F32).astype(BF16)
    vv = jnp.dot(c, wuv_ref[...], preferred_element_type=F32).astype(BF16)
    kpe = kpe_ref[...].astype(BF16)
    lane = lax.broadcasted_iota(jnp.int32, kpe.shape, 1)
    ones_col = jnp.where(lane == 0, 1.0, 0.0).astype(BF16)
    for hd in range(MLA_HEADS):
        k_ref[:, hd * HEAD_W:hd * HEAD_W + LANES] = kn[:, hd * MLA_NOPE:(hd + 1) * MLA_NOPE]
        k_ref[:, hd * HEAD_W + LANES:(hd + 1) * HEAD_W] = kpe
        v_ref[:, hd * HEAD_W:hd * HEAD_W + LANES] = vv[:, hd * MLA_V:(hd + 1) * MLA_V]
        v_ref[:, hd * HEAD_W + LANES:(hd + 1) * HEAD_W] = ones_col


def _kv_up(ckv, kpe, w_uk, w_uv):
    n = ckv.shape[0]
    tk = 512
    full = lambda i: (0, 0)
    return pl.pallas_call(
        _kv_up_kernel,
        out_shape=[jax.ShapeDtypeStruct((n, MLA_HEADS * HEAD_W), BF16),
                   jax.ShapeDtypeStruct((n, MLA_HEADS * HEAD_W), BF16)],
        grid=(n // tk,),
        in_specs=[
            pl.BlockSpec((tk, MLA_KV_RANK), lambda i: (i, 0)),
            pl.BlockSpec((tk, LANES), lambda i: (i, 0)),
            pl.BlockSpec((MLA_KV_RANK, MLA_HEADS * MLA_NOPE), full),
            pl.BlockSpec((MLA_KV_RANK, MLA_HEADS * MLA_V), full),
        ],
        out_specs=[pl.BlockSpec((tk, MLA_HEADS * HEAD_W), lambda i: (i, 0)),
                   pl.BlockSpec((tk, MLA_HEADS * HEAD_W), lambda i: (i, 0))],
        compiler_params=_params(("arbitrary",)),
        name="kv_up",
    )(ckv, kpe, w_uk, w_uv)


def _attn_kernel(q_ref, k_ref, v_ref, o_ref, *, heads):
    for hd in range(heads):
        k = k_ref[:, hd * HEAD_W:(hd + 1) * HEAD_W]
        v = v_ref[:, hd * HEAD_W:(hd + 1) * HEAD_W]
        for r in range(q_ref.shape[0] // ATTN_SUB):
            rows = slice(r * ATTN_SUB, (r + 1) * ATTN_SUB)
            q = q_ref[rows, hd * HEAD_W:(hd + 1) * HEAD_W]
            s = lax.dot_general(q, k, (((1,), (1,)), ((), ())), preferred_element_type=F32)
            m = jnp.max(s, axis=-1, keepdims=True)
            p = jnp.exp2(s - m).astype(BF16)
            o = jnp.dot(p, v, preferred_element_type=F32)
            o_ref[rows, hd * MLA_V:(hd + 1) * MLA_V] = (
                o[:, :MLA_V] / o[:, MLA_V:MLA_V + 1]).astype(o_ref.dtype)


def _attention(q, k, v, batch, tq_total, tk_total, heads_per_step, tq):
    nq = tq_total // tq
    nh = MLA_HEADS // heads_per_step
    hp = heads_per_step
    return pl.pallas_call(
        functools.partial(_attn_kernel, heads=hp),
        out_shape=jax.ShapeDtypeStruct((batch * tq_total, MLA_HEADS * MLA_V), BF16),
        grid=(batch, nh, nq),
        in_specs=[
            pl.BlockSpec((tq, hp * HEAD_W), lambda b, h, i: (b * nq + i, h)),
            pl.BlockSpec((tk_total, hp * HEAD_W), lambda b, h, i: (b, h)),
            pl.BlockSpec((tk_total, hp * HEAD_W), lambda b, h, i: (b, h)),
        ],
        out_specs=pl.BlockSpec((tq, hp * MLA_V), lambda b, h, i: (b * nq + i, h)),
        compiler_params=_params(("arbitrary", "arbitrary", "arbitrary")),
        name="attn",
    )(q, k, v)


_ROPE_PERM = np.concatenate([np.arange(0, 16), np.arange(32, 48), np.arange(16, 32), np.arange(48, 64)])


def _rope_tables(seq):
    pos = jnp.arange(seq)
    row = (pos // GRID_W).astype(F32)
    col = (pos % GRID_W).astype(F32)
    half = MLA_ROPE // 2
    inv = ROPE_BASE ** (-jnp.arange(0, half, 2, dtype=F32) / half)
    ang = jnp.concatenate([row[:, None] * inv, col[:, None] * inv], axis=1)
    cos = jnp.cos(ang)
    sin = jnp.sin(ang)
    pad = jnp.zeros((seq, LANES - MLA_ROPE), F32)
    return (jnp.concatenate([cos, cos, pad], axis=1),
            jnp.concatenate([-sin, sin, pad], axis=1))


def _prep_mla_weights(w_in, w_uq, w_uk, w_uv):
    nq = MLA_Q_RANK + MLA_KV_RANK
    w_in_p = jnp.concatenate(
        [w_in[:, :nq], w_in[:, nq:][:, _ROPE_PERM], jnp.zeros((D_MODEL, LANES - MLA_ROPE), F32)], axis=1)
    wq = w_uq.reshape(MLA_Q_RANK, MLA_HEADS, MLA_NOPE + MLA_ROPE)
    wq_p = jnp.concatenate(
        [wq[:, :, :MLA_NOPE], wq[:, :, MLA_NOPE:][:, :, _ROPE_PERM],
         jnp.zeros((MLA_Q_RANK, MLA_HEADS, HEAD_W - MLA_NOPE - MLA_ROPE), F32)], axis=2)
    return (w_in_p.astype(BF16), wq_p.reshape(MLA_Q_RANK, MLA_HEADS * HEAD_W).astype(BF16),
            w_uk.reshape(MLA_KV_RANK, MLA_HEADS * MLA_NOPE).astype(BF16),
            w_uv.reshape(MLA_KV_RANK, MLA_HEADS * MLA_V).astype(BF16))


def kernel(x_prompt, x_sample, state_ret_fwd, state_ret_bwd, cache_mla_ckv, cache_mla_kpe, c, c_ctx,
           w_ada, b_ada, norm1, norm2, norm_final, ret_w_in, ret_decay, ret_gn, ret_w_out,
           mla_w_in, mla_q_norm, mla_kv_norm, mla_w_uq, mla_w_uk, mla_w_uv, mla_w_o,
           w_router, router_bias, moe_w_gate, moe_w_up, moe_w_down):
    bp, tp, _ = x_prompt.shape
    bs, ts, _ = x_sample.shape
    past = cache_mla_ckv.shape[2]
    xp = x_prompt.reshape(bp * tp, D_MODEL)
    xs = x_sample.reshape(bs * ts, D_MODEL)

    conds = jnp.concatenate([c_ctx[None, :], c, jnp.zeros((8 - 1 - bs, D_MODEL), F32)], axis=0)
    mods = _ada_all(conds, w_ada, b_ada)

    wr_pad = jnp.pad(w_router, ((0, 0), (0, LANES - N_EXPERTS)))
    rb_pad = jnp.pad(router_bias, (0, LANES - N_EXPERTS)).reshape(1, LANES)
    wr_hi = wr_pad.astype(BF16)
    wr_cat = jnp.concatenate([wr_hi, (wr_pad - wr_hi.astype(F32)).astype(BF16)], axis=1)
    nf = norm_final.reshape(1, D_MODEL)
    cos_t, sin_t = _rope_tables(ts)

    groups = {
        "p": dict(x=xp, rows_per_cond=bp * tp, lo=0, hi=1),
        "s": dict(x=xs, rows_per_cond=ts, lo=1, hi=1 + bs),
    }
    out_ckv, out_kpe = [], []
    ret_states = None
    n_ret = state_ret_fwd.shape[1]
    wg = moe_w_gate.astype(BF16)
    wu = moe_w_up.astype(BF16)
    wd = moe_w_down.astype(BF16)

    for i in range(DEPTH):
        j = i // 2
        n1 = norm1[i].reshape(1, D_MODEL)
        n2 = norm2[i].reshape(1, D_MODEL)
        mixed = {}
        if i % 2 == 0:
            w_in = ret_w_in[j].astype(BF16)
            w_out = ret_w_out[j].astype(BF16)
            dec = jnp.broadcast_to(ret_decay[j][:, :, None, None], (2, RET_HEADS, SCAN_CHUNK, SCAN_CHUNK))
            gn = ret_gn[j]
            for name, gr in groups.items():
                mod = mods[i, gr["lo"]:gr["hi"]].reshape(-1, 1, 6 * D_MODEL)
                proj = _proj(gr["x"], mod, n1, w_in, gr["rows_per_cond"],
                             2 * RET_HEADS * RET_DK + RET_HEADS * RET_DV)
                if name == "p":
                    y, *ret_states = _retention_scan(proj, dec, gn, bp, tp, state_slot=j,
                                                     state_bufs=ret_states, n_slots=n_ret)
                else:
                    (y,) = _retention_scan(proj, dec, gn, bs, ts, state_ret_fwd, state_ret_bwd, s0_slot=j)
                mixed[name] = (y, w_out, mod)
        else:
            w_in_p, w_uq_p, w_uk, w_uv = _prep_mla_weights(mla_w_in[j], mla_w_uq[j], mla_w_uk[j], mla_w_uv[j])
            w_o = mla_w_o[j].astype(BF16)
            qn = mla_q_norm[j].reshape(1, MLA_Q_RANK)
            kvn = mla_kv_norm[j].reshape(1, MLA_KV_RANK)
            for name, gr in groups.items():
                mod = mods[i, gr["lo"]:gr["hi"]].reshape(-1, 1, 6 * D_MODEL)
                if name == "p":
                    q, ckv, kpe = _mla_proj(gr["x"], mod, n1, w_in_p, qn, kvn, w_uq_p, gr["rows_per_cond"])
                    out_ckv.append(ckv.reshape(bp, tp, MLA_KV_RANK))
                    out_kpe.append(kpe[:, :MLA_ROPE][:, _ROPE_PERM].reshape(bp, tp, MLA_ROPE))
                    k, v = _kv_up(ckv, kpe, w_uk, w_uv)
                    o = _attention(q, k, v, bp, tp, tp, MLA_HEADS, tp)
                else:
                    q, ckv, kpe = _mla_proj(gr["x"], mod, n1, w_in_p, qn, kvn, w_uq_p, gr["rows_per_cond"],
                                            cos_t, sin_t)
                    cache_kpe = jnp.pad(cache_mla_kpe[:, j][:, :, _ROPE_PERM],
                                        ((0, 0), (0, 0), (0, LANES - MLA_ROPE)))
                    ckv_all = jnp.concatenate([ckv.reshape(bs, ts, MLA_KV_RANK), cache_mla_ckv[:, j]], axis=1)
                    kpe_all = jnp.concatenate([kpe.reshape(bs, ts, LANES), cache_kpe], axis=1)
                    tk = ts + past
                    k, v = _kv_up(ckv_all.reshape(bs * tk, MLA_KV_RANK), kpe_all.reshape(bs * tk, LANES),
                                  w_uk, w_uv)
                    o = _attention(q, k, v, bs, ts, tk, 1, 2048)
                mixed[name] = (o, w_o, mod)

        for name, gr in groups.items():
            y, w_o, mod = mixed[name]
            x_new, h2, info = _mix_out(y, w_o, gr["x"], mod, n2, wr_pad, rb_pad, gr["rows_per_cond"])
            gr["x"] = _routed_moe(x_new, h2, info, mod, wg, wu, wd, wr_cat, wr_hi, i, nf, gr["rows_per_cond"],
                                  i == DEPTH - 1)

    return (groups["p"]["x"].reshape(bp, tp, D_MODEL), groups["s"]["x"].reshape(bs, ts, D_MODEL),
            ret_states[0], ret_states[1],
            jnp.stack(out_ckv, axis=1), jnp.stack(out_kpe, axis=1))
```

```python
import functools

import numpy as np
import jax
import jax.numpy as jnp
from jax import lax
from jax.experimental import pallas as pl
from jax.experimental.pallas import tpu as pltpu

F32 = jnp.float32
BF16 = jnp.bfloat16

D_MODEL = 1024
DEPTH = 4
GRID_W = 64
RET_HEADS = 4
RET_DK = 256
RET_DV = 512
SCAN_CHUNK = 256
MLA_HEADS = 8
MLA_Q_RANK = 512
MLA_KV_RANK = 256
MLA_NOPE = 128
MLA_ROPE = 64
MLA_V = 128
ROPE_BASE = 10000.0
N_EXPERTS = 16
N_GROUPS = 4
EXPERTS_PER_GROUP = N_EXPERTS // N_GROUPS
EXPERT_FF = 256
EPS = 1e-6

LANES = 128
HEAD_W = 2 * LANES
VMEM_LIMIT = 48 * 1024 * 1024
MIX_SUB = 256
ATTN_SUB = 256
MOE_EG = 4
PAIRS_PER_GROUP = EXPERTS_PER_GROUP * (EXPERTS_PER_GROUP - 1) // 2
N_BUCKETS = N_GROUPS * PAIRS_PER_GROUP
HX_W = D_MODEL + LANES
MOE_TILE = 256
PLAN_BLK = 256
PUSH_ROWS = 1024
PULL_ROWS = 512

_PAIRS = [(a, b) for a in range(EXPERTS_PER_GROUP) for b in range(a + 1, EXPERTS_PER_GROUP)]
_BUCKET_LO = np.array([g * EXPERTS_PER_GROUP + a for g in range(N_GROUPS) for a, _ in _PAIRS], np.int32)
_BUCKET_HI = np.array([g * EXPERTS_PER_GROUP + b for g in range(N_GROUPS) for _, b in _PAIRS], np.int32)


def _params(sem):
    return pltpu.CompilerParams(dimension_semantics=sem, vmem_limit_bytes=VMEM_LIMIT)


def _sigmoid(x):
    return 1.0 / (1.0 + jnp.exp(-x))


def _normmod(x, g, sc, sh):
    y = x * lax.rsqrt(jnp.mean(x * x, axis=-1, keepdims=True) + EPS)
    return (y * g) * (1.0 + sc) + sh


def _rms(x, g):
    return x * lax.rsqrt(jnp.mean(x * x, axis=-1, keepdims=True) + EPS) * g


def _ada_kernel(c_ref, w_ref, b_ref, o_ref):
    c = c_ref[...]
    s = (c * _sigmoid(c)).astype(BF16)
    o_ref[0] = jnp.dot(s, w_ref[0].astype(BF16), preferred_element_type=F32) + b_ref[0]


def _ada_all(conds, w_ada, b_ada):
    tn = 1536
    n6 = 6 * D_MODEL
    return pl.pallas_call(
        _ada_kernel,
        out_shape=jax.ShapeDtypeStruct((DEPTH, 8, n6), F32),
        grid=(DEPTH, n6 // tn),
        in_specs=[
            pl.BlockSpec((8, D_MODEL), lambda l, j: (0, 0)),
            pl.BlockSpec((1, D_MODEL, tn), lambda l, j: (l, 0, j)),
            pl.BlockSpec((1, 1, tn), lambda l, j: (l, 0, j)),
        ],
        out_specs=pl.BlockSpec((1, 8, tn), lambda l, j: (l, 0, j)),
        compiler_params=_params(("arbitrary", "arbitrary")),
        name="ada",
    )(conds, w_ada, b_ada.reshape(DEPTH, 1, n6))


def _mod_spec(which, tm, rows_per_cond, ngrid):
    if ngrid == 1:
        return pl.BlockSpec((None, 1, D_MODEL), lambda i: (i * tm // rows_per_cond, 0, which))
    return pl.BlockSpec((None, 1, D_MODEL), lambda i, j: (i * tm // rows_per_cond, 0, which))


def _silu(g):
    hg = 0.5 * g
    return hg + hg * jnp.tanh(hg)


def _proj_kernel(x_ref, g_ref, sc_ref, sh_ref, w_ref, o_ref, h_ref, *, silu_from):
    j = pl.program_id(1)

    @pl.when(j == 0)
    def _():
        h_ref[...] = _normmod(x_ref[...], g_ref[...], sc_ref[...], sh_ref[...]).astype(BF16)

    acc = jnp.dot(h_ref[...], w_ref[...], preferred_element_type=F32)
    o_ref[...] = jnp.where(j >= silu_from, _silu(acc), acc).astype(o_ref.dtype)


def _proj(x, mod, norm_g, w, rows_per_cond, silu_from_col):
    n = x.shape[0]
    nout = w.shape[1]
    tm, tn = 1024, 2048
    return pl.pallas_call(
        functools.partial(_proj_kernel, silu_from=silu_from_col // tn),
        out_shape=jax.ShapeDtypeStruct((n, nout), BF16),
        grid=(n // tm, nout // tn),
        in_specs=[
            pl.BlockSpec((tm, D_MODEL), lambda i, j: (i, 0)),
            pl.BlockSpec((1, D_MODEL), lambda i, j: (0, 0)),
            _mod_spec(1, tm, rows_per_cond, 2),
            _mod_spec(0, tm, rows_per_cond, 2),
            pl.BlockSpec((D_MODEL, tn), lambda i, j: (0, j)),
        ],
        out_specs=pl.BlockSpec((tm, tn), lambda i, j: (i, j)),
        scratch_shapes=[pltpu.VMEM((tm, D_MODEL), BF16)],
        compiler_params=_params(("arbitrary", "arbitrary")),
        name="ret_proj",
    )(x, norm_g, mod, mod, w)


def _scan_kernel(*refs, seq, has_s0, aliased):
    q_ref, k_ref, v_ref, gf_ref, gb_ref, dec_ref, gn_ref = refs[:7]
    pos = 7
    if has_s0:
        s0_refs = refs[pos:pos + 2]
        pos += 2
    if aliased:
        pos += 2
    y_ref = refs[pos]
    s_refs = refs[pos + 1:pos + 3]
    intra_ref, qdec_ref, kdec_ref, cdec_ref, yacc_ref = refs[pos + 3:pos + 8]

    C = SCAN_CHUNK
    nc = seq // C

    @pl.when(pl.program_id(1) == 0)
    def _():
        ri = lax.broadcasted_iota(jnp.int32, (C, C), 0).astype(F32)
        ci = lax.broadcasted_iota(jnp.int32, (C, C), 1).astype(F32)
        k_scale = RET_DK ** -0.5
        for d in range(2):
            lg = -jnp.exp(dec_ref[d])
            if d == 1:
                diff, q_pow, k_pow = ci - ri, C - ri, ri
            else:
                diff, q_pow, k_pow = ri - ci, ri + 1.0, C - 1.0 - ri
            intra_ref[d] = jnp.where(diff >= 0, jnp.exp(lg * jnp.maximum(diff, 0.0)), 0.0) * k_scale
            qdec_ref[d] = jnp.exp(lg * q_pow)
            kdec_ref[d] = jnp.exp(lg * k_pow) * k_scale
            cdec_ref[d] = jnp.exp(lg[0:8, :] * float(C))

    state_is_zero = not has_s0 and nc == 1
    for d in range(2):
        if has_s0:
            s_refs[d][...] = s0_refs[d][...]
        elif not state_is_zero:
            s_refs[d][...] = jnp.zeros_like(s_refs[d])

    def lanes(a, width):
        return jnp.concatenate([a] * (width // C), axis=1)

    def contribution(rows, d, att):
        g_ref = (gf_ref, gb_ref)[d]
        s_ref = s_refs[d]
        q = q_ref[rows, :]
        k = k_ref[rows, :]
        v = v_ref[rows, :]
        att = (att * intra_ref[d]).astype(BF16)
        o = jnp.dot(att, v, preferred_element_type=F32)
        kd = (k.astype(F32) * lanes(kdec_ref[d], RET_DK)).astype(BF16)
        s_new = lax.dot_general(kd, v, (((0,), (0,)), ((), ())), preferred_element_type=F32)
        if state_is_zero:
            s_ref[...] = s_new
        else:
            qd = (q.astype(F32) * lanes(qdec_ref[d], RET_DK)).astype(BF16)
            s_old = s_ref[...]
            o = o + jnp.dot(qd, s_old.astype(BF16), preferred_element_type=F32)
            s_ref[...] = s_old * lanes(cdec_ref[d][0:1, :], RET_DV) + s_new
        mu = jnp.mean(o, axis=-1, keepdims=True)
        dlt = o - mu
        var = jnp.mean(dlt * dlt, axis=-1, keepdims=True)
        yn = (dlt * lax.rsqrt(var + EPS)) * gn_ref[d:d + 1, :]
        return g_ref[rows, :].astype(F32) * yn

    def scores(rows):
        return lax.dot_general(q_ref[rows, :], k_ref[rows, :], (((1,), (1,)), ((), ())),
                               preferred_element_type=F32)

    if nc == 1:
        rows = pl.ds(0, C)
        att = scores(rows)
        y_ref[...] = (contribution(rows, 0, att) + contribution(rows, 1, att)).astype(y_ref.dtype)
    else:
        def pair(i, second_visit):
            for d, c in ((0, i), (1, nc - 1 - i)):
                rows = pl.ds(pl.multiple_of(c * C, C), C)
                contrib = contribution(rows, d, scores(rows))
                if second_visit:
                    y_ref[rows, :] = (yacc_ref[rows, :] + contrib).astype(y_ref.dtype)
                else:
                    yacc_ref[rows, :] = contrib

        def first(i, carry):
            pair(i, False)
            return carry

        def second(i, carry):
            pair(i, True)
            return carry

        lax.fori_loop(0, nc // 2, first, 0)
        lax.fori_loop(nc // 2, nc, second, 0)


def _retention_scan(proj, dec, gn, batch, seq, s0f=None, s0b=None, s0_slot=0, state_slot=None,
                    state_bufs=None, n_slots=1):
    has_s0 = s0f is not None
    want_state = state_slot is not None
    aliased = state_bufs is not None
    hv = RET_HEADS * RET_DV
    nkb = RET_HEADS
    nvb = 2 * RET_HEADS * RET_DK // RET_DV
    C = SCAN_CHUNK
    assert seq % C == 0 and (seq == C or (seq // C) % 2 == 0)
    in_specs = [
        pl.BlockSpec((seq, RET_DK), lambda h, b: (b, h)),
        pl.BlockSpec((seq, RET_DK), lambda h, b: (b, nkb + h)),
        pl.BlockSpec((seq, RET_DV), lambda h, b: (b, nvb + h)),
        pl.BlockSpec((seq, RET_DV), lambda h, b: (b, nvb + RET_HEADS + h)),
        pl.BlockSpec((seq, RET_DV), lambda h, b: (b, nvb + 2 * RET_HEADS + h)),
        pl.BlockSpec((2, None, C, C), lambda h, b: (0, h, 0, 0)),
        pl.BlockSpec((2, RET_DV), lambda h, b: (0, h)),
    ]
    args = [proj, proj, proj, proj, proj, dec, gn]
    if has_s0:
        st_spec = pl.BlockSpec((None, None, None, RET_DK, RET_DV), lambda h, b: (b, s0_slot, h, 0, 0))
        in_specs += [st_spec, st_spec]
        args += [s0f, s0b]
    aliases = {}
    if aliased:
        in_specs += [pl.BlockSpec(memory_space=pl.ANY)] * 2
        aliases = {len(args): 1, len(args) + 1: 2}
        args += list(state_bufs)
    out_shape = [jax.ShapeDtypeStruct((batch * seq, hv), BF16)]
    out_specs = [pl.BlockSpec((seq, RET_DV), lambda h, b: (b, h))]
    scratch = [pltpu.VMEM((2, C, C), F32), pltpu.VMEM((2, C, C), F32), pltpu.VMEM((2, C, C), F32),
               pltpu.VMEM((2, 8, C), F32), pltpu.VMEM((seq, RET_DV), F32)]
    if want_state:
        st = jax.ShapeDtypeStruct((batch, n_slots, RET_HEADS, RET_DK, RET_DV), F32)
        out_shape += [st, st]
        st_spec = pl.BlockSpec((None, None, None, RET_DK, RET_DV), lambda h, b: (b, state_slot, h, 0, 0))
        out_specs += [st_spec, st_spec]
    else:
        scratch = [pltpu.VMEM((RET_DK, RET_DV), F32)] * 2 + scratch
    return pl.pallas_call(
        functools.partial(_scan_kernel, seq=seq, has_s0=has_s0, aliased=aliased),
        out_shape=out_shape,
        grid=(RET_HEADS, batch),
        in_specs=in_specs,
        out_specs=out_specs,
        scratch_shapes=scratch,
        input_output_aliases=aliases,
        compiler_params=_params(("arbitrary", "arbitrary")),
        name="ret_scan",
    )(*args)


def _route(logits, bias):
    s = _sigmoid(logits)
    sb = s + bias
    lane = lax.broadcasted_iota(jnp.int32, sb.shape, 1)
    lane_f = lane.astype(F32)
    neg = -jnp.inf
    big = float(LANES)

    def top2(vals):
        m1 = jnp.max(vals, axis=-1, keepdims=True)
        i1 = jnp.min(jnp.where(vals == m1, lane_f, big), axis=-1, keepdims=True)
        vals2 = jnp.where(lane_f == i1, neg, vals)
        m2 = jnp.max(vals2, axis=-1, keepdims=True)
        i2 = jnp.min(jnp.where(vals2 == m2, lane_f, big), axis=-1, keepdims=True)
        return m1 + m2, i1, i2

    best = None
    for g in range(N_GROUPS):
        in_group = (lane // EXPERTS_PER_GROUP) == g
        gs, i1, i2 = top2(jnp.where(in_group, sb, neg))
        if best is None:
            best = (gs, i1, i2)
        else:
            take = gs > best[0]
            best = (jnp.where(take, gs, best[0]), jnp.where(take, i1, best[1]),
                    jnp.where(take, i2, best[2]))
    _, e1, e2 = best
    w1 = jnp.sum(jnp.where(lane_f == e1, s, 0.0), axis=-1, keepdims=True)
    w2 = jnp.sum(jnp.where(lane_f == e2, s, 0.0), axis=-1, keepdims=True)
    den = w1 + w2
    grp = jnp.floor(e1 * (1.0 / EXPERTS_PER_GROUP))
    first_is_lo = e1 < e2
    lo = jnp.where(first_is_lo, e1, e2) - grp * EXPERTS_PER_GROUP
    hi = jnp.where(first_is_lo, e2, e1) - grp * EXPERTS_PER_GROUP
    bucket = grp * PAIRS_PER_GROUP + 3.0 * lo - 0.5 * lo * (lo - 1.0) + hi - lo - 1.0
    w_lo = jnp.where(first_is_lo, w1, w2) / den
    w_hi = jnp.where(first_is_lo, w2, w1) / den
    return (jnp.where(lane == 0, w_lo, 0.0) + jnp.where(lane == 1, w_hi, 0.0)
            + jnp.where(lane == 2, bucket, 0.0))


def _split_bf16(a):
    hi = a.astype(BF16)
    lo = (a - hi.astype(F32)).astype(BF16)
    return hi, lo


def _mix_out_kernel(y_ref, w_ref, x_ref, g1_ref, n2_ref, sc_ref, sh_ref, wr_ref, rb_ref,
                    xo_ref, h_ref, info_ref):
    w_hi, w_lo = _split_bf16(wr_ref[...])
    w_cat = jnp.concatenate([w_hi, w_lo], axis=1)
    for r in range(y_ref.shape[0] // MIX_SUB):
        rows = slice(r * MIX_SUB, (r + 1) * MIX_SUB)
        x = x_ref[rows, :] + g1_ref[...] * jnp.dot(y_ref[rows, :], w_ref[...], preferred_element_type=F32)
        xo_ref[rows, :] = x
        h = _normmod(x, n2_ref[...], sc_ref[...], sh_ref[...])
        h_ref[rows, :] = h
        h_hi, h_lo = _split_bf16(h)
        both = jnp.dot(h_hi, w_cat, preferred_element_type=F32)
        logits = both[:, :LANES] + both[:, LANES:] + jnp.dot(h_lo, w_hi, preferred_element_type=F32)
        info_ref[rows, :] = _route(logits, rb_ref[...])


def _mix_out(y, w, x, mod, norm2_g, wr_pad, rb_pad, rows_per_cond):
    n, kdim = y.shape
    tm = 512
    full = lambda i: (0, 0)
    return pl.pallas_call(
        _mix_out_kernel,
        out_shape=[jax.ShapeDtypeStruct((n, D_MODEL), F32),
                   jax.ShapeDtypeStruct((n, D_MODEL), F32),
                   jax.ShapeDtypeStruct((n, LANES), F32)],
        grid=(n // tm,),
        in_specs=[
            pl.BlockSpec((tm, kdim), lambda i: (i, 0)),
            pl.BlockSpec((kdim, D_MODEL), full),
            pl.BlockSpec((tm, D_MODEL), lambda i: (i, 0)),
            _mod_spec(2, tm, rows_per_cond, 1),
            pl.BlockSpec((1, D_MODEL), full),
            _mod_spec(4, tm, rows_per_cond, 1),
            _mod_spec(3, tm, rows_per_cond, 1),
            pl.BlockSpec((D_MODEL, LANES), full),
            pl.BlockSpec((1, LANES), full),
        ],
        out_specs=[pl.BlockSpec((tm, D_MODEL), lambda i: (i, 0)),
                   pl.BlockSpec((tm, D_MODEL), lambda i: (i, 0)),
                   pl.BlockSpec((tm, LANES), lambda i: (i, 0))],
        compiler_params=_params(("arbitrary",)),
        name="mix_out",
    )(y, w, x, mod, norm2_g, mod, mod, wr_pad, rb_pad)


def _moe_kernel(x_ref, h_ref, gate_ref, g2_ref, wg_ref, wu_ref, wd_ref, nf_ref, o_ref, acc_ref,
                *, final_norm):
    step = pl.program_id(1)

    @pl.when(step == 0)
    def _():
        acc_ref[...] = jnp.zeros_like(acc_ref)

    h = h_ref[...]
    gate = gate_ref[...]
    lane = lax.broadcasted_iota(jnp.int32, gate.shape, 1)
    acts = []
    for k in range(MOE_EG):
        a = jnp.dot(h, wg_ref[k], preferred_element_type=F32)
        u = jnp.dot(h, wu_ref[k], preferred_element_type=F32)
        ge = jnp.sum(jnp.where(lane == step * MOE_EG + k, gate, 0.0), axis=-1, keepdims=True)
        acts.append((_silu(a) * u * ge).astype(BF16))
    act = jnp.concatenate(acts, axis=1)
    acc_ref[...] += jnp.dot(act, wd_ref[...].reshape(MOE_EG * EXPERT_FF, D_MODEL),
                            preferred_element_type=F32)

    @pl.when(step == N_EXPERTS // MOE_EG - 1)
    def _():
        x = x_ref[...] + g2_ref[...] * acc_ref[...]
        if final_norm:
            x = _rms(x, nf_ref[...])
        o_ref[...] = x


def _moe(x, h, gate, mod, wg, wu, wd, layer, norm_f, rows_per_cond, final_norm):
    n = x.shape[0]
    tm = 1024
    return pl.pallas_call(
        functools.partial(_moe_kernel, final_norm=final_norm),
        out_shape=jax.ShapeDtypeStruct((n, D_MODEL), F32),
        grid=(n // tm, N_EXPERTS // MOE_EG),
        in_specs=[
            pl.BlockSpec((tm, D_MODEL), lambda i, e: (i, 0)),
            pl.BlockSpec((tm, D_MODEL), lambda i, e: (i, 0)),
            pl.BlockSpec((tm, LANES), lambda i, e: (i, 0)),
            _mod_spec(5, tm, rows_per_cond, 2),
            pl.BlockSpec((None, MOE_EG, D_MODEL, EXPERT_FF), lambda i, e: (layer, e, 0, 0)),
            pl.BlockSpec((None, MOE_EG, D_MODEL, EXPERT_FF), lambda i, e: (layer, e, 0, 0)),
            pl.BlockSpec((None, MOE_EG, EXPERT_FF, D_MODEL), lambda i, e: (layer, e, 0, 0)),
            pl.BlockSpec((1, D_MODEL), lambda i, e: (0, 0)),
        ],
        out_specs=pl.BlockSpec((tm, D_MODEL), lambda i, e: (i, 0)),
        scratch_shapes=[pltpu.VMEM((tm, D_MODEL), F32)],
        compiler_params=_params(("arbitrary", "arbitrary")),
        name="moe",
    )(x, h, gate, mod, wg, wu, wd, norm_f)


def _plan_kernel(bk_ref, rank_ref, cnt_ref, carry_ref):
    i = pl.program_id(0)

    @pl.when(i == 0)
    def _():
        carry_ref[...] = jnp.zeros_like(carry_ref)

    b = bk_ref[0]
    sub = lax.broadcasted_iota(jnp.int32, (LANES, PLAN_BLK), 0)
    onehot = jnp.where(sub == b, 1.0, 0.0)
    ti = lax.broadcasted_iota(jnp.int32, (PLAN_BLK, PLAN_BLK), 0)
    tj = lax.broadcasted_iota(jnp.int32, (PLAN_BLK, PLAN_BLK), 1)
    upper = jnp.where(ti <= tj, 1.0, 0.0).astype(BF16)
    running = jnp.dot(onehot.astype(BF16), upper, preferred_element_type=F32) + carry_ref[...]
    rank_ref[0] = (jnp.sum(onehot * running, axis=0, keepdims=True) - 1.0).astype(jnp.int32)
    carry_ref[...] += jnp.sum(onehot, axis=1, keepdims=True)
    cnt_ref[...] = jnp.broadcast_to(carry_ref[...], cnt_ref.shape)


def _plan(bk):
    n = bk.shape[0]
    nb = n // PLAN_BLK
    rank, cnt = pl.pallas_call(
        _plan_kernel,
        out_shape=[jax.ShapeDtypeStruct((nb, 1, PLAN_BLK), jnp.int32),
                   jax.ShapeDtypeStruct((LANES, LANES), F32)],
        grid=(nb,),
        in_specs=[pl.BlockSpec((1, 1, PLAN_BLK), lambda i: (i, 0, 0))],
        out_specs=[pl.BlockSpec((1, 1, PLAN_BLK), lambda i: (i, 0, 0)),
                   pl.BlockSpec((LANES, LANES), lambda i: (0, 0))],
        scratch_shapes=[pltpu.VMEM((LANES, 1), F32)],
        compiler_params=_params(("arbitrary",)),
        name="moe_plan",
    )(bk.reshape(nb, 1, PLAN_BLK))
    return rank.reshape(n), cnt[:N_BUCKETS, 0].astype(jnp.int32)


def _row_copy(src_ref, src_row, dst_ref, dst_row, sem):
    return pltpu.make_async_copy(src_ref.at[pl.ds(src_row, 1), :], dst_ref.at[pl.ds(dst_row, 1), :], sem)


def _push_kernel(pos_ref, pad_lo_ref, pad_hi_ref, h_ref, out_ref, zero_ref, sem, zsem):
    i = pl.program_id(0)
    base = i * PUSH_ROWS

    copies = [_row_copy(h_ref, r, out_ref, pos_ref[base + r], sem) for r in range(PUSH_ROWS)]
    for r, cp in enumerate(copies):
        cp.start(priority=r % 2)
    for cp in copies:
        cp.wait()

    @pl.when(i == pl.num_programs(0) - 1)
    def _():
        zero_ref[...] = jnp.zeros_like(zero_ref)
        for k in range(N_BUCKETS):
            def zfill(p, carry):
                _row_copy(zero_ref, 0, out_ref, p, zsem).start()
                return carry

            def zwait(p, carry):
                _row_copy(zero_ref, 0, out_ref, p, zsem).wait()
                return carry

            lax.fori_loop(pad_lo_ref[k], pad_hi_ref[k], zfill, 0)
            lax.fori_loop(pad_lo_ref[k], pad_hi_ref[k], zwait, 0)


def _push(pos, pad_lo, pad_hi, h, n_sorted):
    n = h.shape[0]
    return pl.pallas_call(
        _push_kernel,
        out_shape=jax.ShapeDtypeStruct((n_sorted, D_MODEL), F32),
        grid_spec=pltpu.PrefetchScalarGridSpec(
            num_scalar_prefetch=3,
            grid=(n // PUSH_ROWS,),
            in_specs=[pl.BlockSpec((PUSH_ROWS, D_MODEL), lambda i, pos, lo, hi: (i, 0))],
            out_specs=pl.BlockSpec(memory_space=pl.ANY),
            scratch_shapes=[pltpu.VMEM((8, D_MODEL), F32), pltpu.SemaphoreType.DMA(()),
                            pltpu.SemaphoreType.DMA(())],
        ),
        compiler_params=_params(("arbitrary",)),
        name="moe_push",
    )(pos, pad_lo, pad_hi, h)


def _experts_kernel(elo_ref, ehi_ref, act_ref, s_ref, wg_hbm, wu_hbm, wd_hbm, wrc_ref, wrh_ref,
                    y_ref, wg_ref, wu_ref, wd_ref, wsem, *, layer):
    i = pl.program_id(0)

    @pl.when(i == 0)
    def _():
        loads = [pltpu.make_async_copy(src.at[layer], dst, wsem.at[k])
                 for k, (src, dst) in enumerate(((wg_hbm, wg_ref), (wu_hbm, wu_ref), (wd_hbm, wd_ref)))]
        for cp in loads:
            cp.start()
        for cp in loads:
            cp.wait()

    @pl.when(act_ref[i] == 1)
    def _():
        _expert_pair(s_ref[...], elo_ref[i], ehi_ref[i], wg_ref, wu_ref, wd_ref, wrc_ref, wrh_ref, y_ref)


def _expert_pair(h, e_lo, e_hi, wg_ref, wu_ref, wd_ref, wrc_ref, wrh_ref, y_ref):
    h_hi, h_lo = _split_bf16(h)
    both = jnp.dot(h_hi, wrc_ref[...], preferred_element_type=F32)
    logits = both[:, :LANES] + both[:, LANES:] + jnp.dot(h_lo, wrh_ref[...], preferred_element_type=F32)
    s = _sigmoid(logits)
    lane = lax.broadcasted_iota(jnp.int32, s.shape, 1)
    s_lo = jnp.sum(jnp.where(lane == e_lo, s, 0.0), axis=-1, keepdims=True)
    s_hi = jnp.sum(jnp.where(lane == e_hi, s, 0.0), axis=-1, keepdims=True)
    den = s_lo + s_hi
    y = None
    for e, w in ((e_lo, s_lo / den), (e_hi, s_hi / den)):
        a = jnp.dot(h_hi, wg_ref[e], preferred_element_type=F32)
        u = jnp.dot(h_hi, wu_ref[e], preferred_element_type=F32)
        act = (_silu(a) * u * w).astype(BF16)
        part = jnp.dot(act, wd_ref[e], preferred_element_type=F32)
        y = part if y is None else y + part
    y_ref[...] = y


def _experts(tile_blk, tile_elo, tile_ehi, tile_act, sorted_h, wg, wu, wd, wr_cat, wr_hi, layer):
    n_sorted = sorted_h.shape[0]
    row = lambda i, elo, ehi, act, blk: (blk[i], 0)
    full = lambda i, elo, ehi, act, blk: (0, 0)
    hbm = pl.BlockSpec(memory_space=pl.ANY)

    def kern(elo_ref, ehi_ref, act_ref, blk_ref, *refs):
        _experts_kernel(elo_ref, ehi_ref, act_ref, *refs, layer=layer)

    return pl.pallas_call(
        kern,
        out_shape=jax.ShapeDtypeStruct((n_sorted, D_MODEL), F32),
        grid_spec=pltpu.PrefetchScalarGridSpec(
            num_scalar_prefetch=4,
            grid=(n_sorted // MOE_TILE,),
            in_specs=[pl.BlockSpec((MOE_TILE, D_MODEL), row), hbm, hbm, hbm,
                      pl.BlockSpec((D_MODEL, 2 * LANES), full), pl.BlockSpec((D_MODEL, LANES), full)],
            out_specs=pl.BlockSpec((MOE_TILE, D_MODEL), row),
            scratch_shapes=[pltpu.VMEM((N_EXPERTS, D_MODEL, EXPERT_FF), BF16),
                            pltpu.VMEM((N_EXPERTS, D_MODEL, EXPERT_FF), BF16),
                            pltpu.VMEM((N_EXPERTS, EXPERT_FF, D_MODEL), BF16),
                            pltpu.SemaphoreType.DMA((3,))],
        ),
        compiler_params=_params(("arbitrary",)),
        name="moe_experts",
    )(tile_elo, tile_ehi, tile_act, tile_blk, sorted_h, wg, wu, wd, wr_cat, wr_hi)


def _pull_kernel(pos_ref, y_ref, x_ref, g2_ref, nf_ref, o_ref, buf_ref, sems, *, final_norm):
    i = pl.program_id(0)
    nsteps = pl.num_programs(0)

    def fetch(step, slot, wait):
        for r in range(PULL_ROWS):
            cp = _row_copy(y_ref, pos_ref[step * PULL_ROWS + r], buf_ref.at[slot], r, sems.at[slot])
            if wait:
                cp.wait()
            else:
                cp.start(priority=r % 2)

    @pl.when(i == 0)
    def _():
        fetch(0, 0, False)

    for slot in range(2):
        @pl.when(i % 2 == slot)
        def _():
            @pl.when(i + 1 < nsteps)
            def _():
                fetch(i + 1, 1 - slot, False)

            fetch(i, slot, True)
            x = x_ref[...] + g2_ref[...] * buf_ref[slot]
            if final_norm:
                x = _rms(x, nf_ref[...])
            o_ref[...] = x


def _pull(pos, y_sorted, x, mod, norm_f, rows_per_cond, final_norm):
    n = x.shape[0]
    tm = PULL_ROWS
    return pl.pallas_call(
        functools.partial(_pull_kernel, final_norm=final_norm),
        out_shape=jax.ShapeDtypeStruct((n, D_MODEL), F32),
        grid_spec=pltpu.PrefetchScalarGridSpec(
            num_scalar_prefetch=1,
            grid=(n // tm,),
            in_specs=[pl.BlockSpec(memory_space=pl.ANY),
                      pl.BlockSpec((tm, D_MODEL), lambda i, pos: (i, 0)),
                      pl.BlockSpec((None, 1, D_MODEL), lambda i, pos: (i * tm // rows_per_cond, 0, 5)),
                      pl.BlockSpec((1, D_MODEL), lambda i, pos: (0, 0))],
            out_specs=pl.BlockSpec((tm, D_MODEL), lambda i, pos: (i, 0)),
            scratch_shapes=[pltpu.VMEM((2, tm, D_MODEL), F32), pltpu.SemaphoreType.DMA((2,))],
        ),
        compiler_params=_params(("arbitrary",)),
        name="moe_pull",
    )(pos, y_sorted, x, mod, norm_f)


def _routed_moe(x, h, info, mod, wg, wu, wd, wr_cat, wr_hi, layer, norm_f, rows_per_cond, final_norm):
    n = x.shape[0]
    n_sorted = n + N_BUCKETS * MOE_TILE
    n_tiles = n_sorted // MOE_TILE
    bk = info[:, 2].astype(jnp.int32)
    rank, counts = _plan(bk)
    padded = (counts + MOE_TILE - 1) // MOE_TILE * MOE_TILE
    ends = jnp.cumsum(padded)
    starts = ends - padded
    onehot = bk[:, None] == jnp.arange(N_BUCKETS, dtype=jnp.int32)[None, :]
    pos = jnp.sum(jnp.where(onehot, starts[None, :], 0), axis=1) + rank
    n_active = ends[-1] // MOE_TILE
    tile = jnp.arange(n_tiles, dtype=jnp.int32)
    tile_act = (tile < n_active).astype(jnp.int32)
    tile_blk = jnp.minimum(tile, n_active - 1)
    tile_bucket = jnp.minimum(jnp.sum(ends[None, :] <= (tile_blk * MOE_TILE)[:, None], axis=1), N_BUCKETS - 1)
    sorted_h = _push(pos, starts + counts, ends, h, n_sorted)
    y_sorted = _experts(tile_blk, jnp.asarray(_BUCKET_LO)[tile_bucket], jnp.asarray(_BUCKET_HI)[tile_bucket],
                        tile_act, sorted_h, wg, wu, wd, wr_cat, wr_hi, layer)
    return _pull(pos, y_sorted, x, mod, norm_f, rows_per_cond, final_norm)


def _swap_pairs(x):
    lane = lax.broadcasted_iota(jnp.int32, x.shape, 1)
    first = (lane % MLA_ROPE) < (MLA_ROPE // 2)
    return jnp.where(first, pltpu.roll(x, LANES - MLA_ROPE // 2, axis=1),
                     pltpu.roll(x, MLA_ROPE // 2, axis=1))


def _mla_proj_kernel(*refs, rope):
    x_ref, g_ref, sc_ref, sh_ref, win_ref, qn_ref, kvn_ref, wuq_ref = refs[:8]
    pos = 8
    if rope:
        cos_ref, sin_ref = refs[pos:pos + 2]
        pos += 2
    q_ref, ckv_ref, kpe_ref = refs[pos:pos + 3]

    h = _normmod(x_ref[...], g_ref[...], sc_ref[...], sh_ref[...]).astype(BF16)
    p = jnp.dot(h, win_ref[...], preferred_element_type=F32)
    cq = _rms(p[:, :MLA_Q_RANK], qn_ref[...]).astype(BF16)
    ckv_ref[...] = _rms(p[:, MLA_Q_RANK:MLA_Q_RANK + MLA_KV_RANK], kvn_ref[...])
    kpe = p[:, MLA_Q_RANK + MLA_KV_RANK:]
    q = jnp.dot(cq, wuq_ref[...], preferred_element_type=F32) * (
        (MLA_NOPE + MLA_ROPE) ** -0.5 * float(np.log2(np.e)))
    if rope:
        cos = cos_ref[...]
        sin = sin_ref[...]
        kpe = kpe * cos + _swap_pairs(kpe) * sin
    kpe_ref[...] = kpe
    for hd in range(MLA_HEADS):
        lo = hd * HEAD_W
        q_ref[:, lo:lo + LANES] = q[:, lo:lo + LANES].astype(BF16)
        qp = q[:, lo + LANES:lo + HEAD_W]
        if rope:
            qp = qp * cos + _swap_pairs(qp) * sin
        q_ref[:, lo + LANES:lo + HEAD_W] = qp.astype(BF16)


def _mla_proj(x, mod, norm_g, w_in, q_norm, kv_norm, w_uq, rows_per_cond, cos=None, sin=None):
    n = x.shape[0]
    tm = 512
    rope = cos is not None
    full = lambda i: (0, 0)
    nin = w_in.shape[1]
    in_specs = [
        pl.BlockSpec((tm, D_MODEL), lambda i: (i, 0)),
        pl.BlockSpec((1, D_MODEL), full),
        _mod_spec(1, tm, rows_per_cond, 1),
        _mod_spec(0, tm, rows_per_cond, 1),
        pl.BlockSpec((D_MODEL, nin), full),
        pl.BlockSpec((1, MLA_Q_RANK), full),
        pl.BlockSpec((1, MLA_KV_RANK), full),
        pl.BlockSpec((MLA_Q_RANK, MLA_HEADS * HEAD_W), full),
    ]
    args = [x, norm_g, mod, mod, w_in, q_norm, kv_norm, w_uq]
    if rope:
        nt = cos.shape[0] // tm
        tab = pl.BlockSpec((tm, LANES), lambda i: (i % nt, 0))
        in_specs += [tab, tab]
        args += [cos, sin]
    return pl.pallas_call(
        functools.partial(_mla_proj_kernel, rope=rope),
        out_shape=[jax.ShapeDtypeStruct((n, MLA_HEADS * HEAD_W), BF16),
                   jax.ShapeDtypeStruct((n, MLA_KV_RANK), F32),
                   jax.ShapeDtypeStruct((n, LANES), F32)],
        grid=(n // tm,),
        in_specs=in_specs,
        out_specs=[pl.BlockSpec((tm, MLA_HEADS * HEAD_W), lambda i: (i, 0)),
                   pl.BlockSpec((tm, MLA_KV_RANK), lambda i: (i, 0)),
                   pl.BlockSpec((tm, LANES), lambda i: (i, 0))],
        compiler_params=_params(("arbitrary",)),
        name="mla_proj",
    )(*args)


def _kv_up_kernel(ckv_ref, kpe_ref, wuk_ref, wuv_ref, k_ref, v_ref):
    c = ckv_ref[...].astype(BF16)
    kn = jnp.dot(c, wuk_ref[...], preferred_element_type=F32).astype(BF16)
    vv = jnp.dot(c, wuv_ref[...], preferred_element_type=F32).astype(BF16)
    kpe = kpe_ref[...].astype(BF16)
    lane = lax.broadcasted_iota(jnp.int32, kpe.shape, 1)
    ones_col = jnp.where(lane == 0, 1.0, 0.0).astype(BF16)
    for hd in range(MLA_HEADS):
        k_ref[:, hd * HEAD_W:hd * HEAD_W + LANES] = kn[:, hd * MLA_NOPE:(hd + 1) * MLA_NOPE]
        k_ref[:, hd * HEAD_W + LANES:(hd + 1) * HEAD_W] = kpe
        v_ref[:, hd * HEAD_W:hd * HEAD_W + LANES] = vv[:, hd * MLA_V:(hd + 1) * MLA_V]
        v_ref[:, hd * HEAD_W + LANES:(hd + 1) * HEAD_W] = ones_col


def _kv_up(ckv, kpe, w_uk, w_uv):
    n = ckv.shape[0]
    tk = 512
    full = lambda i: (0, 0)
    return pl.pallas_call(
        _kv_up_kernel,
        out_shape=[jax.ShapeDtypeStruct((n, MLA_HEADS * HEAD_W), BF16),
                   jax.ShapeDtypeStruct((n, MLA_HEADS * HEAD_W), BF16)],
        grid=(n // tk,),
        in_specs=[
            pl.BlockSpec((tk, MLA_KV_RANK), lambda i: (i, 0)),
            pl.BlockSpec((tk, LANES), lambda i: (i, 0)),
            pl.BlockSpec((MLA_KV_RANK, MLA_HEADS * MLA_NOPE), full),
            pl.BlockSpec((MLA_KV_RANK, MLA_HEADS * MLA_V), full),
        ],
        out_specs=[pl.BlockSpec((tk, MLA_HEADS * HEAD_W), lambda i: (i, 0)),
                   pl.BlockSpec((tk, MLA_HEADS * HEAD_W), lambda i: (i, 0))],
        compiler_params=_params(("arbitrary",)),
        name="kv_up",
    )(ckv, kpe, w_uk, w_uv)


def _attn_kernel(q_ref, k_ref, v_ref, o_ref, *, heads):
    for hd in range(heads):
        k = k_ref[:, hd * HEAD_W:(hd + 1) * HEAD_W]
        v = v_ref[:, hd * HEAD_W:(hd + 1) * HEAD_W]
        for r in range(q_ref.shape[0] // ATTN_SUB):
            rows = slice(r * ATTN_SUB, (r + 1) * ATTN_SUB)
            q = q_ref[rows, hd * HEAD_W:(hd + 1) * HEAD_W]
            s = lax.dot_general(q, k, (((1,), (1,)), ((), ())), preferred_element_type=F32)
            m = jnp.max(s, axis=-1, keepdims=True)
            p = jnp.exp2(s - m).astype(BF16)
            o = jnp.dot(p, v, preferred_element_type=F32)
            o_ref[rows, hd * MLA_V:(hd + 1) * MLA_V] = (
                o[:, :MLA_V] / o[:, MLA_V:MLA_V + 1]).astype(o_ref.dtype)


def _attention(q, k, v, batch, tq_total, tk_total, heads_per_step, tq):
    nq = tq_total // tq
    nh = MLA_HEADS // heads_per_step
    hp = heads_per_step
    return pl.pallas_call(
        functools.partial(_attn_kernel, heads=hp),
        out_shape=jax.ShapeDtypeStruct((batch * tq_total, MLA_HEADS * MLA_V), BF16),
        grid=(batch, nh, nq),
        in_specs=[
            pl.BlockSpec((tq, hp * HEAD_W), lambda b, h, i: (b * nq + i, h)),
            pl.BlockSpec((tk_total, hp * HEAD_W), lambda b, h, i: (b, h)),
            pl.BlockSpec((tk_total, hp * HEAD_W), lambda b, h, i: (b, h)),
        ],
        out_specs=pl.BlockSpec((tq, hp * MLA_V), lambda b, h, i: (b * nq + i, h)),
        compiler_params=_params(("arbitrary", "arbitrary", "arbitrary")),
        name="attn",
    )(q, k, v)


_ROPE_PERM = np.concatenate([np.arange(0, 16), np.arange(32, 48), np.arange(16, 32), np.arange(48, 64)])


def _rope_tables(seq):
    pos = jnp.arange(seq)
    row = (pos // GRID_W).astype(F32)
    col = (pos % GRID_W).astype(F32)
    half = MLA_ROPE // 2
    inv = ROPE_BASE ** (-jnp.arange(0, half, 2, dtype=F32) / half)
    ang = jnp.concatenate([row[:, None] * inv, col[:, None] * inv], axis=1)
    cos = jnp.cos(ang)
    sin = jnp.sin(ang)
    pad = jnp.zeros((seq, LANES - MLA_ROPE), F32)
    return (jnp.concatenate([cos, cos, pad], axis=1),
            jnp.concatenate([-sin, sin, pad], axis=1))


def _prep_mla_weights(w_in, w_uq, w_uk, w_uv):
    nq = MLA_Q_RANK + MLA_KV_RANK
    w_in_p = jnp.concatenate(
        [w_in[:, :nq], w_in[:, nq:][:, _ROPE_PERM], jnp.zeros((D_MODEL, LANES - MLA_ROPE), F32)], axis=1)
    wq = w_uq.reshape(MLA_Q_RANK, MLA_HEADS, MLA_NOPE + MLA_ROPE)
    wq_p = jnp.concatenate(
        [wq[:, :, :MLA_NOPE], wq[:, :, MLA_NOPE:][:, :, _ROPE_PERM],
         jnp.zeros((MLA_Q_RANK, MLA_HEADS, HEAD_W - MLA_NOPE - MLA_ROPE), F32)], axis=2)
    return (w_in_p.astype(BF16), wq_p.reshape(MLA_Q_RANK, MLA_HEADS * HEAD_W).astype(BF16),
            w_uk.reshape(MLA_KV_RANK, MLA_HEADS * MLA_NOPE).astype(BF16),
            w_uv.reshape(MLA_KV_RANK, MLA_HEADS * MLA_V).astype(BF16))


def kernel(x_prompt, x_sample, state_ret_fwd, state_ret_bwd, cache_mla_ckv, cache_mla_kpe, c, c_ctx,
           w_ada, b_ada, norm1, norm2, norm_final, ret_w_in, ret_decay, ret_gn, ret_w_out,
           mla_w_in, mla_q_norm, mla_kv_norm, mla_w_uq, mla_w_uk, mla_w_uv, mla_w_o,
           w_router, router_bias, moe_w_gate, moe_w_up, moe_w_down):
    bp, tp, _ = x_prompt.shape
    bs, ts, _ = x_sample.shape
    past = cache_mla_ckv.shape[2]
    xp = x_prompt.reshape(bp * tp, D_MODEL)
    xs = x_sample.reshape(bs * ts, D_MODEL)

    conds = jnp.concatenate([c_ctx[None, :], c, jnp.zeros((8 - 1 - bs, D_MODEL), F32)], axis=0)
    mods = _ada_all(conds, w_ada, b_ada)

    wr_pad = jnp.pad(w_router, ((0, 0), (0, LANES - N_EXPERTS)))
    rb_pad = jnp.pad(router_bias, (0, LANES - N_EXPERTS)).reshape(1, LANES)
    wr_hi = wr_pad.astype(BF16)
    wr_cat = jnp.concatenate([wr_hi, (wr_pad - wr_hi.astype(F32)).astype(BF16)], axis=1)
    nf = norm_final.reshape(1, D_MODEL)
    cos_t, sin_t = _rope_tables(ts)

    groups = {
        "p": dict(x=xp, rows_per_cond=bp * tp, lo=0, hi=1),
        "s": dict(x=xs, rows_per_cond=ts, lo=1, hi=1 + bs),
    }
    out_ckv, out_kpe = [], []
    ret_states = None
    n_ret = state_ret_fwd.shape[1]
    wg = moe_w_gate.astype(BF16)
    wu = moe_w_up.astype(BF16)
    wd = moe_w_down.astype(BF16)

    for i in range(DEPTH):
        j = i // 2
        n1 = norm1[i].reshape(1, D_MODEL)
        n2 = norm2[i].reshape(1, D_MODEL)
        mixed = {}
        if i % 2 == 0:
            w_in = ret_w_in[j].astype(BF16)
            w_out = ret_w_out[j].astype(BF16)
            dec = jnp.broadcast_to(ret_decay[j][:, :, None, None], (2, RET_HEADS, SCAN_CHUNK, SCAN_CHUNK))
            gn = ret_gn[j]
            for name, gr in groups.items():
                mod = mods[i, gr["lo"]:gr["hi"]].reshape(-1, 1, 6 * D_MODEL)
                proj = _proj(gr["x"], mod, n1, w_in, gr["rows_per_cond"],
                             2 * RET_HEADS * RET_DK + RET_HEADS * RET_DV)
                if name == "p":
                    y, *ret_states = _retention_scan(proj, dec, gn, bp, tp, state_slot=j,
                                                     state_bufs=ret_states, n_slots=n_ret)
                else:
                    (y,) = _retention_scan(proj, dec, gn, bs, ts, state_ret_fwd, state_ret_bwd, s0_slot=j)
                mixed[name] = (y, w_out, mod)
        else:
            w_in_p, w_uq_p, w_uk, w_uv = _prep_mla_weights(mla_w_in[j], mla_w_uq[j], mla_w_uk[j], mla_w_uv[j])
            w_o = mla_w_o[j].astype(BF16)
            qn = mla_q_norm[j].reshape(1, MLA_Q_RANK)
            kvn = mla_kv_norm[j].reshape(1, MLA_KV_RANK)
            for name, gr in groups.items():
                mod = mods[i, gr["lo"]:gr["hi"]].reshape(-1, 1, 6 * D_MODEL)
                if name == "p":
                    q, ckv, kpe = _mla_proj(gr["x"], mod, n1, w_in_p, qn, kvn, w_uq_p, gr["rows_per_cond"])
                    out_ckv.append(ckv.reshape(bp, tp, MLA_KV_RANK))
                    out_kpe.append(kpe[:, :MLA_ROPE][:, _ROPE_PERM].reshape(bp, tp, MLA_ROPE))
                    k, v = _kv_up(ckv, kpe, w_uk, w_uv)
                    o = _attention(q, k, v, bp, tp, tp, MLA_HEADS, tp)
                else:
                    q, ckv, kpe = _mla_proj(gr["x"], mod, n1, w_in_p, qn, kvn, w_uq_p, gr["rows_per_cond"],
                                            cos_t, sin_t)
                    cache_kpe = jnp.pad(cache_mla_kpe[:, j][:, :, _ROPE_PERM],
                                        ((0, 0), (0, 0), (0, LANES - MLA_ROPE)))
                    ckv_all = jnp.concatenate([ckv.reshape(bs, ts, MLA_KV_RANK), cache_mla_ckv[:, j]], axis=1)
                    kpe_all = jnp.concatenate([kpe.reshape(bs, ts, LANES), cache_kpe], axis=1)
                    tk = ts + past
                    k, v = _kv_up(ckv_all.reshape(bs * tk, MLA_KV_RANK), kpe_all.reshape(bs * tk, LANES),
                                  w_uk, w_uv)
                    o = _attention(q, k, v, bs, ts, tk, 1, 2048)
                mixed[name] = (o, w_o, mod)

        for name, gr in groups.items():
            y, w_o, mod = mixed[name]
            x_new, h2, info = _mix_out(y, w_o, gr["x"], mod, n2, wr_pad, rb_pad, gr["rows_per_cond"])
            gr["x"] = _routed_moe(x_new, h2, info, mod, wg, wu, wd, wr_cat, wr_hi, i, nf, gr["rows_per_cond"],
                                  i == DEPTH - 1)

    return (groups["p"]["x"].reshape(bp, tp, D_MODEL), groups["s"]["x"].reshape(bs, ts, D_MODEL),
            ret_states[0], ret_states[1],
            jnp.stack(out_ckv, axis=1), jnp.stack(out_kpe, axis=1))
```

```python
import functools

import numpy as np
import jax
import jax.numpy as jnp
from jax import lax
from jax.experimental import pallas as pl
from jax.experimental.pallas import tpu as pltpu

F32 = jnp.float32
BF16 = jnp.bfloat16

D_MODEL = 1024
DEPTH = 4
GRID_W = 64
RET_HEADS = 4
RET_DK = 256
RET_DV = 512
SCAN_CHUNK = 256
MLA_HEADS = 8
MLA_Q_RANK = 512
MLA_KV_RANK = 256
MLA_NOPE = 128
MLA_ROPE = 64
MLA_V = 128
ROPE_BASE = 10000.0
N_EXPERTS = 16
N_GROUPS = 4
EXPERTS_PER_GROUP = N_EXPERTS // N_GROUPS
EXPERT_FF = 256
EPS = 1e-6

LANES = 128
HEAD_W = 2 * LANES
VMEM_LIMIT = 48 * 1024 * 1024
MIX_SUB = 256
ATTN_SUB = 256
PAIRS_PER_GROUP = EXPERTS_PER_GROUP * (EXPERTS_PER_GROUP - 1) // 2
N_BUCKETS = N_GROUPS * PAIRS_PER_GROUP
MOE_TILE = 256
PLAN_BLK = 256
PUSH_ROWS = 2048
PULL_ROWS = 512

_PAIRS = [(a, b) for a in range(EXPERTS_PER_GROUP) for b in range(a + 1, EXPERTS_PER_GROUP)]
_BUCKET_LO = np.array([g * EXPERTS_PER_GROUP + a for g in range(N_GROUPS) for a, _ in _PAIRS], np.int32)
_BUCKET_HI = np.array([g * EXPERTS_PER_GROUP + b for g in range(N_GROUPS) for _, b in _PAIRS], np.int32)


def _params(sem):
    return pltpu.CompilerParams(dimension_semantics=sem, vmem_limit_bytes=VMEM_LIMIT)


def _sigmoid(x):
    return 1.0 / (1.0 + jnp.exp(-x))


def _normmod(x, g, sc, sh):
    y = x * lax.rsqrt(jnp.mean(x * x, axis=-1, keepdims=True) + EPS)
    return (y * g) * (1.0 + sc) + sh


def _rms(x, g):
    return x * lax.rsqrt(jnp.mean(x * x, axis=-1, keepdims=True) + EPS) * g


def _ada_kernel(c_ref, w_ref, b_ref, o_ref):
    c = c_ref[...]
    s = (c * _sigmoid(c)).astype(BF16)
    o_ref[0] = jnp.dot(s, w_ref[0].astype(BF16), preferred_element_type=F32) + b_ref[0]


def _ada_all(conds, w_ada, b_ada):
    tn = 1536
    n6 = 6 * D_MODEL
    return pl.pallas_call(
        _ada_kernel,
        out_shape=jax.ShapeDtypeStruct((DEPTH, 8, n6), F32),
        grid=(DEPTH, n6 // tn),
        in_specs=[
            pl.BlockSpec((8, D_MODEL), lambda l, j: (0, 0)),
            pl.BlockSpec((1, D_MODEL, tn), lambda l, j: (l, 0, j)),
            pl.BlockSpec((1, 1, tn), lambda l, j: (l, 0, j)),
        ],
        out_specs=pl.BlockSpec((1, 8, tn), lambda l, j: (l, 0, j)),
        compiler_params=_params(("arbitrary", "arbitrary")),
        name="ada",
    )(conds, w_ada, b_ada.reshape(DEPTH, 1, n6))


def _mod_spec(which, tm, rows_per_cond, ngrid):
    if ngrid == 1:
        return pl.BlockSpec((None, 1, D_MODEL), lambda i: (i * tm // rows_per_cond, 0, which))
    return pl.BlockSpec((None, 1, D_MODEL), lambda i, j: (i * tm // rows_per_cond, 0, which))


def _silu(g):
    hg = 0.5 * g
    return hg + hg * jnp.tanh(hg)


def _proj_kernel(x_ref, g_ref, sc_ref, sh_ref, w_ref, o_ref, h_ref, *, silu_from):
    j = pl.program_id(1)

    @pl.when(j == 0)
    def _():
        h_ref[...] = _normmod(x_ref[...], g_ref[...], sc_ref[...], sh_ref[...]).astype(BF16)

    acc = jnp.dot(h_ref[...], w_ref[...], preferred_element_type=F32)
    o_ref[...] = jnp.where(j >= silu_from, _silu(acc), acc).astype(o_ref.dtype)


def _proj(x, mod, norm_g, w, rows_per_cond, silu_from_col):
    n = x.shape[0]
    nout = w.shape[1]
    tm, tn = 1024, 2048
    return pl.pallas_call(
        functools.partial(_proj_kernel, silu_from=silu_from_col // tn),
        out_shape=jax.ShapeDtypeStruct((n, nout), BF16),
        grid=(n // tm, nout // tn),
        in_specs=[
            pl.BlockSpec((tm, D_MODEL), lambda i, j: (i, 0)),
            pl.BlockSpec((1, D_MODEL), lambda i, j: (0, 0)),
            _mod_spec(1, tm, rows_per_cond, 2),
            _mod_spec(0, tm, rows_per_cond, 2),
            pl.BlockSpec((D_MODEL, tn), lambda i, j: (0, j)),
        ],
        out_specs=pl.BlockSpec((tm, tn), lambda i, j: (i, j)),
        scratch_shapes=[pltpu.VMEM((tm, D_MODEL), BF16)],
        compiler_params=_params(("arbitrary", "arbitrary")),
        name="ret_proj",
    )(x, norm_g, mod, mod, w)


def _scan_kernel(*refs, seq, has_s0, aliased):
    q_ref, k_ref, v_ref, gf_ref, gb_ref, dec_ref, gn_ref = refs[:7]
    pos = 7
    if has_s0:
        s0_refs = refs[pos:pos + 2]
        pos += 2
    if aliased:
        pos += 2
    y_ref = refs[pos]
    s_refs = refs[pos + 1:pos + 3]
    intra_ref, qdec_ref, kdec_ref, cdec_ref, yacc_ref = refs[pos + 3:pos + 8]

    C = SCAN_CHUNK
    nc = seq // C

    @pl.when(pl.program_id(1) == 0)
    def _():
        ri = lax.broadcasted_iota(jnp.int32, (C, C), 0).astype(F32)
        ci = lax.broadcasted_iota(jnp.int32, (C, C), 1).astype(F32)
        k_scale = RET_DK ** -0.5
        for d in range(2):
            lg = -jnp.exp(dec_ref[d])
            if d == 1:
                diff, q_pow, k_pow = ci - ri, C - ri, ri
            else:
                diff, q_pow, k_pow = ri - ci, ri + 1.0, C - 1.0 - ri
            intra_ref[d] = jnp.where(diff >= 0, jnp.exp(lg * jnp.maximum(diff, 0.0)), 0.0) * k_scale
            qdec_ref[d] = jnp.exp(lg * q_pow)
            kdec_ref[d] = jnp.exp(lg * k_pow) * k_scale
            cdec_ref[d] = jnp.exp(lg[0:8, :] * float(C))

    state_is_zero = not has_s0 and nc == 1
    for d in range(2):
        if has_s0:
            s_refs[d][...] = s0_refs[d][...]
        elif not state_is_zero:
            s_refs[d][...] = jnp.zeros_like(s_refs[d])

    def lanes(a, width):
        return jnp.concatenate([a] * (width // C), axis=1)

    def contribution(rows, d, att):
        g_ref = (gf_ref, gb_ref)[d]
        s_ref = s_refs[d]
        q = q_ref[rows, :]
        k = k_ref[rows, :]
        v = v_ref[rows, :]
        att = (att * intra_ref[d]).astype(BF16)
        o = jnp.dot(att, v, preferred_element_type=F32)
        kd = (k.astype(F32) * lanes(kdec_ref[d], RET_DK)).astype(BF16)
        s_new = lax.dot_general(kd, v, (((0,), (0,)), ((), ())), preferred_element_type=F32)
        if state_is_zero:
            s_ref[...] = s_new
        else:
            qd = (q.astype(F32) * lanes(qdec_ref[d], RET_DK)).astype(BF16)
            s_old = s_ref[...]
            o = o + jnp.dot(qd, s_old.astype(BF16), preferred_element_type=F32)
            s_ref[...] = s_old * lanes(cdec_ref[d][0:1, :], RET_DV) + s_new
        mu = jnp.mean(o, axis=-1, keepdims=True)
        dlt = o - mu
        var = jnp.mean(dlt * dlt, axis=-1, keepdims=True)
        yn = (dlt * lax.rsqrt(var + EPS)) * gn_ref[d:d + 1, :]
        return g_ref[rows, :].astype(F32) * yn

    def scores(rows):
        return lax.dot_general(q_ref[rows, :], k_ref[rows, :], (((1,), (1,)), ((), ())),
                               preferred_element_type=F32)

    if nc == 1:
        rows = pl.ds(0, C)
        att = scores(rows)
        y_ref[...] = (contribution(rows, 0, att) + contribution(rows, 1, att)).astype(y_ref.dtype)
    else:
        def pair(i, second_visit):
            for d, c in ((0, i), (1, nc - 1 - i)):
                rows = pl.ds(pl.multiple_of(c * C, C), C)
                contrib = contribution(rows, d, scores(rows))
                if second_visit:
                    y_ref[rows, :] = (yacc_ref[rows, :] + contrib).astype(y_ref.dtype)
                else:
                    yacc_ref[rows, :] = contrib

        def first(i, carry):
            pair(i, False)
            return carry

        def second(i, carry):
            pair(i, True)
            return carry

        lax.fori_loop(0, nc // 2, first, 0)
        lax.fori_loop(nc // 2, nc, second, 0)


def _retention_scan(proj, dec, gn, batch, seq, s0f=None, s0b=None, s0_slot=0, state_slot=None,
                    state_bufs=None, n_slots=1):
    has_s0 = s0f is not None
    want_state = state_slot is not None
    aliased = state_bufs is not None
    hv = RET_HEADS * RET_DV
    nkb = RET_HEADS
    nvb = 2 * RET_HEADS * RET_DK // RET_DV
    C = SCAN_CHUNK
    assert seq % C == 0 and (seq == C or (seq // C) % 2 == 0)
    in_specs = [
        pl.BlockSpec((seq, RET_DK), lambda h, b: (b, h)),
        pl.BlockSpec((seq, RET_DK), lambda h, b: (b, nkb + h)),
        pl.BlockSpec((seq, RET_DV), lambda h, b: (b, nvb + h)),
        pl.BlockSpec((seq, RET_DV), lambda h, b: (b, nvb + RET_HEADS + h)),
        pl.BlockSpec((seq, RET_DV), lambda h, b: (b, nvb + 2 * RET_HEADS + h)),
        pl.BlockSpec((2, None, C, C), lambda h, b: (0, h, 0, 0)),
        pl.BlockSpec((2, RET_DV), lambda h, b: (0, h)),
    ]
    args = [proj, proj, proj, proj, proj, dec, gn]
    if has_s0:
        st_spec = pl.BlockSpec((None, None, None, RET_DK, RET_DV), lambda h, b: (b, s0_slot, h, 0, 0))
        in_specs += [st_spec, st_spec]
        args += [s0f, s0b]
    aliases = {}
    if aliased:
        in_specs += [pl.BlockSpec(memory_space=pl.ANY)] * 2
        aliases = {len(args): 1, len(args) + 1: 2}
        args += list(state_bufs)
    out_shape = [jax.ShapeDtypeStruct((batch * seq, hv), BF16)]
    out_specs = [pl.BlockSpec((seq, RET_DV), lambda h, b: (b, h))]
    scratch = [pltpu.VMEM((2, C, C), F32), pltpu.VMEM((2, C, C), F32), pltpu.VMEM((2, C, C), F32),
               pltpu.VMEM((2, 8, C), F32), pltpu.VMEM((seq, RET_DV), F32)]
    if want_state:
        st = jax.ShapeDtypeStruct((batch, n_slots, RET_HEADS, RET_DK, RET_DV), F32)
        out_shape += [st, st]
        st_spec = pl.BlockSpec((None, None, None, RET_DK, RET_DV), lambda h, b: (b, state_slot, h, 0, 0))
        out_specs += [st_spec, st_spec]
    else:
        scratch = [pltpu.VMEM((RET_DK, RET_DV), F32)] * 2 + scratch
    return pl.pallas_call(
        functools.partial(_scan_kernel, seq=seq, has_s0=has_s0, aliased=aliased),
        out_shape=out_shape,
        grid=(RET_HEADS, batch),
        in_specs=in_specs,
        out_specs=out_specs,
        scratch_shapes=scratch,
        input_output_aliases=aliases,
        compiler_params=_params(("arbitrary", "arbitrary")),
        name="ret_scan",
    )(*args)


def _route(logits, bias):
    s = _sigmoid(logits)
    sb = s + bias
    lane = lax.broadcasted_iota(jnp.int32, sb.shape, 1)
    lane_f = lane.astype(F32)
    neg = -jnp.inf
    big = float(LANES)

    def top2(vals):
        m1 = jnp.max(vals, axis=-1, keepdims=True)
        i1 = jnp.min(jnp.where(vals == m1, lane_f, big), axis=-1, keepdims=True)
        vals2 = jnp.where(lane_f == i1, neg, vals)
        m2 = jnp.max(vals2, axis=-1, keepdims=True)
        i2 = jnp.min(jnp.where(vals2 == m2, lane_f, big), axis=-1, keepdims=True)
        return m1 + m2, i1, i2

    best = None
    for g in range(N_GROUPS):
        in_group = (lane // EXPERTS_PER_GROUP) == g
        gs, i1, i2 = top2(jnp.where(in_group, sb, neg))
        if best is None:
            best = (gs, i1, i2)
        else:
            take = gs > best[0]
            best = (jnp.where(take, gs, best[0]), jnp.where(take, i1, best[1]),
                    jnp.where(take, i2, best[2]))
    _, e1, e2 = best
    w1 = jnp.sum(jnp.where(lane_f == e1, s, 0.0), axis=-1, keepdims=True)
    w2 = jnp.sum(jnp.where(lane_f == e2, s, 0.0), axis=-1, keepdims=True)
    den = w1 + w2
    grp = jnp.floor(e1 * (1.0 / EXPERTS_PER_GROUP))
    first_is_lo = e1 < e2
    lo = jnp.where(first_is_lo, e1, e2) - grp * EXPERTS_PER_GROUP
    hi = jnp.where(first_is_lo, e2, e1) - grp * EXPERTS_PER_GROUP
    bucket = grp * PAIRS_PER_GROUP + 3.0 * lo - 0.5 * lo * (lo - 1.0) + hi - lo - 1.0
    w_lo = jnp.where(first_is_lo, w1, w2) / den
    w_hi = jnp.where(first_is_lo, w2, w1) / den
    return (jnp.where(lane == 0, w_lo, 0.0) + jnp.where(lane == 1, w_hi, 0.0)
            + jnp.where(lane == 2, bucket, 0.0))


def _split_bf16(a):
    hi = a.astype(BF16)
    lo = (a - hi.astype(F32)).astype(BF16)
    return hi, lo


def _mix_out_kernel(y_ref, w_ref, x_ref, g1_ref, n2_ref, sc_ref, sh_ref, wr_ref, rb_ref,
                    xo_ref, h_ref, info_ref):
    w_hi, w_lo = _split_bf16(wr_ref[...])
    w_cat = jnp.concatenate([w_hi, w_lo], axis=1)
    for r in range(y_ref.shape[0] // MIX_SUB):
        rows = slice(r * MIX_SUB, (r + 1) * MIX_SUB)
        x = x_ref[rows, :] + g1_ref[...] * jnp.dot(y_ref[rows, :], w_ref[...], preferred_element_type=F32)
        xo_ref[rows, :] = x
        h = _normmod(x, n2_ref[...], sc_ref[...], sh_ref[...])
        h_ref[rows, :] = h
        h_hi, h_lo = _split_bf16(h)
        both = jnp.dot(h_hi, w_cat, preferred_element_type=F32)
        logits = both[:, :LANES] + both[:, LANES:] + jnp.dot(h_lo, w_hi, preferred_element_type=F32)
        info_ref[rows, :] = _route(logits, rb_ref[...])


def _mix_out(y, w, x, mod, norm2_g, wr_pad, rb_pad, rows_per_cond):
    n, kdim = y.shape
    tm = 512
    full = lambda i: (0, 0)
    return pl.pallas_call(
        _mix_out_kernel,
        out_shape=[jax.ShapeDtypeStruct((n, D_MODEL), F32),
                   jax.ShapeDtypeStruct((n, D_MODEL), F32),
                   jax.ShapeDtypeStruct((n, LANES), F32)],
        grid=(n // tm,),
        in_specs=[
            pl.BlockSpec((tm, kdim), lambda i: (i, 0)),
            pl.BlockSpec((kdim, D_MODEL), full),
            pl.BlockSpec((tm, D_MODEL), lambda i: (i, 0)),
            _mod_spec(2, tm, rows_per_cond, 1),
            pl.BlockSpec((1, D_MODEL), full),
            _mod_spec(4, tm, rows_per_cond, 1),
            _mod_spec(3, tm, rows_per_cond, 1),
            pl.BlockSpec((D_MODEL, LANES), full),
            pl.BlockSpec((1, LANES), full),
        ],
        out_specs=[pl.BlockSpec((tm, D_MODEL), lambda i: (i, 0)),
                   pl.BlockSpec((tm, D_MODEL), lambda i: (i, 0)),
                   pl.BlockSpec((tm, LANES), lambda i: (i, 0))],
        compiler_params=_params(("arbitrary",)),
        name="mix_out",
    )(y, w, x, mod, norm2_g, mod, mod, wr_pad, rb_pad)


def _plan_kernel(bk_ref, rank_ref, cnt_ref, carry_ref):
    i = pl.program_id(0)

    @pl.when(i == 0)
    def _():
        carry_ref[...] = jnp.zeros_like(carry_ref)

    b = bk_ref[0]
    sub = lax.broadcasted_iota(jnp.int32, (LANES, PLAN_BLK), 0)
    onehot = jnp.where(sub == b, 1.0, 0.0)
    ti = lax.broadcasted_iota(jnp.int32, (PLAN_BLK, PLAN_BLK), 0)
    tj = lax.broadcasted_iota(jnp.int32, (PLAN_BLK, PLAN_BLK), 1)
    upper = jnp.where(ti <= tj, 1.0, 0.0).astype(BF16)
    running = jnp.dot(onehot.astype(BF16), upper, preferred_element_type=F32) + carry_ref[...]
    rank_ref[0] = (jnp.sum(onehot * running, axis=0, keepdims=True) - 1.0).astype(jnp.int32)
    carry_ref[...] += jnp.sum(onehot, axis=1, keepdims=True)
    cnt_ref[...] = jnp.broadcast_to(carry_ref[...], cnt_ref.shape)


def _plan(bk):
    n = bk.shape[0]
    nb = n // PLAN_BLK
    rank, cnt = pl.pallas_call(
        _plan_kernel,
        out_shape=[jax.ShapeDtypeStruct((nb, 1, PLAN_BLK), jnp.int32),
                   jax.ShapeDtypeStruct((LANES, LANES), F32)],
        grid=(nb,),
        in_specs=[pl.BlockSpec((1, 1, PLAN_BLK), lambda i: (i, 0, 0))],
        out_specs=[pl.BlockSpec((1, 1, PLAN_BLK), lambda i: (i, 0, 0)),
                   pl.BlockSpec((LANES, LANES), lambda i: (0, 0))],
        scratch_shapes=[pltpu.VMEM((LANES, 1), F32)],
        compiler_params=_params(("arbitrary",)),
        name="moe_plan",
    )(bk.reshape(nb, 1, PLAN_BLK))
    return rank.reshape(n), cnt[:N_BUCKETS, 0].astype(jnp.int32)


def _row_copy(src_ref, src_row, dst_ref, dst_row, sem):
    return pltpu.make_async_copy(src_ref.at[pl.ds(src_row, 1), :], dst_ref.at[pl.ds(dst_row, 1), :], sem)


def _push_kernel(pos_ref, pad_lo_ref, pad_hi_ref, h_ref, out_ref, zero_ref, sem, zsem):
    i = pl.program_id(0)
    base = i * PUSH_ROWS

    copies = [_row_copy(h_ref, r, out_ref, pos_ref[base + r], sem) for r in range(PUSH_ROWS)]
    for r, cp in enumerate(copies):
        cp.start(priority=r % 2)
    for cp in copies:
        cp.wait()

    @pl.when(i == pl.num_programs(0) - 1)
    def _():
        zero_ref[...] = jnp.zeros_like(zero_ref)
        for k in range(N_BUCKETS):
            def zfill(p, carry):
                _row_copy(zero_ref, 0, out_ref, p, zsem).start()
                return carry

            def zwait(p, carry):
                _row_copy(zero_ref, 0, out_ref, p, zsem).wait()
                return carry

            lax.fori_loop(pad_lo_ref[k], pad_hi_ref[k], zfill, 0)
            lax.fori_loop(pad_lo_ref[k], pad_hi_ref[k], zwait, 0)


def _push(pos, pad_lo, pad_hi, h, n_sorted):
    n = h.shape[0]
    return pl.pallas_call(
        _push_kernel,
        out_shape=jax.ShapeDtypeStruct((n_sorted, D_MODEL), F32),
        grid_spec=pltpu.PrefetchScalarGridSpec(
            num_scalar_prefetch=3,
            grid=(n // PUSH_ROWS,),
            in_specs=[pl.BlockSpec((PUSH_ROWS, D_MODEL), lambda i, pos, lo, hi: (i, 0))],
            out_specs=pl.BlockSpec(memory_space=pl.ANY),
            scratch_shapes=[pltpu.VMEM((8, D_MODEL), F32), pltpu.SemaphoreType.DMA(()),
                            pltpu.SemaphoreType.DMA(())],
        ),
        compiler_params=_params(("arbitrary",)),
        name="moe_push",
    )(pos, pad_lo, pad_hi, h)


def _experts_kernel(elo_ref, ehi_ref, act_ref, s_ref, wg_hbm, wu_hbm, wd_hbm, wrc_ref, wrh_ref,
                    y_ref, wg_ref, wu_ref, wd_ref, wsem, *, layer):
    i = pl.program_id(0)

    @pl.when(i == 0)
    def _():
        loads = [pltpu.make_async_copy(src.at[layer], dst, wsem.at[k])
                 for k, (src, dst) in enumerate(((wg_hbm, wg_ref), (wu_hbm, wu_ref), (wd_hbm, wd_ref)))]
        for cp in loads:
            cp.start()
        for cp in loads:
            cp.wait()

    @pl.when(act_ref[i] == 1)
    def _():
        _expert_pair(s_ref[...], elo_ref[i], ehi_ref[i], wg_ref, wu_ref, wd_ref, wrc_ref, wrh_ref, y_ref)


def _expert_pair(h, e_lo, e_hi, wg_ref, wu_ref, wd_ref, wrc_ref, wrh_ref, y_ref):
    h_hi, h_lo = _split_bf16(h)
    both = jnp.dot(h_hi, wrc_ref[...], preferred_element_type=F32)
    logits = both[:, :LANES] + both[:, LANES:] + jnp.dot(h_lo, wrh_ref[...], preferred_element_type=F32)
    s = _sigmoid(logits)
    lane = lax.broadcasted_iota(jnp.int32, s.shape, 1)
    s_lo = jnp.sum(jnp.where(lane == e_lo, s, 0.0), axis=-1, keepdims=True)
    s_hi = jnp.sum(jnp.where(lane == e_hi, s, 0.0), axis=-1, keepdims=True)
    den = s_lo + s_hi
    y = None
    for e, w in ((e_lo, s_lo / den), (e_hi, s_hi / den)):
        a = jnp.dot(h_hi, wg_ref[e], preferred_element_type=F32)
        u = jnp.dot(h_hi, wu_ref[e], preferred_element_type=F32)
        act = (_silu(a) * u * w).astype(BF16)
        part = jnp.dot(act, wd_ref[e], preferred_element_type=F32)
        y = part if y is None else y + part
    y_ref[...] = y


def _experts(tile_blk, tile_elo, tile_ehi, tile_act, sorted_h, wg, wu, wd, wr_cat, wr_hi, layer):
    n_sorted = sorted_h.shape[0]
    row = lambda i, elo, ehi, act, blk: (blk[i], 0)
    full = lambda i, elo, ehi, act, blk: (0, 0)
    hbm = pl.BlockSpec(memory_space=pl.ANY)

    def kern(elo_ref, ehi_ref, act_ref, blk_ref, *refs):
        _experts_kernel(elo_ref, ehi_ref, act_ref, *refs, layer=layer)

    return pl.pallas_call(
        kern,
        out_shape=jax.ShapeDtypeStruct((n_sorted, D_MODEL), F32),
        grid_spec=pltpu.PrefetchScalarGridSpec(
            num_scalar_prefetch=4,
            grid=(n_sorted // MOE_TILE,),
            in_specs=[pl.BlockSpec((MOE_TILE, D_MODEL), row), hbm, hbm, hbm,
                      pl.BlockSpec((D_MODEL, 2 * LANES), full), pl.BlockSpec((D_MODEL, LANES), full)],
            out_specs=pl.BlockSpec((MOE_TILE, D_MODEL), row),
            scratch_shapes=[pltpu.VMEM((N_EXPERTS, D_MODEL, EXPERT_FF), BF16),
                            pltpu.VMEM((N_EXPERTS, D_MODEL, EXPERT_FF), BF16),
                            pltpu.VMEM((N_EXPERTS, EXPERT_FF, D_MODEL), BF16),
                            pltpu.SemaphoreType.DMA((3,))],
        ),
        compiler_params=_params(("arbitrary",)),
        name="moe_experts",
    )(tile_elo, tile_ehi, tile_act, tile_blk, sorted_h, wg, wu, wd, wr_cat, wr_hi)


def _pull_kernel(pos_ref, y_ref, x_ref, g2_ref, nf_ref, o_ref, buf_ref, sems, *, final_norm):
    i = pl.program_id(0)
    nsteps = pl.num_programs(0)

    def fetch(step, slot, wait):
        for r in range(PULL_ROWS):
            cp = _row_copy(y_ref, pos_ref[step * PULL_ROWS + r], buf_ref.at[slot], r, sems.at[slot])
            if wait:
                cp.wait()
            else:
                cp.start(priority=r % 2)

    @pl.when(i == 0)
    def _():
        fetch(0, 0, False)

    for slot in range(2):
        @pl.when(i % 2 == slot)
        def _():
            @pl.when(i + 1 < nsteps)
            def _():
                fetch(i + 1, 1 - slot, False)

            fetch(i, slot, True)
            x = x_ref[...] + g2_ref[...] * buf_ref[slot]
            if final_norm:
                x = _rms(x, nf_ref[...])
            o_ref[...] = x


def _pull(pos, y_sorted, x, mod, norm_f, rows_per_cond, final_norm):
    n = x.shape[0]
    tm = PULL_ROWS
    return pl.pallas_call(
        functools.partial(_pull_kernel, final_norm=final_norm),
        out_shape=jax.ShapeDtypeStruct((n, D_MODEL), F32),
        grid_spec=pltpu.PrefetchScalarGridSpec(
            num_scalar_prefetch=1,
            grid=(n // tm,),
            in_specs=[pl.BlockSpec(memory_space=pl.ANY),
                      pl.BlockSpec((tm, D_MODEL), lambda i, pos: (i, 0)),
                      pl.BlockSpec((None, 1, D_MODEL), lambda i, pos: (i * tm // rows_per_cond, 0, 5)),
                      pl.BlockSpec((1, D_MODEL), lambda i, pos: (0, 0))],
            out_specs=pl.BlockSpec((tm, D_MODEL), lambda i, pos: (i, 0)),
            scratch_shapes=[pltpu.VMEM((2, tm, D_MODEL), F32), pltpu.SemaphoreType.DMA((2,))],
        ),
        compiler_params=_params(("arbitrary",)),
        name="moe_pull",
    )(pos, y_sorted, x, mod, norm_f)


def _routed_moe(x, h, info, mod, wg, wu, wd, wr_cat, wr_hi, layer, norm_f, rows_per_cond, final_norm):
    n = x.shape[0]
    n_sorted = n + N_BUCKETS * MOE_TILE
    n_tiles = n_sorted // MOE_TILE
    bk = info[:, 2].astype(jnp.int32)
    rank, counts = _plan(bk)
    padded = (counts + MOE_TILE - 1) // MOE_TILE * MOE_TILE
    ends = jnp.cumsum(padded)
    starts = ends - padded
    onehot = bk[:, None] == jnp.arange(N_BUCKETS, dtype=jnp.int32)[None, :]
    pos = jnp.sum(jnp.where(onehot, starts[None, :], 0), axis=1) + rank
    n_active = ends[-1] // MOE_TILE
    tile = jnp.arange(n_tiles, dtype=jnp.int32)
    tile_act = (tile < n_active).astype(jnp.int32)
    tile_blk = jnp.minimum(tile, n_active - 1)
    tile_bucket = jnp.minimum(jnp.sum(ends[None, :] <= (tile_blk * MOE_TILE)[:, None], axis=1), N_BUCKETS - 1)
    sorted_h = _push(pos, starts + counts, ends, h, n_sorted)
    y_sorted = _experts(tile_blk, jnp.asarray(_BUCKET_LO)[tile_bucket], jnp.asarray(_BUCKET_HI)[tile_bucket],
                        tile_act, sorted_h, wg, wu, wd, wr_cat, wr_hi, layer)
    return _pull(pos, y_sorted, x, mod, norm_f, rows_per_cond, final_norm)


def _swap_pairs(x):
    lane = lax.broadcasted_iota(jnp.int32, x.shape, 1)
    first = (lane % MLA_ROPE) < (MLA_ROPE // 2)
    return jnp.where(first, pltpu.roll(x, LANES - MLA_ROPE // 2, axis=1),
                     pltpu.roll(x, MLA_ROPE // 2, axis=1))


def _mla_proj_kernel(*refs, rope):
    x_ref, g_ref, sc_ref, sh_ref, win_ref, qn_ref, kvn_ref, wuq_ref = refs[:8]
    pos = 8
    if rope:
        cos_ref, sin_ref = refs[pos:pos + 2]
        pos += 2
    q_ref, ckv_ref, kpe_ref = refs[pos:pos + 3]

    h = _normmod(x_ref[...], g_ref[...], sc_ref[...], sh_ref[...]).astype(BF16)
    p = jnp.dot(h, win_ref[...], preferred_element_type=F32)
    cq = _rms(p[:, :MLA_Q_RANK], qn_ref[...]).astype(BF16)
    ckv_ref[...] = _rms(p[:, MLA_Q_RANK:MLA_Q_RANK + MLA_KV_RANK], kvn_ref[...])
    kpe = p[:, MLA_Q_RANK + MLA_KV_RANK:]
    q = jnp.dot(cq, wuq_ref[...], preferred_element_type=F32) * (
        (MLA_NOPE + MLA_ROPE) ** -0.5 * float(np.log2(np.e)))
    if rope:
        cos = cos_ref[...]
        sin = sin_ref[...]
        kpe = kpe * cos + _swap_pairs(kpe) * sin
    kpe_ref[...] = kpe
    for hd in range(MLA_HEADS):
        lo = hd * HEAD_W
        q_ref[:, lo:lo + LANES] = q[:, lo:lo + LANES].astype(BF16)
        qp = q[:, lo + LANES:lo + HEAD_W]
        if rope:
            qp = qp * cos + _swap_pairs(qp) * sin
        q_ref[:, lo + LANES:lo + HEAD_W] = qp.astype(BF16)


def _mla_proj(x, mod, norm_g, w_in, q_norm, kv_norm, w_uq, rows_per_cond, cos=None, sin=None):
    n = x.shape[0]
    tm = 512
    rope = cos is not None
    full = lambda i: (0, 0)
    nin = w_in.shape[1]
    in_specs = [
        pl.BlockSpec((tm, D_MODEL), lambda i: (i, 0)),
        pl.BlockSpec((1, D_MODEL), full),
        _mod_spec(1, tm, rows_per_cond, 1),
        _mod_spec(0, tm, rows_per_cond, 1),
        pl.BlockSpec((D_MODEL, nin), full),
        pl.BlockSpec((1, MLA_Q_RANK), full),
        pl.BlockSpec((1, MLA_KV_RANK), full),
        pl.BlockSpec((MLA_Q_RANK, MLA_HEADS * HEAD_W), full),
    ]
    args = [x, norm_g, mod, mod, w_in, q_norm, kv_norm, w_uq]
    if rope:
        nt = cos.shape[0] // tm
        tab = pl.BlockSpec((tm, LANES), lambda i: (i % nt, 0))
        in_specs += [tab, tab]
        args += [cos, sin]
    return pl.pallas_call(
        functools.partial(_mla_proj_kernel, rope=rope),
        out_shape=[jax.ShapeDtypeStruct((n, MLA_HEADS * HEAD_W), BF16),
                   jax.ShapeDtypeStruct((n, MLA_KV_RANK), F32),
                   jax.ShapeDtypeStruct((n, LANES), F32)],
        grid=(n // tm,),
        in_specs=in_specs,
        out_specs=[pl.BlockSpec((tm, MLA_HEADS * HEAD_W), lambda i: (i, 0)),
                   pl.BlockSpec((tm, MLA_KV_RANK), lambda i: (i, 0)),
                   pl.BlockSpec((tm, LANES), lambda i: (i, 0))],
        compiler_params=_params(("arbitrary",)),
        name="mla_proj",
    )(*args)


def _kv_up_kernel(ckv_ref, kpe_ref, wuk_ref, wuv_ref, k_ref, v_ref):
    c = ckv_ref[...].astype(BF16)
    kn = jnp.dot(c, wuk_ref[...], preferred_element_type=F32).astype(BF16)
    vv = jnp.dot(c, wuv_ref[...], preferred_element_type=F32).astype(BF16)
    kpe = kpe_ref[...].astype(BF16)
    lane = lax.broadcasted_iota(jnp.int32, kpe.shape, 1)
    ones_col = jnp.where(lane == 0, 1.0, 0.0).astype(BF16)
    for hd in range(MLA_HEADS):
        k_ref[:, hd * HEAD_W:hd * HEAD_W + LANES] = kn[:, hd * MLA_NOPE:(hd + 1) * MLA_NOPE]
        k_ref[:, hd * HEAD_W + LANES:(hd + 1) * HEAD_W] = kpe
        v_ref[:, hd * HEAD_W:hd * HEAD_W + LANES] = vv[:, hd * MLA_V:(hd + 1) * MLA_V]
        v_ref[:, hd * HEAD_W + LANES:(hd + 1) * HEAD_W] = ones_col


def _kv_up(ckv, kpe, w_uk, w_uv):
    n = ckv.shape[0]
    tk = 512
    full = lambda i: (0, 0)
    return pl.pallas_call(
        _kv_up_kernel,
        out_shape=[jax.ShapeDtypeStruct((n, MLA_HEADS * HEAD_W), BF16),
                   jax.ShapeDtypeStruct((n, MLA_HEADS * HEAD_W), BF16)],
        grid=(n // tk,),
        in_specs=[
            pl.BlockSpec((tk, MLA_KV_RANK), lambda i: (i, 0)),
            pl.BlockSpec((tk, LANES), lambda i: (i, 0)),
            pl.BlockSpec((MLA_KV_RANK, MLA_HEADS * MLA_NOPE), full),
            pl.BlockSpec((MLA_KV_RANK, MLA_HEADS * MLA_V), full),
        ],
        out_specs=[pl.BlockSpec((tk, MLA_HEADS * HEAD_W), lambda i: (i, 0)),
                   pl.BlockSpec((tk, MLA_HEADS * HEAD_W), lambda i: (i, 0))],
        compiler_params=_params(("arbitrary",)),
        name="kv_up",
    )(ckv, kpe, w_uk, w_uv)


def _attn_kernel(q_ref, k_ref, v_ref, o_ref, *, heads, seqs):
    tq = q_ref.shape[0] // seqs
    tk = k_ref.shape[0] // seqs
    for sq in range(seqs):
        for hd in range(heads):
            k = k_ref[sq * tk:(sq + 1) * tk, hd * HEAD_W:(hd + 1) * HEAD_W]
            v = v_ref[sq * tk:(sq + 1) * tk, hd * HEAD_W:(hd + 1) * HEAD_W]
            for r in range(tq // ATTN_SUB):
                rows = slice(sq * tq + r * ATTN_SUB, sq * tq + (r + 1) * ATTN_SUB)
                q = q_ref[rows, hd * HEAD_W:(hd + 1) * HEAD_W]
                s = lax.dot_general(q, k, (((1,), (1,)), ((), ())), preferred_element_type=F32)
                m = jnp.max(s, axis=-1, keepdims=True)
                p = jnp.exp2(s - m).astype(BF16)
                o = jnp.dot(p, v, preferred_element_type=F32)
                o_ref[rows, hd * MLA_V:(hd + 1) * MLA_V] = (
                    o[:, :MLA_V] / o[:, MLA_V:MLA_V + 1]).astype(o_ref.dtype)


def _attention(q, k, v, batch, tq_total, tk_total, heads_per_step, tq, seqs=1):
    assert seqs == 1 or tq == tq_total
    nq = tq_total // tq
    nh = MLA_HEADS // heads_per_step
    hp = heads_per_step
    return pl.pallas_call(
        functools.partial(_attn_kernel, heads=hp, seqs=seqs),
        out_shape=jax.ShapeDtypeStruct((batch * tq_total, MLA_HEADS * MLA_V), BF16),
        grid=(batch // seqs, nh, nq),
        in_specs=[
            pl.BlockSpec((seqs * tq, hp * HEAD_W), lambda b, h, i: (b * nq + i, h)),
            pl.BlockSpec((seqs * tk_total, hp * HEAD_W), lambda b, h, i: (b, h)),
            pl.BlockSpec((seqs * tk_total, hp * HEAD_W), lambda b, h, i: (b, h)),
        ],
        out_specs=pl.BlockSpec((seqs * tq, hp * MLA_V), lambda b, h, i: (b * nq + i, h)),
        compiler_params=_params(("arbitrary", "arbitrary", "arbitrary")),
        name="attn",
    )(q, k, v)


_ROPE_PERM = np.concatenate([np.arange(0, 16), np.arange(32, 48), np.arange(16, 32), np.arange(48, 64)])


def _rope_tables(seq):
    pos = jnp.arange(seq)
    row = (pos // GRID_W).astype(F32)
    col = (pos % GRID_W).astype(F32)
    half = MLA_ROPE // 2
    inv = ROPE_BASE ** (-jnp.arange(0, half, 2, dtype=F32) / half)
    ang = jnp.concatenate([row[:, None] * inv, col[:, None] * inv], axis=1)
    cos = jnp.cos(ang)
    sin = jnp.sin(ang)
    pad = jnp.zeros((seq, LANES - MLA_ROPE), F32)
    return (jnp.concatenate([cos, cos, pad], axis=1),
            jnp.concatenate([-sin, sin, pad], axis=1))


def _prep_mla_weights(w_in, w_uq, w_uk, w_uv):
    nq = MLA_Q_RANK + MLA_KV_RANK
    w_in_p = jnp.concatenate(
        [w_in[:, :nq], w_in[:, nq:][:, _ROPE_PERM], jnp.zeros((D_MODEL, LANES - MLA_ROPE), F32)], axis=1)
    wq = w_uq.reshape(MLA_Q_RANK, MLA_HEADS, MLA_NOPE + MLA_ROPE)
    wq_p = jnp.concatenate(
        [wq[:, :, :MLA_NOPE], wq[:, :, MLA_NOPE:][:, :, _ROPE_PERM],
         jnp.zeros((MLA_Q_RANK, MLA_HEADS, HEAD_W - MLA_NOPE - MLA_ROPE), F32)], axis=2)
    return (w_in_p.astype(BF16), wq_p.reshape(MLA_Q_RANK, MLA_HEADS * HEAD_W).astype(BF16),
            w_uk.reshape(MLA_KV_RANK, MLA_HEADS * MLA_NOPE).astype(BF16),
            w_uv.reshape(MLA_KV_RANK, MLA_HEADS * MLA_V).astype(BF16))


def kernel(x_prompt, x_sample, state_ret_fwd, state_ret_bwd, cache_mla_ckv, cache_mla_kpe, c, c_ctx,
           w_ada, b_ada, norm1, norm2, norm_final, ret_w_in, ret_decay, ret_gn, ret_w_out,
           mla_w_in, mla_q_norm, mla_kv_norm, mla_w_uq, mla_w_uk, mla_w_uv, mla_w_o,
           w_router, router_bias, moe_w_gate, moe_w_up, moe_w_down):
    bp, tp, _ = x_prompt.shape
    bs, ts, _ = x_sample.shape
    past = cache_mla_ckv.shape[2]
    xp = x_prompt.reshape(bp * tp, D_MODEL)
    xs = x_sample.reshape(bs * ts, D_MODEL)

    conds = jnp.concatenate([c_ctx[None, :], c, jnp.zeros((8 - 1 - bs, D_MODEL), F32)], axis=0)
    mods = _ada_all(conds, w_ada, b_ada)

    wr_pad = jnp.pad(w_router, ((0, 0), (0, LANES - N_EXPERTS)))
    rb_pad = jnp.pad(router_bias, (0, LANES - N_EXPERTS)).reshape(1, LANES)
    wr_hi = wr_pad.astype(BF16)
    wr_cat = jnp.concatenate([wr_hi, (wr_pad - wr_hi.astype(F32)).astype(BF16)], axis=1)
    nf = norm_final.reshape(1, D_MODEL)
    cos_t, sin_t = _rope_tables(ts)

    groups = {
        "p": dict(x=xp, rows_per_cond=bp * tp, lo=0, hi=1),
        "s": dict(x=xs, rows_per_cond=ts, lo=1, hi=1 + bs),
    }
    out_ckv, out_kpe = [], []
    ret_states = None
    n_ret = state_ret_fwd.shape[1]
    wg = moe_w_gate.astype(BF16)
    wu = moe_w_up.astype(BF16)
    wd = moe_w_down.astype(BF16)

    for i in range(DEPTH):
        j = i // 2
        n1 = norm1[i].reshape(1, D_MODEL)
        n2 = norm2[i].reshape(1, D_MODEL)
        mixed = {}
        if i % 2 == 0:
            w_in = ret_w_in[j].astype(BF16)
            w_out = ret_w_out[j].astype(BF16)
            dec = jnp.broadcast_to(ret_decay[j][:, :, None, None], (2, RET_HEADS, SCAN_CHUNK, SCAN_CHUNK))
            gn = ret_gn[j]
            for name, gr in groups.items():
                mod = mods[i, gr["lo"]:gr["hi"]].reshape(-1, 1, 6 * D_MODEL)
                proj = _proj(gr["x"], mod, n1, w_in, gr["rows_per_cond"],
                             2 * RET_HEADS * RET_DK + RET_HEADS * RET_DV)
                if name == "p":
                    y, *ret_states = _retention_scan(proj, dec, gn, bp, tp, state_slot=j,
                                                     state_bufs=ret_states, n_slots=n_ret)
                else:
                    (y,) = _retention_scan(proj, dec, gn, bs, ts, state_ret_fwd, state_ret_bwd, s0_slot=j)
                mixed[name] = (y, w_out, mod)
        else:
            w_in_p, w_uq_p, w_uk, w_uv = _prep_mla_weights(mla_w_in[j], mla_w_uq[j], mla_w_uk[j], mla_w_uv[j])
            w_o = mla_w_o[j].astype(BF16)
            qn = mla_q_norm[j].reshape(1, MLA_Q_RANK)
            kvn = mla_kv_norm[j].reshape(1, MLA_KV_RANK)
            for name, gr in groups.items():
                mod = mods[i, gr["lo"]:gr["hi"]].reshape(-1, 1, 6 * D_MODEL)
                if name == "p":
                    q, ckv, kpe = _mla_proj(gr["x"], mod, n1, w_in_p, qn, kvn, w_uq_p, gr["rows_per_cond"])
                    out_ckv.append(ckv.reshape(bp, tp, MLA_KV_RANK))
                    out_kpe.append(kpe[:, :MLA_ROPE][:, _ROPE_PERM].reshape(bp, tp, MLA_ROPE))
                    k, v = _kv_up(ckv, kpe, w_uk, w_uv)
                    o = _attention(q, k, v, bp, tp, tp, MLA_HEADS // 2, tp)
                else:
                    q, ckv, kpe = _mla_proj(gr["x"], mod, n1, w_in_p, qn, kvn, w_uq_p, gr["rows_per_cond"],
                                            cos_t, sin_t)
                    cache_kpe = jnp.pad(cache_mla_kpe[:, j][:, :, _ROPE_PERM],
                                        ((0, 0), (0, 0), (0, LANES - MLA_ROPE)))
                    ckv_all = jnp.concatenate([ckv.reshape(bs, ts, MLA_KV_RANK), cache_mla_ckv[:, j]], axis=1)
                    kpe_all = jnp.concatenate([kpe.reshape(bs, ts, LANES), cache_kpe], axis=1)
                    tk = ts + past
                    k, v = _kv_up(ckv_all.reshape(bs * tk, MLA_KV_RANK), kpe_all.reshape(bs * tk, LANES),
                                  w_uk, w_uv)
                    o = _attention(q, k, v, bs, ts, tk, 1, 2048)
                mixed[name] = (o, w_o, mod)

        for name, gr in groups.items():
            y, w_o, mod = mixed[name]
            x_new, h2, info = _mix_out(y, w_o, gr["x"], mod, n2, wr_pad, rb_pad, gr["rows_per_cond"])
            gr["x"] = _routed_moe(x_new, h2, info, mod, wg, wu, wd, wr_cat, wr_hi, i, nf, gr["rows_per_cond"],
                                  i == DEPTH - 1)

    return (groups["p"]["x"].reshape(bp, tp, D_MODEL), groups["s"]["x"].reshape(bs, ts, D_MODEL),
            ret_states[0], ret_states[1],
            jnp.stack(out_ckv, axis=1), jnp.stack(out_kpe, axis=1))
```

```python
import functools

import numpy as np
import jax
import jax.numpy as jnp
from jax import lax
from jax.experimental import pallas as pl
from jax.experimental.pallas import tpu as pltpu

F32 = jnp.float32
BF16 = jnp.bfloat16

D_MODEL = 1024
DEPTH = 4
GRID_W = 64
RET_HEADS = 4
RET_DK = 256
RET_DV = 512
SCAN_CHUNK = 256
MLA_HEADS = 8
MLA_Q_RANK = 512
MLA_KV_RANK = 256
MLA_NOPE = 128
MLA_ROPE = 64
MLA_V = 128
ROPE_BASE = 10000.0
N_EXPERTS = 16
N_GROUPS = 4
EXPERTS_PER_GROUP = N_EXPERTS // N_GROUPS
EXPERT_FF = 256
EPS = 1e-6

LANES = 128
HEAD_W = 2 * LANES
VMEM_LIMIT = 48 * 1024 * 1024
MIX_SUB = 256
ATTN_SUB = 256
PAIRS_PER_GROUP = EXPERTS_PER_GROUP * (EXPERTS_PER_GROUP - 1) // 2
N_BUCKETS = N_GROUPS * PAIRS_PER_GROUP
MOE_TILE = 256
PLAN_BLK = 256
PUSH_ROWS = 2048
PULL_ROWS = 512

_PAIRS = [(a, b) for a in range(EXPERTS_PER_GROUP) for b in range(a + 1, EXPERTS_PER_GROUP)]
_BUCKET_LO = np.array([g * EXPERTS_PER_GROUP + a for g in range(N_GROUPS) for a, _ in _PAIRS], np.int32)
_BUCKET_HI = np.array([g * EXPERTS_PER_GROUP + b for g in range(N_GROUPS) for _, b in _PAIRS], np.int32)


def _params(sem):
    return pltpu.CompilerParams(dimension_semantics=sem, vmem_limit_bytes=VMEM_LIMIT)


def _sigmoid(x):
    return 1.0 / (1.0 + jnp.exp(-x))


def _normmod(x, g, sc, sh):
    y = x * lax.rsqrt(jnp.mean(x * x, axis=-1, keepdims=True) + EPS)
    return (y * g) * (1.0 + sc) + sh


def _rms(x, g):
    return x * lax.rsqrt(jnp.mean(x * x, axis=-1, keepdims=True) + EPS) * g


def _ada_kernel(c_ref, w_ref, b_ref, o_ref):
    c = c_ref[...]
    s = (c * _sigmoid(c)).astype(BF16)
    o_ref[0] = jnp.dot(s, w_ref[0].astype(BF16), preferred_element_type=F32) + b_ref[0]


def _ada_all(conds, w_ada, b_ada):
    tn = 1536
    n6 = 6 * D_MODEL
    return pl.pallas_call(
        _ada_kernel,
        out_shape=jax.ShapeDtypeStruct((DEPTH, 8, n6), F32),
        grid=(DEPTH, n6 // tn),
        in_specs=[
            pl.BlockSpec((8, D_MODEL), lambda l, j: (0, 0)),
            pl.BlockSpec((1, D_MODEL, tn), lambda l, j: (l, 0, j)),
            pl.BlockSpec((1, 1, tn), lambda l, j: (l, 0, j)),
        ],
        out_specs=pl.BlockSpec((1, 8, tn), lambda l, j: (l, 0, j)),
        compiler_params=_params(("arbitrary", "arbitrary")),
        name="ada",
    )(conds, w_ada, b_ada.reshape(DEPTH, 1, n6))


def _mod_spec(which, tm, rows_per_cond, ngrid):
    if ngrid == 1:
        return pl.BlockSpec((None, 1, D_MODEL), lambda i: (i * tm // rows_per_cond, 0, which))
    return pl.BlockSpec((None, 1, D_MODEL), lambda i, j: (i * tm // rows_per_cond, 0, which))


def _silu(g):
    hg = 0.5 * g
    return hg + hg * jnp.tanh(hg)


def _proj_kernel(x_ref, g_ref, sc_ref, sh_ref, w_ref, o_ref, h_ref, *, silu_from):
    j = pl.program_id(1)

    @pl.when(j == 0)
    def _():
        h_ref[...] = _normmod(x_ref[...], g_ref[...], sc_ref[...], sh_ref[...]).astype(BF16)

    acc = jnp.dot(h_ref[...], w_ref[...], preferred_element_type=F32)
    o_ref[...] = jnp.where(j >= silu_from, _silu(acc), acc).astype(o_ref.dtype)


def _proj(x, mod, norm_g, w, rows_per_cond, silu_from_col):
    n = x.shape[0]
    nout = w.shape[1]
    tm, tn = 1024, 2048
    return pl.pallas_call(
        functools.partial(_proj_kernel, silu_from=silu_from_col // tn),
        out_shape=jax.ShapeDtypeStruct((n, nout), BF16),
        grid=(n // tm, nout // tn),
        in_specs=[
            pl.BlockSpec((tm, D_MODEL), lambda i, j: (i, 0)),
            pl.BlockSpec((1, D_MODEL), lambda i, j: (0, 0)),
            _mod_spec(1, tm, rows_per_cond, 2),
            _mod_spec(0, tm, rows_per_cond, 2),
            pl.BlockSpec((D_MODEL, tn), lambda i, j: (0, j)),
        ],
        out_specs=pl.BlockSpec((tm, tn), lambda i, j: (i, j)),
        scratch_shapes=[pltpu.VMEM((tm, D_MODEL), BF16)],
        compiler_params=_params(("arbitrary", "arbitrary")),
        name="ret_proj",
    )(x, norm_g, mod, mod, w)


def _scan_kernel(*refs, seq, has_s0, aliased):
    q_ref, k_ref, v_ref, gf_ref, gb_ref, dec_ref, gn_ref = refs[:7]
    pos = 7
    if has_s0:
        s0_refs = refs[pos:pos + 2]
        pos += 2
    if aliased:
        pos += 2
    y_ref = refs[pos]
    s_refs = refs[pos + 1:pos + 3]
    intra_ref, qdec_ref, kdec_ref, cdec_ref, yacc_ref = refs[pos + 3:pos + 8]

    C = SCAN_CHUNK
    nc = seq // C

    @pl.when(pl.program_id(1) == 0)
    def _():
        ri = lax.broadcasted_iota(jnp.int32, (C, C), 0).astype(F32)
        ci = lax.broadcasted_iota(jnp.int32, (C, C), 1).astype(F32)
        k_scale = RET_DK ** -0.5
        for d in range(2):
            lg = -jnp.exp(dec_ref[d])
            if d == 1:
                diff, q_pow, k_pow = ci - ri, C - ri, ri
            else:
                diff, q_pow, k_pow = ri - ci, ri + 1.0, C - 1.0 - ri
            intra_ref[d] = jnp.where(diff >= 0, jnp.exp(lg * jnp.maximum(diff, 0.0)), 0.0) * k_scale
            qdec_ref[d] = jnp.exp(lg * q_pow)
            kdec_ref[d] = jnp.exp(lg * k_pow) * k_scale
            cdec_ref[d] = jnp.exp(lg[0:8, :] * float(C))

    state_is_zero = not has_s0 and nc == 1
    for d in range(2):
        if has_s0:
            s_refs[d][...] = s0_refs[d][...]
        elif not state_is_zero:
            s_refs[d][...] = jnp.zeros_like(s_refs[d])

    def lanes(a, width):
        return jnp.concatenate([a] * (width // C), axis=1)

    def contribution(rows, d, att):
        g_ref = (gf_ref, gb_ref)[d]
        s_ref = s_refs[d]
        q = q_ref[rows, :]
        k = k_ref[rows, :]
        v = v_ref[rows, :]
        att = (att * intra_ref[d]).astype(BF16)
        o = jnp.dot(att, v, preferred_element_type=F32)
        kd = (k.astype(F32) * lanes(kdec_ref[d], RET_DK)).astype(BF16)
        s_new = lax.dot_general(kd, v, (((0,), (0,)), ((), ())), preferred_element_type=F32)
        if state_is_zero:
            s_ref[...] = s_new
        else:
            qd = (q.astype(F32) * lanes(qdec_ref[d], RET_DK)).astype(BF16)
            s_old = s_ref[...]
            o = o + jnp.dot(qd, s_old.astype(BF16), preferred_element_type=F32)
            s_ref[...] = s_old * lanes(cdec_ref[d][0:1, :], RET_DV) + s_new
        mu = jnp.mean(o, axis=-1, keepdims=True)
        dlt = o - mu
        var = jnp.mean(dlt * dlt, axis=-1, keepdims=True)
        yn = (dlt * lax.rsqrt(var + EPS)) * gn_ref[d:d + 1, :]
        return g_ref[rows, :].astype(F32) * yn

    def scores(rows):
        return lax.dot_general(q_ref[rows, :], k_ref[rows, :], (((1,), (1,)), ((), ())),
                               preferred_element_type=F32)

    if nc == 1:
        rows = pl.ds(0, C)
        att = scores(rows)
        y_ref[...] = (contribution(rows, 0, att) + contribution(rows, 1, att)).astype(y_ref.dtype)
    else:
        def pair(i, second_visit):
            for d, c in ((0, i), (1, nc - 1 - i)):
                rows = pl.ds(pl.multiple_of(c * C, C), C)
                contrib = contribution(rows, d, scores(rows))
                if second_visit:
                    y_ref[rows, :] = (yacc_ref[rows, :] + contrib).astype(y_ref.dtype)
                else:
                    yacc_ref[rows, :] = contrib

        def first(i, carry):
            pair(i, False)
            return carry

        def second(i, carry):
            pair(i, True)
            return carry

        lax.fori_loop(0, nc // 2, first, 0)
        lax.fori_loop(nc // 2, nc, second, 0)


def _retention_scan(proj, dec, gn, batch, seq, s0f=None, s0b=None, s0_slot=0, state_slot=None,
                    state_bufs=None, n_slots=1):
    has_s0 = s0f is not None
    want_state = state_slot is not None
    aliased = state_bufs is not None
    hv = RET_HEADS * RET_DV
    nkb = RET_HEADS
    nvb = 2 * RET_HEADS * RET_DK // RET_DV
    C = SCAN_CHUNK
    assert seq % C == 0 and (seq == C or (seq // C) % 2 == 0)
    in_specs = [
        pl.BlockSpec((seq, RET_DK), lambda h, b: (b, h)),
        pl.BlockSpec((seq, RET_DK), lambda h, b: (b, nkb + h)),
        pl.BlockSpec((seq, RET_DV), lambda h, b: (b, nvb + h)),
        pl.BlockSpec((seq, RET_DV), lambda h, b: (b, nvb + RET_HEADS + h)),
        pl.BlockSpec((seq, RET_DV), lambda h, b: (b, nvb + 2 * RET_HEADS + h)),
        pl.BlockSpec((2, None, C, C), lambda h, b: (0, h, 0, 0)),
        pl.BlockSpec((2, RET_DV), lambda h, b: (0, h)),
    ]
    args = [proj, proj, proj, proj, proj, dec, gn]
    if has_s0:
        st_spec = pl.BlockSpec((None, None, None, RET_DK, RET_DV), lambda h, b: (b, s0_slot, h, 0, 0))
        in_specs += [st_spec, st_spec]
        args += [s0f, s0b]
    aliases = {}
    if aliased:
        in_specs += [pl.BlockSpec(memory_space=pl.ANY)] * 2
        aliases = {len(args): 1, len(args) + 1: 2}
        args += list(state_bufs)
    out_shape = [jax.ShapeDtypeStruct((batch * seq, hv), BF16)]
    out_specs = [pl.BlockSpec((seq, RET_DV), lambda h, b: (b, h))]
    scratch = [pltpu.VMEM((2, C, C), F32), pltpu.VMEM((2, C, C), F32), pltpu.VMEM((2, C, C), F32),
               pltpu.VMEM((2, 8, C), F32), pltpu.VMEM((seq, RET_DV), F32)]
    if want_state:
        st = jax.ShapeDtypeStruct((batch, n_slots, RET_HEADS, RET_DK, RET_DV), F32)
        out_shape += [st, st]
        st_spec = pl.BlockSpec((None, None, None, RET_DK, RET_DV), lambda h, b: (b, state_slot, h, 0, 0))
        out_specs += [st_spec, st_spec]
    else:
        scratch = [pltpu.VMEM((RET_DK, RET_DV), F32)] * 2 + scratch
    return pl.pallas_call(
        functools.partial(_scan_kernel, seq=seq, has_s0=has_s0, aliased=aliased),
        out_shape=out_shape,
        grid=(RET_HEADS, batch),
        in_specs=in_specs,
        out_specs=out_specs,
        scratch_shapes=scratch,
        input_output_aliases=aliases,
        compiler_params=_params(("arbitrary", "arbitrary")),
        name="ret_scan",
    )(*args)


def _route(logits, bias):
    s = _sigmoid(logits)
    sb = s + bias
    lane = lax.broadcasted_iota(jnp.int32, sb.shape, 1)
    lane_f = lane.astype(F32)
    neg = -jnp.inf
    big = float(LANES)

    def top2(vals):
        m1 = jnp.max(vals, axis=-1, keepdims=True)
        i1 = jnp.min(jnp.where(vals == m1, lane_f, big), axis=-1, keepdims=True)
        vals2 = jnp.where(lane_f == i1, neg, vals)
        m2 = jnp.max(vals2, axis=-1, keepdims=True)
        i2 = jnp.min(jnp.where(vals2 == m2, lane_f, big), axis=-1, keepdims=True)
        return m1 + m2, i1, i2

    best = None
    for g in range(N_GROUPS):
        in_group = (lane // EXPERTS_PER_GROUP) == g
        gs, i1, i2 = top2(jnp.where(in_group, sb, neg))
        if best is None:
            best = (gs, i1, i2)
        else:
            take = gs > best[0]
            best = (jnp.where(take, gs, best[0]), jnp.where(take, i1, best[1]),
                    jnp.where(take, i2, best[2]))
    _, e1, e2 = best
    w1 = jnp.sum(jnp.where(lane_f == e1, s, 0.0), axis=-1, keepdims=True)
    w2 = jnp.sum(jnp.where(lane_f == e2, s, 0.0), axis=-1, keepdims=True)
    den = w1 + w2
    grp = jnp.floor(e1 * (1.0 / EXPERTS_PER_GROUP))
    first_is_lo = e1 < e2
    lo = jnp.where(first_is_lo, e1, e2) - grp * EXPERTS_PER_GROUP
    hi = jnp.where(first_is_lo, e2, e1) - grp * EXPERTS_PER_GROUP
    bucket = grp * PAIRS_PER_GROUP + 3.0 * lo - 0.5 * lo * (lo - 1.0) + hi - lo - 1.0
    w_lo = jnp.where(first_is_lo, w1, w2) / den
    w_hi = jnp.where(first_is_lo, w2, w1) / den
    return (jnp.where(lane == 0, w_lo, 0.0) + jnp.where(lane == 1, w_hi, 0.0)
            + jnp.where(lane == 2, bucket, 0.0))


def _split_bf16(a):
    hi = a.astype(BF16)
    lo = (a - hi.astype(F32)).astype(BF16)
    return hi, lo


def _mix_out_kernel(y_ref, w_ref, x_ref, g1_ref, n2_ref, sc_ref, sh_ref, wr_ref, rb_ref,
                    xo_ref, h_ref, info_ref):
    w_hi, w_lo = _split_bf16(wr_ref[...])
    w_cat = jnp.concatenate([w_hi, w_lo], axis=1)
    for r in range(y_ref.shape[0] // MIX_SUB):
        rows = slice(r * MIX_SUB, (r + 1) * MIX_SUB)
        x = x_ref[rows, :] + g1_ref[...] * jnp.dot(y_ref[rows, :], w_ref[...], preferred_element_type=F32)
        xo_ref[rows, :] = x
        h = _normmod(x, n2_ref[...], sc_ref[...], sh_ref[...])
        h_ref[rows, :] = h
        h_hi, h_lo = _split_bf16(h)
        both = jnp.dot(h_hi, w_cat, preferred_element_type=F32)
        logits = both[:, :LANES] + both[:, LANES:] + jnp.dot(h_lo, w_hi, preferred_element_type=F32)
        info_ref[rows, :] = _route(logits, rb_ref[...])


def _mix_out(y, w, x, mod, norm2_g, wr_pad, rb_pad, rows_per_cond):
    n, kdim = y.shape
    tm = 512
    full = lambda i: (0, 0)
    return pl.pallas_call(
        _mix_out_kernel,
        out_shape=[jax.ShapeDtypeStruct((n, D_MODEL), F32),
                   jax.ShapeDtypeStruct((n, D_MODEL), F32),
                   jax.ShapeDtypeStruct((n, LANES), F32)],
        grid=(n // tm,),
        in_specs=[
            pl.BlockSpec((tm, kdim), lambda i: (i, 0)),
            pl.BlockSpec((kdim, D_MODEL), full),
            pl.BlockSpec((tm, D_MODEL), lambda i: (i, 0)),
            _mod_spec(2, tm, rows_per_cond, 1),
            pl.BlockSpec((1, D_MODEL), full),
            _mod_spec(4, tm, rows_per_cond, 1),
            _mod_spec(3, tm, rows_per_cond, 1),
            pl.BlockSpec((D_MODEL, LANES), full),
            pl.BlockSpec((1, LANES), full),
        ],
        out_specs=[pl.BlockSpec((tm, D_MODEL), lambda i: (i, 0)),
                   pl.BlockSpec((tm, D_MODEL), lambda i: (i, 0)),
                   pl.BlockSpec((tm, LANES), lambda i: (i, 0))],
        compiler_params=_params(("arbitrary",)),
        name="mix_out",
    )(y, w, x, mod, norm2_g, mod, mod, wr_pad, rb_pad)


def _plan_kernel(bk_ref, rank_ref, cnt_ref, carry_ref):
    i = pl.program_id(0)

    @pl.when(i == 0)
    def _():
        carry_ref[...] = jnp.zeros_like(carry_ref)

    b = bk_ref[0]
    sub = lax.broadcasted_iota(jnp.int32, (LANES, PLAN_BLK), 0)
    onehot = jnp.where(sub == b, 1.0, 0.0)
    ti = lax.broadcasted_iota(jnp.int32, (PLAN_BLK, PLAN_BLK), 0)
    tj = lax.broadcasted_iota(jnp.int32, (PLAN_BLK, PLAN_BLK), 1)
    upper = jnp.where(ti <= tj, 1.0, 0.0).astype(BF16)
    running = jnp.dot(onehot.astype(BF16), upper, preferred_element_type=F32) + carry_ref[...]
    rank_ref[0] = (jnp.sum(onehot * running, axis=0, keepdims=True) - 1.0).astype(jnp.int32)
    carry_ref[...] += jnp.sum(onehot, axis=1, keepdims=True)
    cnt_ref[...] = jnp.broadcast_to(carry_ref[...], cnt_ref.shape)


def _plan(bk):
    n = bk.shape[0]
    nb = n // PLAN_BLK
    rank, cnt = pl.pallas_call(
        _plan_kernel,
        out_shape=[jax.ShapeDtypeStruct((nb, 1, PLAN_BLK), jnp.int32),
                   jax.ShapeDtypeStruct((LANES, LANES), F32)],
        grid=(nb,),
        in_specs=[pl.BlockSpec((1, 1, PLAN_BLK), lambda i: (i, 0, 0))],
        out_specs=[pl.BlockSpec((1, 1, PLAN_BLK), lambda i: (i, 0, 0)),
                   pl.BlockSpec((LANES, LANES), lambda i: (0, 0))],
        scratch_shapes=[pltpu.VMEM((LANES, 1), F32)],
        compiler_params=_params(("arbitrary",)),
        name="moe_plan",
    )(bk.reshape(nb, 1, PLAN_BLK))
    return rank.reshape(n), cnt[:N_BUCKETS, 0].astype(jnp.int32)


def _row_copy(src_ref, src_row, dst_ref, dst_row, sem):
    return pltpu.make_async_copy(src_ref.at[pl.ds(src_row, 1), :], dst_ref.at[pl.ds(dst_row, 1), :], sem)


def _push_kernel(pos_ref, pad_lo_ref, pad_hi_ref, h_ref, out_ref, zero_ref, sem, zsem):
    i = pl.program_id(0)
    base = i * PUSH_ROWS

    copies = [_row_copy(h_ref, r, out_ref, pos_ref[base + r], sem) for r in range(PUSH_ROWS)]
    for r, cp in enumerate(copies):
        cp.start(priority=r % 2)
    for cp in copies:
        cp.wait()

    @pl.when(i == pl.num_programs(0) - 1)
    def _():
        zero_ref[...] = jnp.zeros_like(zero_ref)
        for k in range(N_BUCKETS):
            def zfill(p, carry):
                _row_copy(zero_ref, 0, out_ref, p, zsem).start()
                return carry

            def zwait(p, carry):
                _row_copy(zero_ref, 0, out_ref, p, zsem).wait()
                return carry

            lax.fori_loop(pad_lo_ref[k], pad_hi_ref[k], zfill, 0)
            lax.fori_loop(pad_lo_ref[k], pad_hi_ref[k], zwait, 0)


def _push(pos, pad_lo, pad_hi, h, n_sorted):
    n = h.shape[0]
    return pl.pallas_call(
        _push_kernel,
        out_shape=jax.ShapeDtypeStruct((n_sorted, D_MODEL), F32),
        grid_spec=pltpu.PrefetchScalarGridSpec(
            num_scalar_prefetch=3,
            grid=(n // PUSH_ROWS,),
            in_specs=[pl.BlockSpec((PUSH_ROWS, D_MODEL), lambda i, pos, lo, hi: (i, 0))],
            out_specs=pl.BlockSpec(memory_space=pl.ANY),
            scratch_shapes=[pltpu.VMEM((8, D_MODEL), F32), pltpu.SemaphoreType.DMA(()),
                            pltpu.SemaphoreType.DMA(())],
        ),
        compiler_params=_params(("arbitrary",)),
        name="moe_push",
    )(pos, pad_lo, pad_hi, h)


def _experts_kernel(elo_ref, ehi_ref, act_ref, s_ref, wg_hbm, wu_hbm, wd_hbm, wrc_ref, wrh_ref,
                    y_ref, wg_ref, wu_ref, wd_ref, wsem, *, layer):
    i = pl.program_id(0)

    @pl.when(i == 0)
    def _():
        loads = [pltpu.make_async_copy(src.at[layer], dst, wsem.at[k])
                 for k, (src, dst) in enumerate(((wg_hbm, wg_ref), (wu_hbm, wu_ref), (wd_hbm, wd_ref)))]
        for cp in loads:
            cp.start()
        for cp in loads:
            cp.wait()

    @pl.when(act_ref[i] == 1)
    def _():
        _expert_pair(s_ref[...], elo_ref[i], ehi_ref[i], wg_ref, wu_ref, wd_ref, wrc_ref, wrh_ref, y_ref)


def _expert_pair(h, e_lo, e_hi, wg_ref, wu_ref, wd_ref, wrc_ref, wrh_ref, y_ref):
    h_hi, h_lo = _split_bf16(h)
    both = jnp.dot(h_hi, wrc_ref[...], preferred_element_type=F32)
    logits = both[:, :LANES] + both[:, LANES:] + jnp.dot(h_lo, wrh_ref[...], preferred_element_type=F32)
    s = _sigmoid(logits)
    lane = lax.broadcasted_iota(jnp.int32, s.shape, 1)
    s_lo = jnp.sum(jnp.where(lane == e_lo, s, 0.0), axis=-1, keepdims=True)
    s_hi = jnp.sum(jnp.where(lane == e_hi, s, 0.0), axis=-1, keepdims=True)
    den = s_lo + s_hi
    y = None
    for e, w in ((e_lo, s_lo / den), (e_hi, s_hi / den)):
        a = jnp.dot(h_hi, wg_ref[e], preferred_element_type=F32)
        u = jnp.dot(h_hi, wu_ref[e], preferred_element_type=F32)
        act = (_silu(a) * u * w).astype(BF16)
        part = jnp.dot(act, wd_ref[e], preferred_element_type=F32)
        y = part if y is None else y + part
    y_ref[...] = y


def _experts(tile_blk, tile_elo, tile_ehi, tile_act, sorted_h, wg, wu, wd, wr_cat, wr_hi, layer):
    n_sorted = sorted_h.shape[0]
    row = lambda i, elo, ehi, act, blk: (blk[i], 0)
    full = lambda i, elo, ehi, act, blk: (0, 0)
    hbm = pl.BlockSpec(memory_space=pl.ANY)

    def kern(elo_ref, ehi_ref, act_ref, blk_ref, *refs):
        _experts_kernel(elo_ref, ehi_ref, act_ref, *refs, layer=layer)

    return pl.pallas_call(
        kern,
        out_shape=jax.ShapeDtypeStruct((n_sorted, D_MODEL), F32),
        grid_spec=pltpu.PrefetchScalarGridSpec(
            num_scalar_prefetch=4,
            grid=(n_sorted // MOE_TILE,),
            in_specs=[pl.BlockSpec((MOE_TILE, D_MODEL), row), hbm, hbm, hbm,
                      pl.BlockSpec((D_MODEL, 2 * LANES), full), pl.BlockSpec((D_MODEL, LANES), full)],
            out_specs=pl.BlockSpec((MOE_TILE, D_MODEL), row),
            scratch_shapes=[pltpu.VMEM((N_EXPERTS, D_MODEL, EXPERT_FF), BF16),
                            pltpu.VMEM((N_EXPERTS, D_MODEL, EXPERT_FF), BF16),
                            pltpu.VMEM((N_EXPERTS, EXPERT_FF, D_MODEL), BF16),
                            pltpu.SemaphoreType.DMA((3,))],
        ),
        compiler_params=_params(("arbitrary",)),
        name="moe_experts",
    )(tile_elo, tile_ehi, tile_act, tile_blk, sorted_h, wg, wu, wd, wr_cat, wr_hi)


def _pull_kernel(pos_ref, y_ref, x_ref, g2_ref, nf_ref, o_ref, buf_ref, sems, *, final_norm):
    i = pl.program_id(0)
    nsteps = pl.num_programs(0)

    def fetch(step, slot, wait):
        for r in range(PULL_ROWS):
            cp = _row_copy(y_ref, pos_ref[step * PULL_ROWS + r], buf_ref.at[slot], r, sems.at[slot])
            if wait:
                cp.wait()
            else:
                cp.start(priority=r % 2)

    @pl.when(i == 0)
    def _():
        fetch(0, 0, False)

    for slot in range(2):
        @pl.when(i % 2 == slot)
        def _():
            @pl.when(i + 1 < nsteps)
            def _():
                fetch(i + 1, 1 - slot, False)

            fetch(i, slot, True)
            x = x_ref[...] + g2_ref[...] * buf_ref[slot]
            if final_norm:
                x = _rms(x, nf_ref[...])
            o_ref[...] = x


def _pull(pos, y_sorted, x, mod, norm_f, rows_per_cond, final_norm):
    n = x.shape[0]
    tm = PULL_ROWS
    return pl.pallas_call(
        functools.partial(_pull_kernel, final_norm=final_norm),
        out_shape=jax.ShapeDtypeStruct((n, D_MODEL), F32),
        grid_spec=pltpu.PrefetchScalarGridSpec(
            num_scalar_prefetch=1,
            grid=(n // tm,),
            in_specs=[pl.BlockSpec(memory_space=pl.ANY),
                      pl.BlockSpec((tm, D_MODEL), lambda i, pos: (i, 0)),
                      pl.BlockSpec((None, 1, D_MODEL), lambda i, pos: (i * tm // rows_per_cond, 0, 5)),
                      pl.BlockSpec((1, D_MODEL), lambda i, pos: (0, 0))],
            out_specs=pl.BlockSpec((tm, D_MODEL), lambda i, pos: (i, 0)),
            scratch_shapes=[pltpu.VMEM((2, tm, D_MODEL), F32), pltpu.SemaphoreType.DMA((2,))],
        ),
        compiler_params=_params(("arbitrary",)),
        name="moe_pull",
    )(pos, y_sorted, x, mod, norm_f)


def _routed_moe(x, h, info, mod, wg, wu, wd, wr_cat, wr_hi, layer, norm_f, rows_per_cond, final_norm):
    n = x.shape[0]
    n_sorted = n + N_BUCKETS * MOE_TILE
    n_tiles = n_sorted // MOE_TILE
    bk = info[:, 2].astype(jnp.int32)
    rank, counts = _plan(bk)
    padded = (counts + MOE_TILE - 1) // MOE_TILE * MOE_TILE
    ends = jnp.cumsum(padded)
    starts = ends - padded
    onehot = bk[:, None] == jnp.arange(N_BUCKETS, dtype=jnp.int32)[None, :]
    pos = jnp.sum(jnp.where(onehot, starts[None, :], 0), axis=1) + rank
    n_active = ends[-1] // MOE_TILE
    tile = jnp.arange(n_tiles, dtype=jnp.int32)
    tile_act = (tile < n_active).astype(jnp.int32)
    tile_blk = jnp.minimum(tile, n_active - 1)
    tile_bucket = jnp.minimum(jnp.sum(ends[None, :] <= (tile_blk * MOE_TILE)[:, None], axis=1), N_BUCKETS - 1)
    sorted_h = _push(pos, starts + counts, ends, h, n_sorted)
    y_sorted = _experts(tile_blk, jnp.asarray(_BUCKET_LO)[tile_bucket], jnp.asarray(_BUCKET_HI)[tile_bucket],
                        tile_act, sorted_h, wg, wu, wd, wr_cat, wr_hi, layer)
    return _pull(pos, y_sorted, x, mod, norm_f, rows_per_cond, final_norm)


def _swap_pairs(x):
    lane = lax.broadcasted_iota(jnp.int32, x.shape, 1)
    first = (lane % MLA_ROPE) < (MLA_ROPE // 2)
    return jnp.where(first, pltpu.roll(x, LANES - MLA_ROPE // 2, axis=1),
                     pltpu.roll(x, MLA_ROPE // 2, axis=1))


def _mla_proj_kernel(*refs, rope):
    x_ref, g_ref, sc_ref, sh_ref, win_ref, qn_ref, kvn_ref, wuq_ref = refs[:8]
    pos = 8
    if rope:
        cos_ref, sin_ref = refs[pos:pos + 2]
        pos += 2
    q_ref, ckv_ref, kpe_ref = refs[pos:pos + 3]

    h = _normmod(x_ref[...], g_ref[...], sc_ref[...], sh_ref[...]).astype(BF16)
    p = jnp.dot(h, win_ref[...], preferred_element_type=F32)
    cq = _rms(p[:, :MLA_Q_RANK], qn_ref[...]).astype(BF16)
    ckv_ref[...] = _rms(p[:, MLA_Q_RANK:MLA_Q_RANK + MLA_KV_RANK], kvn_ref[...])
    kpe = p[:, MLA_Q_RANK + MLA_KV_RANK:]
    q = jnp.dot(cq, wuq_ref[...], preferred_element_type=F32) * (
        (MLA_NOPE + MLA_ROPE) ** -0.5 * float(np.log2(np.e)))
    if rope:
        cos = cos_ref[...]
        sin = sin_ref[...]
        kpe = kpe * cos + _swap_pairs(kpe) * sin
    kpe_ref[...] = kpe
    for hd in range(MLA_HEADS):
        lo = hd * HEAD_W
        q_ref[:, lo:lo + LANES] = q[:, lo:lo + LANES].astype(BF16)
        qp = q[:, lo + LANES:lo + HEAD_W]
        if rope:
            qp = qp * cos + _swap_pairs(qp) * sin
        q_ref[:, lo + LANES:lo + HEAD_W] = qp.astype(BF16)


def _mla_proj(x, mod, norm_g, w_in, q_norm, kv_norm, w_uq, rows_per_cond, cos=None, sin=None):
    n = x.shape[0]
    tm = 512
    rope = cos is not None
    full = lambda i: (0, 0)
    nin = w_in.shape[1]
    in_specs = [
        pl.BlockSpec((tm, D_MODEL), lambda i: (i, 0)),
        pl.BlockSpec((1, D_MODEL), full),
        _mod_spec(1, tm, rows_per_cond, 1),
        _mod_spec(0, tm, rows_per_cond, 1),
        pl.BlockSpec((D_MODEL, nin), full),
        pl.BlockSpec((1, MLA_Q_RANK), full),
        pl.BlockSpec((1, MLA_KV_RANK), full),
        pl.BlockSpec((MLA_Q_RANK, MLA_HEADS * HEAD_W), full),
    ]
    args = [x, norm_g, mod, mod, w_in, q_norm, kv_norm, w_uq]
    if rope:
        nt = cos.shape[0] // tm
        tab = pl.BlockSpec((tm, LANES), lambda i: (i % nt, 0))
        in_specs += [tab, tab]
        args += [cos, sin]
    return pl.pallas_call(
        functools.partial(_mla_proj_kernel, rope=rope),
        out_shape=[jax.ShapeDtypeStruct((n, MLA_HEADS * HEAD_W), BF16),
                   jax.ShapeDtypeStruct((n, MLA_KV_RANK), F32),
                   jax.ShapeDtypeStruct((n, LANES), F32)],
        grid=(n // tm,),
        in_specs=in_specs,
        out_specs=[pl.BlockSpec((tm, MLA_HEADS * HEAD_W), lambda i: (i, 0)),
                   pl.BlockSpec((tm, MLA_KV_RANK), lambda i: (i, 0)),
                   pl.BlockSpec((tm, LANES), lambda i: (i, 0))],
        compiler_params=_params(("arbitrary",)),
        name="mla_proj",
    )(*args)


def _kv_up_kernel(ckv_ref, kpe_ref, wuk_ref, wuv_ref, k_ref, v_ref):
    c = ckv_ref[...].astype(BF16)
    kn = jnp.dot(c, wuk_ref[...], preferred_element_type=F32).astype(BF16)
    vv = jnp.dot(c, wuv_ref[...], preferred_element_type=F32).astype(BF16)
    kpe = kpe_ref[...].astype(BF16)
    lane = lax.broadcasted_iota(jnp.int32, kpe.shape, 1)
    ones_col = jnp.where(lane == 0, 1.0, 0.0).astype(BF16)
    for hd in range(MLA_HEADS):
        k_ref[:, hd * HEAD_W:hd * HEAD_W + LANES] = kn[:, hd * MLA_NOPE:(hd + 1) * MLA_NOPE]
        k_ref[:, hd * HEAD_W + LANES:(hd + 1) * HEAD_W] = kpe
        v_ref[:, hd * HEAD_W:hd * HEAD_W + LANES] = vv[:, hd * MLA_V:(hd + 1) * MLA_V]
        v_ref[:, hd * HEAD_W + LANES:(hd + 1) * HEAD_W] = ones_col


def _kv_up(ckv, kpe, w_uk, w_uv):
    n = ckv.shape[0]
    tk = 512
    full = lambda i: (0, 0)
    return pl.pallas_call(
        _kv_up_kernel,
        out_shape=[jax.ShapeDtypeStruct((n, MLA_HEADS * HEAD_W), BF16),
                   jax.ShapeDtypeStruct((n, MLA_HEADS * HEAD_W), BF16)],
        grid=(n // tk,),
        in_specs=[
            pl.BlockSpec((tk, MLA_KV_RANK), lambda i: (i, 0)),
            pl.BlockSpec((tk, LANES), lambda i: (i, 0)),
            pl.BlockSpec((MLA_KV_RANK, MLA_HEADS * MLA_NOPE), full),
            pl.BlockSpec((MLA_KV_RANK, MLA_HEADS * MLA_V), full),
        ],
        out_specs=[pl.BlockSpec((tk, MLA_HEADS * HEAD_W), lambda i: (i, 0)),
                   pl.BlockSpec((tk, MLA_HEADS * HEAD_W), lambda i: (i, 0))],
        compiler_params=_params(("arbitrary",)),
        name="kv_up",
    )(ckv, kpe, w_uk, w_uv)


def _attn_kernel(q_ref, k_ref, v_ref, o_ref, *, heads, seqs):
    tq = q_ref.shape[0] // seqs
    tk = k_ref.shape[0] // seqs
    for sq in range(seqs):
        for hd in range(heads):
            k = k_ref[sq * tk:(sq + 1) * tk, hd * HEAD_W:(hd + 1) * HEAD_W]
            v = v_ref[sq * tk:(sq + 1) * tk, hd * HEAD_W:(hd + 1) * HEAD_W]
            for r in range(tq // ATTN_SUB):
                rows = slice(sq * tq + r * ATTN_SUB, sq * tq + (r + 1) * ATTN_SUB)
                q = q_ref[rows, hd * HEAD_W:(hd + 1) * HEAD_W]
                s = lax.dot_general(q, k, (((1,), (1,)), ((), ())), preferred_element_type=F32)
                m = jnp.max(s, axis=-1, keepdims=True)
                p = jnp.exp2(s - m).astype(BF16)
                o = jnp.dot(p, v, preferred_element_type=F32)
                o_ref[rows, hd * MLA_V:(hd + 1) * MLA_V] = (
                    o[:, :MLA_V] / o[:, MLA_V:MLA_V + 1]).astype(o_ref.dtype)


def _attention(q, k, v, batch, tq_total, tk_total, heads_per_step, tq, seqs=1):
    assert seqs == 1 or tq == tq_total
    nq = tq_total // tq
    nh = MLA_HEADS // heads_per_step
    hp = heads_per_step
    return pl.pallas_call(
        functools.partial(_attn_kernel, heads=hp, seqs=seqs),
        out_shape=jax.ShapeDtypeStruct((batch * tq_total, MLA_HEADS * MLA_V), BF16),
        grid=(batch // seqs, nh, nq),
        in_specs=[
            pl.BlockSpec((seqs * tq, hp * HEAD_W), lambda b, h, i: (b * nq + i, h)),
            pl.BlockSpec((seqs * tk_total, hp * HEAD_W), lambda b, h, i: (b, h)),
            pl.BlockSpec((seqs * tk_total, hp * HEAD_W), lambda b, h, i: (b, h)),
        ],
        out_specs=pl.BlockSpec((seqs * tq, hp * MLA_V), lambda b, h, i: (b * nq + i, h)),
        compiler_params=_params(("arbitrary", "arbitrary", "arbitrary")),
        name="attn",
    )(q, k, v)


_ROPE_PERM = np.concatenate([np.arange(0, 16), np.arange(32, 48), np.arange(16, 32), np.arange(48, 64)])


def _rope_tables(seq):
    pos = jnp.arange(seq)
    row = (pos // GRID_W).astype(F32)
    col = (pos % GRID_W).astype(F32)
    half = MLA_ROPE // 2
    inv = ROPE_BASE ** (-jnp.arange(0, half, 2, dtype=F32) / half)
    ang = jnp.concatenate([row[:, None] * inv, col[:, None] * inv], axis=1)
    cos = jnp.cos(ang)
    sin = jnp.sin(ang)
    pad = jnp.zeros((seq, LANES - MLA_ROPE), F32)
    return (jnp.concatenate([cos, cos, pad], axis=1),
            jnp.concatenate([-sin, sin, pad], axis=1))


def _prep_mla_weights(w_in, w_uq, w_uk, w_uv):
    nq = MLA_Q_RANK + MLA_KV_RANK
    w_in_p = jnp.concatenate(
        [w_in[:, :nq], w_in[:, nq:][:, _ROPE_PERM], jnp.zeros((D_MODEL, LANES - MLA_ROPE), F32)], axis=1)
    wq = w_uq.reshape(MLA_Q_RANK, MLA_HEADS, MLA_NOPE + MLA_ROPE)
    wq_p = jnp.concatenate(
        [wq[:, :, :MLA_NOPE], wq[:, :, MLA_NOPE:][:, :, _ROPE_PERM],
         jnp.zeros((MLA_Q_RANK, MLA_HEADS, HEAD_W - MLA_NOPE - MLA_ROPE), F32)], axis=2)
    return (w_in_p.astype(BF16), wq_p.reshape(MLA_Q_RANK, MLA_HEADS * HEAD_W).astype(BF16),
            w_uk.reshape(MLA_KV_RANK, MLA_HEADS * MLA_NOPE).astype(BF16),
            w_uv.reshape(MLA_KV_RANK, MLA_HEADS * MLA_V).astype(BF16))


def kernel(x_prompt, x_sample, state_ret_fwd, state_ret_bwd, cache_mla_ckv, cache_mla_kpe, c, c_ctx,
           w_ada, b_ada, norm1, norm2, norm_final, ret_w_in, ret_decay, ret_gn, ret_w_out,
           mla_w_in, mla_q_norm, mla_kv_norm, mla_w_uq, mla_w_uk, mla_w_uv, mla_w_o,
           w_router, router_bias, moe_w_gate, moe_w_up, moe_w_down):
    bp, tp, _ = x_prompt.shape
    bs, ts, _ = x_sample.shape
    past = cache_mla_ckv.shape[2]
    xp = x_prompt.reshape(bp * tp, D_MODEL)
    xs = x_sample.reshape(bs * ts, D_MODEL)

    conds = jnp.concatenate([c_ctx[None, :], c, jnp.zeros((8 - 1 - bs, D_MODEL), F32)], axis=0)
    mods = _ada_all(conds, w_ada, b_ada)

    wr_pad = jnp.pad(w_router, ((0, 0), (0, LANES - N_EXPERTS)))
    rb_pad = jnp.pad(router_bias, (0, LANES - N_EXPERTS)).reshape(1, LANES)
    wr_hi = wr_pad.astype(BF16)
    wr_cat = jnp.concatenate([wr_hi, (wr_pad - wr_hi.astype(F32)).astype(BF16)], axis=1)
    nf = norm_final.reshape(1, D_MODEL)
    cos_t, sin_t = _rope_tables(ts)

    groups = {
        "p": dict(x=xp, rows_per_cond=bp * tp, lo=0, hi=1),
        "s": dict(x=xs, rows_per_cond=ts, lo=1, hi=1 + bs),
    }
    out_ckv, out_kpe = [], []
    ret_states = None
    n_ret = state_ret_fwd.shape[1]
    wg = moe_w_gate.astype(BF16)
    wu = moe_w_up.astype(BF16)
    wd = moe_w_down.astype(BF16)

    for i in range(DEPTH):
        j = i // 2
        n1 = norm1[i].reshape(1, D_MODEL)
        n2 = norm2[i].reshape(1, D_MODEL)
        mixed = {}
        if i % 2 == 0:
            w_in = ret_w_in[j].astype(BF16)
            w_out = ret_w_out[j].astype(BF16)
            dec = jnp.broadcast_to(ret_decay[j][:, :, None, None], (2, RET_HEADS, SCAN_CHUNK, SCAN_CHUNK))
            gn = ret_gn[j]
            for name, gr in groups.items():
                mod = mods[i, gr["lo"]:gr["hi"]].reshape(-1, 1, 6 * D_MODEL)
                proj = _proj(gr["x"], mod, n1, w_in, gr["rows_per_cond"],
                             2 * RET_HEADS * RET_DK + RET_HEADS * RET_DV)
                if name == "p":
                    y, *ret_states = _retention_scan(proj, dec, gn, bp, tp, state_slot=j,
                                                     state_bufs=ret_states, n_slots=n_ret)
                else:
                    (y,) = _retention_scan(proj, dec, gn, bs, ts, state_ret_fwd, state_ret_bwd, s0_slot=j)
                mixed[name] = (y, w_out, mod)
        else:
            w_in_p, w_uq_p, w_uk, w_uv = _prep_mla_weights(mla_w_in[j], mla_w_uq[j], mla_w_uk[j], mla_w_uv[j])
            w_o = mla_w_o[j].astype(BF16)
            qn = mla_q_norm[j].reshape(1, MLA_Q_RANK)
            kvn = mla_kv_norm[j].reshape(1, MLA_KV_RANK)
            for name, gr in groups.items():
                mod = mods[i, gr["lo"]:gr["hi"]].reshape(-1, 1, 6 * D_MODEL)
                if name == "p":
                    q, ckv, kpe = _mla_proj(gr["x"], mod, n1, w_in_p, qn, kvn, w_uq_p, gr["rows_per_cond"])
                    out_ckv.append(ckv.reshape(bp, tp, MLA_KV_RANK))
                    out_kpe.append(kpe[:, :MLA_ROPE][:, _ROPE_PERM].reshape(bp, tp, MLA_ROPE))
                    k, v = _kv_up(ckv, kpe, w_uk, w_uv)
                    o = _attention(q, k, v, bp, tp, tp, MLA_HEADS, tp)
                else:
                    q, ckv, kpe = _mla_proj(gr["x"], mod, n1, w_in_p, qn, kvn, w_uq_p, gr["rows_per_cond"],
                                            cos_t, sin_t)
                    cache_kpe = jnp.pad(cache_mla_kpe[:, j][:, :, _ROPE_PERM],
                                        ((0, 0), (0, 0), (0, LANES - MLA_ROPE)))
                    ckv_all = jnp.concatenate([ckv.reshape(bs, ts, MLA_KV_RANK), cache_mla_ckv[:, j]], axis=1)
                    kpe_all = jnp.concatenate([kpe.reshape(bs, ts, LANES), cache_kpe], axis=1)
                    tk = ts + past
                    k, v = _kv_up(ckv_all.reshape(bs * tk, MLA_KV_RANK), kpe_all.reshape(bs * tk, LANES),
                                  w_uk, w_uv)
                    o = _attention(q, k, v, bs, ts, tk, 4, 2048)
                mixed[name] = (o, w_o, mod)

        for name, gr in groups.items():
            y, w_o, mod = mixed[name]
            x_new, h2, info = _mix_out(y, w_o, gr["x"], mod, n2, wr_pad, rb_pad, gr["rows_per_cond"])
            gr["x"] = _routed_moe(x_new, h2, info, mod, wg, wu, wd, wr_cat, wr_hi, i, nf, gr["rows_per_cond"],
                                  i == DEPTH - 1)

    return (groups["p"]["x"].reshape(bp, tp, D_MODEL), groups["s"]["x"].reshape(bs, ts, D_MODEL),
            ret_states[0], ret_states[1],
            jnp.stack(out_ckv, axis=1), jnp.stack(out_kpe, axis=1))
```
